```python
import math
import jax
import jax.numpy as jnp
from jax import lax
import numpy as np

D_MODEL = 1024
BATCH = 8
SEQ = 2048
DEPTH = 4

GRID_W = 64
CTX_LEN = 256
N_EVEN = (DEPTH + 1) // 2
N_ODD = DEPTH // 2
N_VRES = max(N_ODD - 1, 0)

ALPHA = (2.0 * DEPTH) ** 0.25
BETA = (8.0 * DEPTH) ** -0.25
LN_EPS = 1e-6
ROPE_BASE = 10000.0

D_FF = ((8 * D_MODEL + 3 * 256 - 1) // (3 * 256)) * 256

DA_QK = 64
DA_V = 2 * DA_QK
DA_HEADS = (D_MODEL // 2) // DA_V
DA_WIDTH = DA_HEADS * DA_V
Q_BLOCK = 128
RT_QK = 128
RT_V = 128
RT_HEADS = (D_MODEL // 2) // RT_V
RT_WIDTH = RT_HEADS * RT_V
RT_CHUNK = 128
IN_SIZES = (DA_HEADS * 2 * DA_QK, DA_HEADS * 2 * DA_QK, DA_WIDTH,
            RT_HEADS * RT_QK, RT_HEADS * RT_QK, RT_WIDTH, RT_WIDTH)
IN_SPLITS = tuple(int(s) for s in np.cumsum(IN_SIZES)[:-1])
IN_COLS = sum(IN_SIZES)

RW_HEAD = 64
RW_HEADS = D_MODEL // RW_HEAD
RW_DECAY_LORA = max(32, int(round(1.8 * D_MODEL ** 0.5 / 32)) * 32)
RW_AAA_LORA = max(32, int(round(1.8 * D_MODEL ** 0.5 / 32)) * 32)
RW_MV_LORA = max(32, int(round(1.3 * D_MODEL ** 0.5 / 32)) * 32)
RW_GATE_LORA = max(32, int(round(0.6 * D_MODEL ** 0.8 / 32)) * 32)
RW_GN_EPS = 64e-5

kernel_name = "hybrid_diffattn_retention_rwkv7_prefix_dit"


def layer_norm(x, g, b):
    xf = x.astype(jnp.float32)
    mu = jnp.mean(xf, -1, keepdims=True)
    var = jnp.mean(jnp.square(xf - mu), -1, keepdims=True)
    return ((xf - mu) * lax.rsqrt(var + LN_EPS) * g + b).astype(x.dtype)


def heads_rms(x):
    xf = x.astype(jnp.float32)
    return xf * lax.rsqrt(jnp.mean(xf * xf, -1, keepdims=True) + LN_EPS)


def modulate(x, shift, scale):
    return x * (1.0 + scale) + shift


def swiglu(h, w1, w3, w2):
    return (jax.nn.silu(h @ w1) * (h @ w3)) @ w2


def axial_rope_angles(n, dim):
    t = jnp.arange(n)
    row = (t // GRID_W).astype(jnp.float32)
    col = (t % GRID_W).astype(jnp.float32)
    n_freq = dim // 4
    inv = ROPE_BASE ** (-jnp.arange(n_freq, dtype=jnp.float32) / n_freq)
    return jnp.concatenate([row[:, None] * inv, col[:, None] * inv], -1)


def apply_rope(x, ang):
    T = x.shape[1]
    d2 = x.shape[-1] // 2
    shape = (1, T) + (1,) * (x.ndim - 3) + (d2,)
    cos = jnp.cos(ang).reshape(shape).astype(x.dtype)
    sin = jnp.sin(ang).reshape(shape).astype(x.dtype)
    x1, x2 = x[..., :d2], x[..., d2:]
    return jnp.concatenate([x1 * cos - x2 * sin, x1 * sin + x2 * cos], -1)


def diff_scores(q, k, v, lam):
    s = jnp.einsum('bqhmd,bkhmd->bhmqk', q, k).astype(jnp.float32) * (DA_QK ** -0.5)
    p = jax.nn.softmax(s, axis=-1)
    w = p[:, :, 0] - lam * p[:, :, 1]
    return jnp.einsum('bhqk,bkhv->bqhv', w.astype(v.dtype), v)


def diff_attention_blocks(q, k_all, v_all, lam):
    B_, T = q.shape[:2]
    nb = T // Q_BLOCK
    qb = jnp.moveaxis(q.reshape((B_, nb, Q_BLOCK) + q.shape[2:]), 1, 0)
    ob = lax.map(lambda blk: diff_scores(blk, k_all, v_all, lam), qb)
    return jnp.moveaxis(ob, 0, 1).reshape(B_, T, DA_HEADS, DA_V)


def retention_chunkwise(q, k, v, gamma, s0):
    B_, T, H, dk = q.shape
    dv = v.shape[-1]
    C = RT_CHUNK
    n = T // C
    qc = q.reshape(B_, n, C, H, dk)
    kc = k.reshape(B_, n, C, H, dk)
    vc = v.reshape(B_, n, C, H, dv)
    log_g = jnp.log(gamma.astype(jnp.float32))
    idx = jnp.arange(C, dtype=jnp.float32)
    diff = idx[:, None] - idx[None, :]
    decay = jnp.where(diff >= 0, jnp.exp(log_g[:, None, None] * jnp.maximum(diff, 0.0)), 0.0)
    inner = jnp.einsum('bnqhd,bnkhd->bnhqk', qc, kc) * decay
    o_inner = jnp.einsum('bnhqk,bnkhv->bnqhv', inner, vc)
    zeta = jnp.exp(log_g[:, None] * (C - 1.0 - idx))
    u = jnp.einsum('bnkhd,hk,bnkhv->bnhdv', kc, zeta, vc)
    g_chunk = jnp.exp(log_g * C)[:, None, None]

    def step(s, u_i):
        return g_chunk * s + u_i, s

    s_fin, s_prev = lax.scan(step, s0, jnp.moveaxis(u, 1, 0))
    xi = jnp.exp(log_g[:, None] * (idx + 1.0))
    o_cross = jnp.einsum('bnqhd,nbhdv,hq->bnqhv', qc, s_prev, xi)
    return (o_inner + o_cross).reshape(B_, T, H, dv), s_fin


def bi_retention(q_lat, k_lat, v_lat, q_ctx, k_ctx, v_ctx, gam):
    B_ = q_lat.shape[0]
    zero = jnp.zeros((B_, RT_HEADS, RT_QK, RT_V), jnp.float32)
    fl = lambda t: jnp.flip(t, axis=1)
    c_f, s_f = retention_chunkwise(q_ctx, k_ctx, v_ctx, gam[0], zero)
    c_b, s_b = retention_chunkwise(fl(q_ctx), fl(k_ctx), fl(v_ctx), gam[1], zero)
    l_f, _ = retention_chunkwise(q_lat, k_lat, v_lat, gam[0], s_f)
    l_b, _ = retention_chunkwise(fl(q_lat), fl(k_lat), fl(v_lat), gam[1], s_b)
    return l_f + fl(l_b), c_f + fl(c_b)


def even_mixer(h_lat, h_ctx, w_in, w_out, lq1, lk1, lq2, lk2, gn_g, dec_logit, lam_init, with_ctx):
    B_, T, _ = h_lat.shape

    def project(h):
        n = h.shape[1]
        aq, ak, av, bq, bk, bv, bg = jnp.split(h @ w_in, IN_SPLITS, axis=-1)
        return (aq.reshape(B_, n, DA_HEADS, 2, DA_QK), ak.reshape(B_, n, DA_HEADS, 2, DA_QK),
                av.reshape(B_, n, DA_HEADS, DA_V), bq.reshape(B_, n, RT_HEADS, RT_QK),
                bk.reshape(B_, n, RT_HEADS, RT_QK) * (RT_QK ** -0.5), bv.reshape(B_, n, RT_HEADS, RT_V), bg)

    aq, ak, av, bq, bk, bv, bg = project(h_lat)
    caq, cak, cav, cbq, cbk, cbv, cbg = project(h_ctx)
    ang_a = axial_rope_angles(T, DA_QK)
    ang_b = axial_rope_angles(T, RT_QK)
    aq, ak = apply_rope(aq, ang_a), apply_rope(ak, ang_a)
    bq, bk = apply_rope(bq, ang_b), apply_rope(bk, ang_b)

    lam = jnp.exp(jnp.sum(lq1 * lk1)) - jnp.exp(jnp.sum(lq2 * lk2)) + lam_init
    gn = gn_g.reshape(DA_HEADS, DA_V) * (1.0 - lam_init)
    k_all = jnp.concatenate([cak, ak], axis=1)
    v_all = jnp.concatenate([cav, av], axis=1)
    a_lat = (heads_rms(diff_attention_blocks(aq, k_all, v_all, lam)) * gn).reshape(B_, T, DA_WIDTH)

    gam = jax.nn.sigmoid(dec_logit.astype(jnp.float32))
    b_lat, b_ctx = bi_retention(bq, bk, bv, cbq, cbk, cbv, gam)
    b_lat = heads_rms(b_lat).reshape(B_, T, RT_WIDTH) * jax.nn.silu(bg)

    o_lat = (jnp.concatenate([a_lat, b_lat], -1) @ w_out).astype(h_lat.dtype)
    if not with_ctx:
        return o_lat, None
    Tc = h_ctx.shape[1]
    a_ctx = (heads_rms(diff_scores(caq, cak, cav, lam)) * gn).reshape(B_, Tc, DA_WIDTH)
    b_ctx = heads_rms(b_ctx).reshape(B_, Tc, RT_WIDTH) * jax.nn.silu(cbg)
    o_ctx = (jnp.concatenate([a_ctx, b_ctx], -1) @ w_out).astype(h_ctx.dtype)
    return o_lat, o_ctx


def q_shift(x, rows):
    B_, T, D = x.shape
    g = x.reshape(B_, rows, GRID_W, D)
    q = D // 4
    left = jnp.pad(g[:, :, :-1, :q], ((0, 0), (0, 0), (1, 0), (0, 0)))
    right = jnp.pad(g[:, :, 1:, q:2 * q], ((0, 0), (0, 0), (0, 1), (0, 0)))
    up = jnp.pad(g[:, :-1, :, 2 * q:3 * q], ((0, 0), (1, 0), (0, 0), (0, 0)))
    down = jnp.pad(g[:, 1:, :, 3 * q:], ((0, 0), (0, 1), (0, 0), (0, 0)))
    return jnp.concatenate([left, right, up, down], -1).reshape(B_, T, D)


def bi_shift(x):
    h = x.shape[-1] // 2
    prev = jnp.pad(x[:, :-1, :h], ((0, 0), (1, 0), (0, 0)))
    nxt = jnp.pad(x[:, 1:, h:], ((0, 0), (0, 1), (0, 0)))
    return jnp.concatenate([prev, nxt], -1)


def rwkv_project(x, x_shift, v_first, mu, wr, wk, wv, w0, w1, w2, a0, a1, a2, g1, g2, k_k, k_a, vres):
    B_, T, _ = x.shape
    hd = lambda t: t.astype(jnp.float32).reshape(B_, T, RW_HEADS, RW_HEAD)
    xx = x_shift - x
    xr, xw, xk, xv, xa, xg = (x + xx * mu[i] for i in range(6))
    r = xr @ wr
    k = xk @ wk
    v = xv @ wv
    if vres is None:
        v_first = v
    else:
        v0, v1, v2 = vres
        v = v + (v_first - v) * jax.nn.sigmoid(v0 + (xv @ v1) @ v2)
    g = jax.nn.sigmoid(xg @ g1) @ g2
    kk = hd(k * k_k)
    kk = kk / jnp.maximum(jnp.sqrt(jnp.sum(kk * kk, -1, keepdims=True)), 1e-12)
    k_a_h = k_a.astype(jnp.float32).reshape(RW_HEADS, RW_HEAD)
    dirs = []
    for d in range(2):
        w_pre = (w0[d] + jnp.tanh(xw @ w1[d]) @ w2[d]).astype(jnp.float32)
        w_log = -jax.nn.softplus(-w_pre) - 0.5
        decay = hd(jnp.exp(-jnp.exp(w_log)))
        a = hd(jax.nn.sigmoid(a0[d] + (xa @ a1[d]) @ a2[d]))
        kd = hd(k) * (1.0 + (a - 1.0) * k_a_h)
        dirs.append((decay, kd, a))
    return (hd(r), hd(v), g, kk, dirs, v_first)


def wkv7_scan(r, w, k, v, a, b, s0, reverse):
    def step(s, inp):
        r_t, w_t, k_t, v_t, a_t, b_t = inp
        sa = jnp.einsum('bhvk,bhk->bhv', s, a_t)
        s = s * w_t[:, :, None, :] + sa[..., None] * b_t[:, :, None, :] + v_t[..., None] * k_t[:, :, None, :]
        return s, jnp.einsum('bhvk,bhk->bhv', s, r_t)

    xs = tuple(jnp.moveaxis(t, 1, 0) for t in (r, w, k, v, a, b))
    s_fin, y = lax.scan(step, s0, xs, reverse=reverse)
    return jnp.moveaxis(y, 0, 1), s_fin


def rwkv_readout(stream, y, wo, r_k, lnx_g, lnx_b):
    r, v, g, _, dirs, _ = stream
    B_, T = y.shape[:2]
    mu = jnp.mean(y, -1, keepdims=True)
    var = jnp.mean(jnp.square(y - mu), -1, keepdims=True)
    yn = ((y - mu) * lax.rsqrt(var + RW_GN_EPS)).reshape(B_, T, D_MODEL) * lnx_g + lnx_b
    bonus = jnp.sum(r * (dirs[0][1] + dirs[1][1]) * r_k, -1, keepdims=True) * v
    return ((yn + bonus.reshape(B_, T, D_MODEL)) * g) @ wo


def rwkv_mixer(h_lat, h_ctx, vf_lat, vf_ctx, rows, mu, wr, wk, wv, wo, w0, w1, w2, a0, a1, a2,
               g1, g2, k_k, k_a, r_k, lnx_g, lnx_b, vres, with_ctx):
    proj = lambda h, hs, vf: rwkv_project(h, hs, vf, mu, wr, wk, wv, w0, w1, w2, a0, a1, a2,
                                          g1, g2, k_k, k_a, vres)
    lat = proj(h_lat, q_shift(h_lat, rows), vf_lat)
    ctx = proj(h_ctx, bi_shift(h_ctx), vf_ctx)
    zero = jnp.zeros((h_lat.shape[0], RW_HEADS, RW_HEAD, RW_HEAD), jnp.float32)

    def run(stream, d, s0):
        r, v, _, kk, dirs, _ = stream
        decay, kd, a = dirs[d]
        return wkv7_scan(r, decay, kd, v, -kk, kk * a, s0, reverse=(d == 1))

    yc_f, sc_f = run(ctx, 0, zero)
    yc_b, sc_b = run(ctx, 1, zero)
    yl_f, _ = run(lat, 0, sc_f)
    yl_b, _ = run(lat, 1, sc_b)
    o_lat = rwkv_readout(lat, yl_f + yl_b, wo, r_k, lnx_g, lnx_b).astype(h_lat.dtype)
    o_ctx = rwkv_readout(ctx, yc_f + yc_b, wo, r_k, lnx_g, lnx_b).astype(h_ctx.dtype) if with_ctx else None
    return o_lat, o_ctx, lat[5], ctx[5]


def setup_inputs(seed: int = 0) -> dict:
    key = jax.random.key(seed)
    ks = iter(jax.random.split(key, 64))
    nrm = lambda shape, s: jax.random.normal(next(ks), shape, jnp.float32) * s
    unif = lambda shape, lo, hi: jax.random.uniform(next(ks), shape, jnp.float32, lo, hi)
    D = D_MODEL
    inv = D ** -0.5
    g_ret = 1.0 - 2.0 ** (-5.0 - np.arange(RT_HEADS))
    ret_logit = jnp.asarray(np.log(g_ret / (1.0 - g_ret)), jnp.float32)
    return {
        "x": nrm((BATCH, SEQ, D), 1.0),
        "c": nrm((BATCH, D), 1.0),
        "ctx": nrm((BATCH, CTX_LEN, D), 1.0),
        "c_ctx": nrm((D,), 1.0),
        "mod_w": nrm((DEPTH, D, 6 * D), inv),
        "mod_b": nrm((DEPTH, 6 * D), 0.02),
        "ln1_g": 1.0 + nrm((DEPTH, D), 0.02),
        "ln1_b": nrm((DEPTH, D), 0.02),
        "ln2_g": 1.0 + nrm((DEPTH, D), 0.02),
        "ln2_b": nrm((DEPTH, D), 0.02),
        "ffn_w1": nrm((DEPTH, D, D_FF), inv),
        "ffn_w3": nrm((DEPTH, D, D_FF), inv),
        "ffn_w2": nrm((DEPTH, D_FF, D), BETA * D_FF ** -0.5),
        "ev_w_in": nrm((N_EVEN, D, IN_COLS), inv),
        "ev_w_out": nrm((N_EVEN, DA_WIDTH + RT_WIDTH, D), BETA * (DA_WIDTH + RT_WIDTH) ** -0.5),
        "da_lam_q1": nrm((N_EVEN, DA_QK), 0.1),
        "da_lam_k1": nrm((N_EVEN, DA_QK), 0.1),
        "da_lam_q2": nrm((N_EVEN, DA_QK), 0.1),
        "da_lam_k2": nrm((N_EVEN, DA_QK), 0.1),
        "da_gn_g": 1.0 + nrm((N_EVEN, DA_WIDTH), 0.02),
        "rt_decay_logit": ret_logit[None, None, :] + nrm((N_EVEN, 2, RT_HEADS), 0.1),
        "rw_mu": unif((N_ODD, 6, D), 0.0, 1.0),
        "rw_wr": nrm((N_ODD, D, D), inv),
        "rw_wk": nrm((N_ODD, D, D), inv),
        "rw_wv": nrm((N_ODD, D, D), inv),
        "rw_wo": nrm((N_ODD, D, D), BETA * inv),
        "rw_w0": unif((N_ODD, 2, D), -6.0, -1.0),
        "rw_w1": nrm((N_ODD, 2, D, RW_DECAY_LORA), inv),
        "rw_w2": nrm((N_ODD, 2, RW_DECAY_LORA, D), 0.1 * RW_DECAY_LORA ** -0.5),
        "rw_a0": nrm((N_ODD, 2, D), 0.1),
        "rw_a1": nrm((N_ODD, 2, D, RW_AAA_LORA), inv),
        "rw_a2": nrm((N_ODD, 2, RW_AAA_LORA, D), RW_AAA_LORA ** -0.5),
        "rw_v0": 1.0 + nrm((N_VRES, D), 0.1),
        "rw_v1": nrm((N_VRES, D, RW_MV_LORA), inv),
        "rw_v2": nrm((N_VRES, RW_MV_LORA, D), RW_MV_LORA ** -0.5),
        "rw_g1": nrm((N_ODD, D, RW_GATE_LORA), inv),
        "rw_g2": nrm((N_ODD, RW_GATE_LORA, D), RW_GATE_LORA ** -0.5),
        "rw_kk": 0.85 + nrm((N_ODD, D), 0.05),
        "rw_ka": 1.0 + nrm((N_ODD, D), 0.05),
        "rw_rk": nrm((N_ODD, RW_HEADS, RW_HEAD), 0.1),
        "rw_lnx_g": 1.0 + nrm((N_ODD, D), 0.02),
        "rw_lnx_b": nrm((N_ODD, D), 0.02),
    }


def reference(x, c, ctx, c_ctx, mod_w, mod_b, ln1_g, ln1_b, ln2_g, ln2_b, ffn_w1, ffn_w3, ffn_w2,
              ev_w_in, ev_w_out, da_lam_q1, da_lam_k1, da_lam_q2, da_lam_k2, da_gn_g, rt_decay_logit,
              rw_mu, rw_wr, rw_wk, rw_wv, rw_wo, rw_w0, rw_w1, rw_w2, rw_a0, rw_a1, rw_a2,
              rw_v0, rw_v1, rw_v2, rw_g1, rw_g2, rw_kk, rw_ka, rw_rk, rw_lnx_g, rw_lnx_b):
    rows = x.shape[1] // GRID_W
    xc = ctx
    c_act = jax.nn.silu(c)
    cc_act = jax.nn.silu(c_ctx)
    vf_lat = None
    vf_ctx = None
    for l in range(DEPTH):
        last = l == DEPTH - 1
        m = jnp.split((c_act @ mod_w[l] + mod_b[l])[:, None, :], 6, axis=-1)
        mc = jnp.split(cc_act @ mod_w[l] + mod_b[l], 6, axis=-1)
        h_lat = modulate(x, m[0], m[1])
        h_ctx = modulate(xc, mc[0], mc[1])
        if l % 2 == 0:
            e = l // 2
            lam_init = 0.8 - 0.6 * math.exp(-0.3 * l)
            o_lat, o_ctx = even_mixer(h_lat, h_ctx, ev_w_in[e], ev_w_out[e], da_lam_q1[e], da_lam_k1[e],
                                      da_lam_q2[e], da_lam_k2[e], da_gn_g[e], rt_decay_logit[e],
                                      lam_init, not last)
        else:
            j = l // 2
            vres = None if j == 0 else (rw_v0[j - 1], rw_v1[j - 1], rw_v2[j - 1])
            o_lat, o_ctx, vf_lat, vf_ctx = rwkv_mixer(
                h_lat, h_ctx, vf_lat, vf_ctx, rows, rw_mu[j], rw_wr[j], rw_wk[j], rw_wv[j], rw_wo[j],
                rw_w0[j], rw_w1[j], rw_w2[j], rw_a0[j], rw_a1[j], rw_a2[j], rw_g1[j], rw_g2[j],
                rw_kk[j], rw_ka[j], rw_rk[j], rw_lnx_g[j], rw_lnx_b[j], vres, not last)
        x = layer_norm(ALPHA * x + m[2] * o_lat, ln1_g[l], ln1_b[l])
        f_lat = swiglu(modulate(x, m[3], m[4]), ffn_w1[l], ffn_w3[l], ffn_w2[l])
        x = layer_norm(ALPHA * x + m[5] * f_lat, ln2_g[l], ln2_b[l])
        if not last:
            xc = layer_norm(ALPHA * xc + mc[2] * o_ctx, ln1_g[l], ln1_b[l])
            f_ctx = swiglu(modulate(xc, mc[3], mc[4]), ffn_w1[l], ffn_w3[l], ffn_w2[l])
            xc = layer_norm(ALPHA * xc + mc[5] * f_ctx, ln2_g[l], ln2_b[l])
    return x
```

```python
import functools
import math

import jax
import jax.numpy as jnp
import numpy as np
from jax import lax
from jax.experimental import pallas as pl
from jax.experimental.pallas import tpu as pltpu

F32 = jnp.float32
BF16 = jnp.bfloat16

D = 1024
DEPTH = 4
GRID_W = 64
ALPHA = (2.0 * DEPTH) ** 0.25
LN_EPS = 1e-6
ROPE_BASE = 10000.0
D_FF = 2816
DA_QK = 64
DA_HEADS = 4
RT_QK = 128
RT_HEADS = 4
RT_CHUNK = 128
Q_BLOCK = 128
EV_SEG = 512
RW_HEAD = 64
RW_GN_EPS = 64e-5
LORA_PAD = 128
GATE_PAD = 256
TM = 256
HALO = GRID_W
SCAN_L = 64
SCAN_W = 4 * RW_HEAD
VMEM_LIMIT = 56 * 1024 * 1024


def _dot(a, b):
    return jnp.dot(a, b, preferred_element_type=F32)


def _dot_nt(a, b):
    return lax.dot_general(a, b, (((1,), (1,)), ((), ())), preferred_element_type=F32)


def _dot_tn(a, b):
    return lax.dot_general(a, b, (((0,), (0,)), ((), ())), preferred_element_type=F32)


def _bf(x):
    return x.astype(BF16)


def _split2(x):
    hi = _bf(x)
    return hi, _bf(x - hi.astype(F32))


def _sigmoid(x):
    return 1.0 / (1.0 + jnp.exp(-x))


def _ln(x, g, b):
    mu = jnp.mean(x, -1, keepdims=True)
    xc = x - mu
    var = jnp.mean(xc * xc, -1, keepdims=True)
    return xc * lax.rsqrt(var + LN_EPS) * g + b


def _params(n_grid):
    return pltpu.CompilerParams(dimension_semantics=("arbitrary",) * n_grid,
                                vmem_limit_bytes=VMEM_LIMIT)


def _resident(shape):
    nd = len(shape)
    return pl.BlockSpec(shape, lambda *_: (0,) * nd, pipeline_mode=pl.Buffered(1))


def _mod_kernel(c_ref, w_ref, b_ref, o_ref):
    c = c_ref[...]
    act = _bf(c * _sigmoid(c))
    o_ref[...] = _dot(act, _bf(w_ref[...])) + b_ref[...]


def _adaln(cc, mod_w, mod_b):
    tn = 1536
    nt = mod_w.shape[2] // tn
    return pl.pallas_call(
        _mod_kernel,
        grid=(DEPTH, nt),
        in_specs=[pl.BlockSpec((16, D), lambda l, j: (0, 0)),
                  pl.BlockSpec((None, D, tn), lambda l, j: (l, 0, j)),
                  pl.BlockSpec((None, 1, tn), lambda l, j: (l, 0, j))],
        out_specs=pl.BlockSpec((None, 16, tn), lambda l, j: (l, 0, j)),
        out_shape=jax.ShapeDtypeStruct((DEPTH, 16, mod_w.shape[2]), F32),
        compiler_params=_params(2),
        name="adaln",
    )(cc, mod_w, mod_b.reshape(DEPTH, 1, -1))


def _evproj_kernel(x_ref, mod_ref, w_ref, ca_ref, sa_ref, cb_ref, sb_ref,
                   aq_ref, ak_ref, av_ref, bq_ref, bk_ref, bv_ref, bg_ref):
    h = _bf(x_ref[...] * (1.0 + mod_ref[1:2, :]) + mod_ref[0:1, :])
    ca, sa, cb, sb = ca_ref[...], sa_ref[...], cb_ref[...], sb_ref[...]
    lane = lax.broadcasted_iota(jnp.int32, (TM, 128), 1)
    first_half = (lane & (DA_QK // 2)) == 0

    def seg(j):
        return _dot(h, w_ref[:, j * EV_SEG:(j + 1) * EV_SEG])

    def rope_a(p, out_ref, scale):
        for j in range(EV_SEG // 128):
            blk = p[:, j * 128:(j + 1) * 128]
            sw = jnp.where(first_half, pltpu.roll(blk, 96, 1), pltpu.roll(blk, 32, 1))
            out_ref[:, j * 128:(j + 1) * 128] = _bf((blk * ca + sw * sa) * scale)

    def rope_b(p, out_ref):
        for j in range(EV_SEG // 128):
            blk = p[:, j * 128:(j + 1) * 128]
            out_ref[:, j * 128:(j + 1) * 128] = _bf(blk * cb + pltpu.roll(blk, 64, 1) * sb)

    rope_a(seg(0), aq_ref, DA_QK ** -0.5)
    rope_a(seg(1), ak_ref, 1.0)
    av_ref[...] = _bf(seg(2))
    rope_b(seg(3), bq_ref)
    rope_b(seg(4) * (RT_QK ** -0.5), bk_ref)
    bv_ref[...] = _bf(seg(5))
    bg_ref[...] = seg(6)


def _even_project(xs, mod, w_in, tabs):
    B, N, _ = xs.shape
    nt = N // TM
    tile = lambda w: pl.BlockSpec((None, TM, w), lambda b, i: (b, i, 0))
    tab = pl.BlockSpec((TM, 128), lambda b, i: (i, 0))
    outs = [jax.ShapeDtypeStruct((B, N, EV_SEG), BF16)] * 6 + [jax.ShapeDtypeStruct((B, N, EV_SEG), F32)]
    return pl.pallas_call(
        _evproj_kernel,
        grid=(B, nt),
        in_specs=[tile(D),
                  pl.BlockSpec((None, None, 6, D), lambda b, i: (b, jnp.minimum(i, 1), 0, 0)),
                  _resident(w_in.shape), tab, tab, tab, tab],
        out_specs=[tile(EV_SEG)] * 7,
        out_shape=outs,
        compiler_params=_params(2),
        name="even_project",
    )(xs, mod, w_in, *tabs)


def _attn_kernel(lam_ref, q_ref, k_ref, v_ref, gn_ref, o_ref, *, n_ctx, n_all):
    i = pl.program_id(2)
    lam = lam_ref[0]
    q = q_ref[...]
    lane = lax.broadcasted_iota(jnp.int32, q.shape, 1)
    zero = jnp.zeros_like(q)
    qq = jnp.concatenate([jnp.where(lane < DA_QK, q, zero), jnp.where(lane >= DA_QK, q, zero)], axis=0)

    def run(nk):
        k = k_ref[0:nk, :]
        v = v_ref[0:nk, :]
        s = _dot_nt(qq, k)
        e = jnp.exp(s - jnp.max(s, -1, keepdims=True))
        p = e * (1.0 / jnp.sum(e, -1, keepdims=True))
        w = _bf(p[0:Q_BLOCK] - lam * p[Q_BLOCK:2 * Q_BLOCK])
        o = _dot(w, v)
        o = o * lax.rsqrt(jnp.mean(o * o, -1, keepdims=True) + LN_EPS) * gn_ref[...]
        o_ref[...] = o.astype(o_ref.dtype)

    @pl.when(i < n_ctx // Q_BLOCK)
    def _():
        run(n_ctx)

    @pl.when(i >= n_ctx // Q_BLOCK)
    def _():
        run(n_all)


def _diff_attention(aq, ak, av, lam, gn, n_ctx):
    B, N, _ = aq.shape
    kv = pl.BlockSpec((None, N, 128), lambda b, h, i: (b, 0, h))
    qo = pl.BlockSpec((None, Q_BLOCK, 128), lambda b, h, i: (b, i, h))
    return pl.pallas_call(
        functools.partial(_attn_kernel, n_ctx=n_ctx, n_all=N),
        grid=(B, DA_HEADS, N // Q_BLOCK),
        in_specs=[pl.BlockSpec(memory_space=pltpu.SMEM), qo, kv, kv,
                  pl.BlockSpec((1, 128), lambda b, h, i: (0, h))],
        out_specs=qo,
        out_shape=jax.ShapeDtypeStruct((B, N, EV_SEG), BF16),
        compiler_params=_params(3),
        name="diff_attention",
    )(lam, aq, ak, av, gn)


def _ret_kernel(lg_ref, q_ref, k_ref, v_ref, g_ref, o_ref, s_ref, acc_ref, *, n_chunks, n_ctx_chunks):
    h = pl.program_id(1)
    C = RT_CHUNK
    ii = lax.broadcasted_iota(jnp.int32, (C, C), 0)
    jj = lax.broadcasted_iota(jnp.int32, (C, C), 1)
    idx = lax.broadcasted_iota(jnp.int32, (C, 1), 0).astype(F32)
    for d in range(2):
        lg = lg_ref[d, h]
        diff = ((ii - jj) if d == 0 else (jj - ii)).astype(F32)
        dec = jnp.where(diff >= 0, jnp.exp(lg * jnp.maximum(diff, 0.0)), 0.0)
        if d == 0:
            zeta = jnp.exp(lg * (C - 1.0 - idx))
            xi = jnp.exp(lg * (idx + 1.0))
        else:
            zeta = jnp.exp(lg * idx)
            xi = jnp.exp(lg * (C - idx))
        g_chunk = jnp.exp(lg * jnp.full((1, 1), float(C), F32))
        s_ref[...] = jnp.zeros_like(s_ref)

        def step(s, carry, d=d, dec=dec, zeta=zeta, xi=xi, g_chunk=g_chunk):
            if d == 0:
                c = s
            else:
                c = jnp.where(s < n_ctx_chunks, n_ctx_chunks - 1 - s, n_chunks + n_ctx_chunks - 1 - s)
            rows = pl.ds(pl.multiple_of(c * C, C), C)
            qc = q_ref[rows, :]
            kc = k_ref[rows, :]
            vc = v_ref[rows, :]
            state = s_ref[...]
            inner = _dot_nt(qc, kc) * dec
            o = _dot(_bf(inner), vc) + _dot(qc, _bf(state)) * xi
            kz = _bf(kc.astype(F32) * zeta)
            s_ref[...] = g_chunk * state + _dot_tn(kz, vc)
            if d == 0:
                acc_ref[rows, :] = o
            else:
                acc_ref[rows, :] += o
            return carry

        lax.fori_loop(0, n_chunks, step, 0)
    b = acc_ref[...]
    g = g_ref[...]
    b = b * lax.rsqrt(jnp.mean(b * b, -1, keepdims=True) + LN_EPS)
    o_ref[...] = (b * (g * _sigmoid(g))).astype(o_ref.dtype)


def _retention(bq, bk, bv, bg, log_gamma, n_ctx):
    B, N, _ = bq.shape
    blk = pl.BlockSpec((None, N, 128), lambda b, h: (b, 0, h))
    return pl.pallas_call(
        functools.partial(_ret_kernel, n_chunks=N // RT_CHUNK, n_ctx_chunks=n_ctx // RT_CHUNK),
        grid=(B, RT_HEADS),
        in_specs=[pl.BlockSpec(memory_space=pltpu.SMEM), blk, blk, blk, blk],
        out_specs=blk,
        out_shape=jax.ShapeDtypeStruct((B, N, EV_SEG), BF16),
        scratch_shapes=[pltpu.VMEM((RT_QK, 128), F32), pltpu.VMEM((N, 128), F32)],
        compiler_params=_params(2),
        name="retention",
    )(log_gamma, bq, bk, bv, bg)


def _post_kernel(*refs, mix_widths):
    n_mix = len(mix_widths)
    x_ref, mod_ref = refs[0], refs[1]
    mix_refs = refs[2:2 + n_mix]
    wo_ref, w1_ref, w3_ref, w2_ref, ln_ref, o_ref = refs[2 + n_mix:]
    x = x_ref[...]
    o = None
    off = 0
    for m_ref, w in zip(mix_refs, mix_widths):
        part = _dot(m_ref[...], wo_ref[off:off + w, :])
        o = part if o is None else o + part
        off += w
    x1 = _ln(ALPHA * x + mod_ref[2:3, :] * o, ln_ref[0:1, :], ln_ref[1:2, :])
    hm = _bf(x1 * (1.0 + mod_ref[4:5, :]) + mod_ref[3:4, :])
    u = _dot(hm, w1_ref[...])
    t = _dot(hm, w3_ref[...])
    z = _bf(u * _sigmoid(u) * t)
    f = _dot(z, w2_ref[...])
    o_ref[...] = _ln(ALPHA * x1 + mod_ref[5:6, :] * f, ln_ref[2:3, :], ln_ref[3:4, :])


def _post_mixer(xs, mod, mixes, wo, w1, w3, w2, ln, latent_only, n_ctx):
    B, N, _ = xs.shape
    t0 = n_ctx // TM if latent_only else 0
    nt = N // TM - t0
    seg = (lambda i: 1) if latent_only else (lambda i: jnp.minimum(i, 1))
    in_tile = lambda w, off=t0: pl.BlockSpec((None, TM, w), lambda b, i: (b, i + off, 0))
    widths = tuple(m.shape[-1] for m in mixes)
    mix_off = [t0 if m.shape[1] == N else 0 for m in mixes]
    return pl.pallas_call(
        functools.partial(_post_kernel, mix_widths=widths),
        grid=(B, nt),
        in_specs=[in_tile(D), pl.BlockSpec((None, None, 6, D), lambda b, i: (b, seg(i), 0, 0))]
                 + [in_tile(w, off) for w, off in zip(widths, mix_off)]
                 + [_resident(wo.shape), _resident(w1.shape), _resident(w3.shape), _resident(w2.shape),
                    _resident(ln.shape)],
        out_specs=pl.BlockSpec((None, TM, D), lambda b, i: (b, i, 0)),
        out_shape=jax.ShapeDtypeStruct((B, nt * TM, D), F32),
        compiler_params=_params(2),
        name="post_mixer",
    )(xs, mod, *mixes, wo, w1, w3, w2, ln)


PV_MU, PV_W0, PV_A0, PV_KK, PV_KA, PV_V0 = 0, 6, 8, 10, 11, 12


def _head_sum(x, e_ref):
    hi, lo = _split2(x)
    return _dot(hi, e_ref[...]) + _dot(lo, e_ref[...])


def _rwproj_kernel(*refs, has_vres, n_tiles):
    if has_vres:
        (xp_ref, xc_ref, xn_ref, mod_ref, vf_ref, pv_ref, wr_ref, wk_ref, wv_ref, w1_ref, w2_ref,
         a1_ref, a2_ref, g1_ref, g2_ref, e_ref, v1_ref, v2_ref,
         r_ref, v_ref, g_ref, kk_ref, lw_ref, kd_ref, ag_ref, hbuf, hs) = refs
    else:
        (xp_ref, xc_ref, xn_ref, mod_ref, pv_ref, wr_ref, wk_ref, wv_ref, w1_ref, w2_ref,
         a1_ref, a2_ref, g1_ref, g2_ref, e_ref,
         r_ref, v_ref, g_ref, kk_ref, lw_ref, kd_ref, ag_ref, hbuf, hs) = refs
    i = pl.program_id(1)
    one_scale = 1.0 + mod_ref[1:2, :]
    shift = mod_ref[0:1, :]
    hbuf[0:HALO, :] = xp_ref[...] * one_scale + shift
    hbuf[HALO:HALO + TM, :] = xc_ref[...] * one_scale + shift
    hbuf[HALO + TM:, :] = xn_ref[...] * one_scale + shift
    row = lax.broadcasted_iota(jnp.int32, (TM, 1), 0)
    q = D // 4

    @pl.when(i == 0)
    def _():
        hs[:, 0:2 * q] = jnp.where(row == 0, 0.0, hbuf[HALO - 1:HALO - 1 + TM, 0:2 * q])
        hs[:, 2 * q:] = jnp.where(row == TM - 1, 0.0, hbuf[HALO + 1:HALO + 1 + TM, 2 * q:])

    @pl.when(i > 0)
    def _():
        col = row & (GRID_W - 1)
        hs[:, 0:q] = jnp.where(col == 0, 0.0, hbuf[HALO - 1:HALO - 1 + TM, 0:q])
        hs[:, q:2 * q] = jnp.where(col == GRID_W - 1, 0.0, hbuf[HALO + 1:HALO + 1 + TM, q:2 * q])
        top = jnp.logical_and(i == 1, row < GRID_W)
        hs[:, 2 * q:3 * q] = jnp.where(top, 0.0, hbuf[0:TM, 2 * q:3 * q])
        bottom = jnp.logical_and(i == n_tiles - 1, row >= TM - GRID_W)
        hs[:, 3 * q:] = jnp.where(bottom, 0.0, hbuf[2 * HALO:2 * HALO + TM, 3 * q:])

    h = hbuf[HALO:HALO + TM, :]
    xx = hs[...] - h
    mix = lambda j: _bf(h + xx * pv_ref[PV_MU + j:PV_MU + j + 1, :])
    xr, xw, xk, xv, xa, xg = (mix(j) for j in range(6))
    r = _dot(xr, wr_ref[...])
    k = _dot(xk, wk_ref[...])
    v = _dot(xv, wv_ref[...])
    if has_vres:
        gate = _sigmoid(pv_ref[PV_V0:PV_V0 + 1, :] + _dot(_bf(_dot(xv, v1_ref[...])), v2_ref[...]))
        v = v + (vf_ref[...] - v) * gate
    g = _dot(_bf(_sigmoid(_dot(xg, g1_ref[...]))), g2_ref[...])
    kx = k * pv_ref[PV_KK:PV_KK + 1, :]
    kkn = kx * lax.rsqrt(jnp.maximum(_head_sum(kx * kx, e_ref), 1e-24))
    r_ref[...] = r
    v_ref[...] = v
    g_ref[...] = g
    kk_ref[...] = kkn
    tw = jnp.tanh(_dot(xw, w1_ref[...]))
    ta = _dot(xa, a1_ref[...])
    for d in range(2):
        cols = slice(d * LORA_PAD, (d + 1) * LORA_PAD)
        w_pre = pv_ref[PV_W0 + d:PV_W0 + d + 1, :] + _dot(_bf(tw[:, cols]), w2_ref[d])
        lw_ref[d] = -math.exp(-0.5) * _sigmoid(w_pre)
        ag = _sigmoid(pv_ref[PV_A0 + d:PV_A0 + d + 1, :] + _dot(_bf(ta[:, cols]), a2_ref[d]))
        ag_ref[d] = ag
        kd_ref[d] = k * (1.0 + (ag - 1.0) * pv_ref[PV_KA:PV_KA + 1, :])


def _rwkv_project(xs, mod, vf, p):
    B, N, _ = xs.shape
    nt = N // TM
    hp = TM // HALO
    n_halo = N // HALO
    tile = pl.BlockSpec((None, TM, D), lambda b, i: (b, i, 0))
    tile2 = pl.BlockSpec((2, None, TM, D), lambda b, i: (0, b, i, 0))
    in_specs = [pl.BlockSpec((None, HALO, D), lambda b, i: (b, jnp.maximum(i * hp - 1, 0), 0)),
                tile,
                pl.BlockSpec((None, HALO, D), lambda b, i: (b, jnp.minimum((i + 1) * hp, n_halo - 1), 0)),
                pl.BlockSpec((None, None, 6, D), lambda b, i: (b, jnp.minimum(i, 1), 0, 0))]
    args = [xs, xs, xs, mod]
    has_vres = vf is not None
    if has_vres:
        in_specs.append(tile)
        args.append(vf)
    names = ["pv", "wr", "wk", "wv", "w1", "w2", "a1", "a2", "g1", "g2", "e"] + (["v1", "v2"] if has_vres else [])
    for nm in names:
        in_specs.append(_resident(p[nm].shape))
        args.append(p[nm])
    one = jax.ShapeDtypeStruct((B, N, D), F32)
    two = jax.ShapeDtypeStruct((2, B, N, D), F32)
    return pl.pallas_call(
        functools.partial(_rwproj_kernel, has_vres=has_vres, n_tiles=nt),
        grid=(B, nt),
        in_specs=in_specs,
        out_specs=[tile, tile, tile, tile, tile2, tile2, tile2],
        out_shape=[one, one, one, one, two, two, two],
        scratch_shapes=[pltpu.VMEM((TM + 2 * HALO, D), F32), pltpu.VMEM((TM, D), F32)],
        compiler_params=_params(2),
        name="rwkv_project",
    )(*args)


def _scan_block(r_ref, v_ref, kk_ref, lw_ref, kd_ref, ag_ref, y_ref, h_ref, reverse):
    L, W = SCAN_L, SCAN_W
    row = lax.broadcasted_iota(jnp.int32, (L, W), 0)
    lane = lax.broadcasted_iota(jnp.int32, (L, W), 1)
    pos = lane & (RW_HEAD - 1)
    head = lane >> 6
    strict = (pos > row) if reverse else (pos < row)
    incl = (pos >= row) if reverse else (pos <= row)
    ident = pos == row
    r64 = lax.broadcasted_iota(jnp.int32, (L, L), 0)
    c64 = lax.broadcasted_iota(jnp.int32, (L, L), 1)
    tri = jnp.where((c64 >= r64) if reverse else (c64 <= r64), 1.0, 0.0).astype(BF16)

    def bd(x):
        zero = jnp.zeros_like(x)
        return jnp.concatenate([jnp.where(head == hh, x, zero) for hh in range(4)], axis=0)

    def diag_blocks(full):
        out = full[3 * RW_HEAD:4 * RW_HEAD]
        for hh in (2, 1, 0):
            out = jnp.where(head == hh, full[hh * RW_HEAD:(hh + 1) * RW_HEAD], out)
        return out

    n_chunks = TM // L
    order = range(n_chunks - 1, -1, -1) if reverse else range(n_chunks)
    pre = []
    for c in order:
        rows = slice(c * L, (c + 1) * L)
        lw = lw_ref[rows, :]
        hi = _bf(lw)
        r1 = lw - hi.astype(F32)
        mid = _bf(r1)
        lo = _bf(r1 - mid.astype(F32))
        cum = _dot(tri, hi) + _dot(tri, mid) + _dot(tri, lo)
        gam = jnp.exp(cum)
        gam_inv = jnp.exp(-cum)
        kk = kk_ref[rows, :]
        a_t = _bf(-kk * jnp.exp(cum - lw))
        r_t = r_ref[rows, :] * gam
        b_t = kk * ag_ref[rows, :] * gam_inv
        k_t = kd_ref[rows, :] * gam_inv
        vb = _bf(v_ref[rows, :])
        g_last = gam[0:1, :] if reverse else gam[L - 1:L, :]
        bd_v = bd(vb)
        g4 = _dot_nt(jnp.concatenate([a_t, _bf(r_t)], axis=0),
                     jnp.concatenate([bd(_bf(b_t)), bd(_bf(k_t))], axis=0))
        n_m = jnp.where(strict, g4[0:L, 0:W], 0.0)
        p_m = jnp.where(strict, g4[0:L, W:2 * W], 0.0)
        rb = _bf(jnp.where(incl, g4[L:2 * L, 0:W], 0.0))
        rk = _bf(jnp.where(incl, g4[L:2 * L, W:2 * W], 0.0))
        t_m = jnp.where(ident, 1.0, 0.0)
        m = 1
        while m < L:
            later, earlier = (pos, row) if reverse else (row, pos)
            coupling = jnp.logical_and(jnp.logical_and((later & m) != 0, (earlier & m) == 0),
                                       (row // (2 * m)) == (pos // (2 * m)))
            n_off = jnp.where(coupling, n_m, 0.0)
            if m == 1:
                t_m = t_m + n_off
            else:
                t_b = _bf(t_m)
                t_m = t_m + _dot(_bf(_dot(t_b, bd(_bf(n_off)))), bd(t_b))
            m *= 2
        t_m = _bf(t_m)
        pv = _dot(_bf(p_m), bd_v)
        tw = _dot(t_m, jnp.concatenate([bd(a_t), bd(_bf(pv))], axis=1))
        a_hat = _bf(tw[:, 0:W])
        w_b = _bf(tw[:, W:2 * W])
        q_hat = _bf(r_t + _dot(rb, bd(a_hat)))
        y_c = _dot(jnp.concatenate([rb, rk], axis=1), jnp.concatenate([bd(w_b), bd_v], axis=0))
        b_g = _bf(b_t * g_last)
        k_g = _bf(k_t * g_last)
        m_m = jnp.where(ident, g_last, 0.0) + diag_blocks(_dot_tn(b_g, a_hat))
        c_m = diag_blocks(_dot_tn(jnp.concatenate([b_g, k_g], axis=0), jnp.concatenate([w_b, vb], axis=0)))
        pre.append((rows, q_hat, y_c, _split2(m_m), c_m))
    for rows, q_hat, y_c, (m_hi, m_lo), c_m in pre:
        h_hi, h_lo = _split2(h_ref[...])
        bd_hi = bd(h_hi)
        y_ref[rows, :] = _dot(q_hat, bd_hi) + y_c
        h_ref[...] = _dot(m_hi, bd_hi) + _dot(m_hi, bd(h_lo)) + _dot(m_lo, bd_hi) + c_m


def _scan_kernel(r_ref, v_ref, kk_ref, lw_ref, kd_ref, ag_ref, y_ref, h_ref):
    d = pl.program_id(2)

    @pl.when(pl.program_id(3) == 0)
    def _():
        h_ref[...] = jnp.zeros_like(h_ref)

    @pl.when(d == 0)
    def _():
        _scan_block(r_ref, v_ref, kk_ref, lw_ref, kd_ref, ag_ref, y_ref, h_ref, False)

    @pl.when(d == 1)
    def _():
        _scan_block(r_ref, v_ref, kk_ref, lw_ref, kd_ref, ag_ref, y_ref, h_ref, True)


def _rwkv_scan(r, v, kk, lw, kd, ag, n_ctx):
    B, N, _ = r.shape
    nt = N // TM
    nc = n_ctx // TM

    def tile_of(d, s):
        back = jnp.where(s < nc, nc - 1 - s, nt + nc - 1 - s)
        return jnp.where(d == 0, s, back)

    one = pl.BlockSpec((None, TM, SCAN_W), lambda b, g, d, s: (b, tile_of(d, s), g))
    two = pl.BlockSpec((None, None, TM, SCAN_W), lambda b, g, d, s: (d, b, tile_of(d, s), g))
    return pl.pallas_call(
        _scan_kernel,
        grid=(B, D // SCAN_W, 2, nt),
        in_specs=[one, one, one, two, two, two],
        out_specs=two,
        out_shape=jax.ShapeDtypeStruct((2, B, N, D), F32),
        scratch_shapes=[pltpu.VMEM((RW_HEAD, SCAN_W), F32)],
        compiler_params=_params(4),
        name="rwkv_scan",
    )(r, v, kk, lw, kd, ag)


def _rwread_kernel(yf_ref, yb_ref, r_ref, v_ref, g_ref, kd0_ref, kd1_ref, vec_ref, e_ref, o_ref):
    inv = 1.0 / RW_HEAD
    y = yf_ref[...] + yb_ref[...]
    mu = _head_sum(y, e_ref) * inv
    yc = y - mu
    var = _head_sum(yc * yc, e_ref) * inv
    yn = yc * lax.rsqrt(var + RW_GN_EPS) * vec_ref[1:2, :] + vec_ref[2:3, :]
    bonus = _head_sum(r_ref[...] * (kd0_ref[...] + kd1_ref[...]) * vec_ref[0:1, :], e_ref) * v_ref[...]
    o_ref[...] = _bf((yn + bonus) * g_ref[...])


def _rwkv_readout(y, r, v, g, kd, vec, e, latent_only, n_ctx):
    B, N, _ = r.shape
    t0 = n_ctx // TM if latent_only else 0
    nt = N // TM - t0
    one = pl.BlockSpec((None, TM, D), lambda b, i: (b, i + t0, 0))
    pick = lambda d: pl.BlockSpec((None, None, TM, D), lambda b, i: (d, b, i + t0, 0))
    return pl.pallas_call(
        _rwread_kernel,
        grid=(B, nt),
        in_specs=[pick(0), pick(1), one, one, one, pick(0), pick(1), _resident(vec.shape), _resident(e.shape)],
        out_specs=pl.BlockSpec((None, TM, D), lambda b, i: (b, i, 0)),
        out_shape=jax.ShapeDtypeStruct((B, nt * TM, D), BF16),
        compiler_params=_params(2),
        name="rwkv_readout",
    )(y, y, r, v, g, kd, kd, vec, e)


def _rope_tables(n_lat, n_ctx):
    t = jnp.arange(n_lat)
    rowp = (t // GRID_W).astype(F32)
    colp = (t % GRID_W).astype(F32)

    def table(dim):
        n_freq = dim // 4
        inv = ROPE_BASE ** (-jnp.arange(n_freq, dtype=F32) / n_freq)
        ang = jnp.concatenate([rowp[:, None] * inv, colp[:, None] * inv], -1)
        cos, sin = jnp.cos(ang), jnp.sin(ang)
        reps = 128 // dim
        cos_t = jnp.tile(jnp.concatenate([cos, cos], -1), (1, reps))
        sin_t = jnp.tile(jnp.concatenate([-sin, sin], -1), (1, reps))
        cos_t = jnp.concatenate([jnp.ones((n_ctx, 128), F32), cos_t], 0)
        sin_t = jnp.concatenate([jnp.zeros((n_ctx, 128), F32), sin_t], 0)
        return cos_t, sin_t

    ca, sa = table(DA_QK)
    cb, sb = table(RT_QK)
    return ca, sa, cb, sb


def _pad_cols(w, n):
    return jnp.pad(w, ((0, 0), (0, n - w.shape[1])))


def _pad_rows(w, n):
    return jnp.pad(w, ((0, n - w.shape[0]), (0, 0)))


def kernel(x, c, ctx, c_ctx, mod_w, mod_b, ln1_g, ln1_b, ln2_g, ln2_b, ffn_w1, ffn_w3, ffn_w2, ev_w_in, ev_w_out, da_lam_q1, da_lam_k1, da_lam_q2, da_lam_k2, da_gn_g, rt_decay_logit, rw_mu, rw_wr, rw_wk, rw_wv, rw_wo, rw_w0, rw_w1, rw_w2, rw_a0, rw_a1, rw_a2, rw_v0, rw_v1, rw_v2, rw_g1, rw_g2, rw_kk, rw_ka, rw_rk, rw_lnx_g, rw_lnx_b):
    B, T, _ = x.shape
    n_ctx = ctx.shape[1]
    assert n_ctx == TM and T % TM == 0 and x.shape[2] == D and B <= 15
    xs = jnp.concatenate([ctx, x], axis=1)

    cc = jnp.zeros((16, D), F32).at[:B].set(c).at[B].set(c_ctx)
    mod_all = _adaln(cc, mod_w, mod_b)
    m_lat = mod_all[:, :B].reshape(DEPTH, B, 1, 6, D)
    m_ctx = jnp.broadcast_to(mod_all[:, B].reshape(DEPTH, 1, 1, 6, D), (DEPTH, B, 1, 6, D))
    mod_tab = jnp.concatenate([m_ctx, m_lat], axis=2)

    tabs = _rope_tables(T, n_ctx)
    lane_head = np.arange(D) // RW_HEAD
    head_ones = jnp.asarray(lane_head[:, None] == lane_head[None, :], BF16)

    vf = None
    for l in range(DEPTH):
        last = l == DEPTH - 1
        mod = mod_tab[l]
        ln = jnp.stack([ln1_g[l], ln1_b[l], ln2_g[l], ln2_b[l]])
        if l % 2 == 0:
            e = l // 2
            lam_init = 0.8 - 0.6 * math.exp(-0.3 * l)
            lam = (jnp.exp(jnp.sum(da_lam_q1[e] * da_lam_k1[e])) - jnp.exp(jnp.sum(da_lam_q2[e] * da_lam_k2[e]))
                   + lam_init).reshape(1).astype(F32)
            gn = (da_gn_g[e] * (1.0 - lam_init)).reshape(1, -1)
            log_gamma = jnp.log(jax.nn.sigmoid(rt_decay_logit[e].astype(F32)))
            aq, ak, av, bq, bk, bv, bg = _even_project(xs, mod, _bf(ev_w_in[e]), tabs)
            a_mix = _diff_attention(aq, ak, av, lam, gn, n_ctx)
            b_mix = _retention(bq, bk, bv, bg, log_gamma, n_ctx)
            mixes = (a_mix, b_mix)
            wo = _bf(ev_w_out[e])
        else:
            j = l // 2
            has_vres = j > 0
            pvec = jnp.zeros((16, D), F32)
            pvec = pvec.at[PV_MU:PV_MU + 6].set(rw_mu[j]).at[PV_W0:PV_W0 + 2].set(rw_w0[j])
            pvec = pvec.at[PV_A0:PV_A0 + 2].set(rw_a0[j]).at[PV_KK].set(rw_kk[j]).at[PV_KA].set(rw_ka[j])
            p = {
                "wr": _bf(rw_wr[j]), "wk": _bf(rw_wk[j]), "wv": _bf(rw_wv[j]),
                "w1": _bf(jnp.concatenate([_pad_cols(rw_w1[j, d], LORA_PAD) for d in range(2)], 1)),
                "w2": _bf(jnp.stack([_pad_rows(rw_w2[j, d], LORA_PAD) for d in range(2)])),
                "a1": _bf(jnp.concatenate([_pad_cols(rw_a1[j, d], LORA_PAD) for d in range(2)], 1)),
                "a2": _bf(jnp.stack([_pad_rows(rw_a2[j, d], LORA_PAD) for d in range(2)])),
                "g1": _bf(_pad_cols(rw_g1[j], GATE_PAD)), "g2": _bf(_pad_rows(rw_g2[j], GATE_PAD)),
                "e": head_ones,
            }
            if has_vres:
                pvec = pvec.at[PV_V0].set(rw_v0[j - 1])
                p["v1"] = _bf(_pad_cols(rw_v1[j - 1], LORA_PAD))
                p["v2"] = _bf(_pad_rows(rw_v2[j - 1], LORA_PAD))
            p["pv"] = pvec
            r, v, g, kk, lw, kd, ag = _rwkv_project(xs, mod, vf if has_vres else None, p)
            if not has_vres:
                vf = v
            y = _rwkv_scan(r, v, kk, lw, kd, ag, n_ctx)
            vec = jnp.zeros((8, D), F32).at[0].set(rw_rk[j].reshape(-1)).at[1].set(rw_lnx_g[j]).at[2].set(rw_lnx_b[j])
            mixes = (_rwkv_readout(y, r, v, g, kd, vec, head_ones, last, n_ctx),)
            wo = _bf(rw_wo[j])
        xs = _post_mixer(xs, mod, mixes, wo, _bf(ffn_w1[l]), _bf(ffn_w3[l]), _bf(ffn_w2[l]), ln, last, n_ctx)
    return xs
```

```python
import functools
import math

import jax
import jax.numpy as jnp
import numpy as np
from jax import lax
from jax.experimental import pallas as pl
from jax.experimental.pallas import tpu as pltpu

F32 = jnp.float32
BF16 = jnp.bfloat16

D = 1024
DEPTH = 4
GRID_W = 64
ALPHA = (2.0 * DEPTH) ** 0.25
LN_EPS = 1e-6
ROPE_BASE = 10000.0
D_FF = 2816
DA_QK = 64
DA_HEADS = 4
RT_QK = 128
RT_HEADS = 4
RT_CHUNK = 128
Q_BLOCK = 128
EV_SEG = 512
RW_HEAD = 64
RW_GN_EPS = 64e-5
LORA_PAD = 128
GATE_PAD = 256
TM = 256
HALO = GRID_W
SCAN_L = 64
SCAN_W = 4 * RW_HEAD
VMEM_LIMIT = 56 * 1024 * 1024


def _dot(a, b):
    return jnp.dot(a, b, preferred_element_type=F32)


def _dot_nt(a, b):
    return lax.dot_general(a, b, (((1,), (1,)), ((), ())), preferred_element_type=F32)


def _dot_tn(a, b):
    return lax.dot_general(a, b, (((0,), (0,)), ((), ())), preferred_element_type=F32)


def _bf(x):
    return x.astype(BF16)


def _split2(x):
    hi = _bf(x)
    return hi, _bf(x - hi.astype(F32))


def _sigmoid(x):
    return 1.0 / (1.0 + jnp.exp(-x))


def _ln(x, g, b):
    mu = jnp.mean(x, -1, keepdims=True)
    xc = x - mu
    var = jnp.mean(xc * xc, -1, keepdims=True)
    return xc * lax.rsqrt(var + LN_EPS) * g + b


def _params(n_grid):
    return pltpu.CompilerParams(dimension_semantics=("arbitrary",) * n_grid,
                                vmem_limit_bytes=VMEM_LIMIT)


def _resident(shape):
    nd = len(shape)
    return pl.BlockSpec(shape, lambda *_: (0,) * nd, pipeline_mode=pl.Buffered(1))


def _mod_kernel(c_ref, w_ref, b_ref, o_ref):
    c = c_ref[...]
    act = _bf(c * _sigmoid(c))
    o_ref[...] = _dot(act, _bf(w_ref[...])) + b_ref[...]


def _adaln(cc, mod_w, mod_b):
    tn = 1536
    nt = mod_w.shape[2] // tn
    return pl.pallas_call(
        _mod_kernel,
        grid=(DEPTH, nt),
        in_specs=[pl.BlockSpec((16, D), lambda l, j: (0, 0)),
                  pl.BlockSpec((None, D, tn), lambda l, j: (l, 0, j)),
                  pl.BlockSpec((None, 1, tn), lambda l, j: (l, 0, j))],
        out_specs=pl.BlockSpec((None, 16, tn), lambda l, j: (l, 0, j)),
        out_shape=jax.ShapeDtypeStruct((DEPTH, 16, mod_w.shape[2]), F32),
        compiler_params=_params(2),
        name="adaln",
    )(cc, mod_w, mod_b.reshape(DEPTH, 1, -1))


def _evproj_kernel(x_ref, mod_ref, w_ref, ca_ref, sa_ref, cb_ref, sb_ref,
                   aq_ref, ak_ref, av_ref, bq_ref, bk_ref, bv_ref, bg_ref):
    h = _bf(x_ref[...] * (1.0 + mod_ref[1:2, :]) + mod_ref[0:1, :])
    ca, sa, cb, sb = ca_ref[...], sa_ref[...], cb_ref[...], sb_ref[...]
    lane = lax.broadcasted_iota(jnp.int32, (TM, 128), 1)
    first_half = (lane & (DA_QK // 2)) == 0

    def seg(j):
        return _dot(h, w_ref[:, j * EV_SEG:(j + 1) * EV_SEG])

    def rope_a(p, out_ref, scale):
        for j in range(EV_SEG // 128):
            blk = p[:, j * 128:(j + 1) * 128]
            sw = jnp.where(first_half, pltpu.roll(blk, 96, 1), pltpu.roll(blk, 32, 1))
            out_ref[:, j * 128:(j + 1) * 128] = _bf((blk * ca + sw * sa) * scale)

    def rope_b(p, out_ref):
        for j in range(EV_SEG // 128):
            blk = p[:, j * 128:(j + 1) * 128]
            out_ref[:, j * 128:(j + 1) * 128] = _bf(blk * cb + pltpu.roll(blk, 64, 1) * sb)

    rope_a(seg(0), aq_ref, DA_QK ** -0.5)
    rope_a(seg(1), ak_ref, 1.0)
    av_ref[...] = _bf(seg(2))
    rope_b(seg(3), bq_ref)
    rope_b(seg(4) * (RT_QK ** -0.5), bk_ref)
    bv_ref[...] = _bf(seg(5))
    bg_ref[...] = seg(6)


def _even_project(xs, mod, w_in, tabs):
    B, N, _ = xs.shape
    nt = N // TM
    tile = lambda w: pl.BlockSpec((None, TM, w), lambda b, i: (b, i, 0))
    tab = pl.BlockSpec((TM, 128), lambda b, i: (i, 0))
    outs = [jax.ShapeDtypeStruct((B, N, EV_SEG), BF16)] * 6 + [jax.ShapeDtypeStruct((B, N, EV_SEG), F32)]
    return pl.pallas_call(
        _evproj_kernel,
        grid=(B, nt),
        in_specs=[tile(D),
                  pl.BlockSpec((None, None, 6, D), lambda b, i: (b, jnp.minimum(i, 1), 0, 0)),
                  _resident(w_in.shape), tab, tab, tab, tab],
        out_specs=[tile(EV_SEG)] * 7,
        out_shape=outs,
        compiler_params=_params(2),
        name="even_project",
    )(xs, mod, w_in, *tabs)


def _attn_kernel(lam_ref, q_ref, k_ref, v_ref, gn_ref, o_ref, *, n_ctx, n_all):
    i = pl.program_id(2)
    lam = lam_ref[0]
    q = q_ref[...]
    lane = lax.broadcasted_iota(jnp.int32, q.shape, 1)
    zero = jnp.zeros_like(q)
    qq = jnp.concatenate([jnp.where(lane < DA_QK, q, zero), jnp.where(lane >= DA_QK, q, zero)], axis=0)

    def run(nk):
        k = k_ref[0:nk, :]
        v = v_ref[0:nk, :]
        s = _dot_nt(qq, k)
        e = jnp.exp(s - jnp.max(s, -1, keepdims=True))
        p = e * (1.0 / jnp.sum(e, -1, keepdims=True))
        w = _bf(p[0:Q_BLOCK] - lam * p[Q_BLOCK:2 * Q_BLOCK])
        o = _dot(w, v)
        o = o * lax.rsqrt(jnp.mean(o * o, -1, keepdims=True) + LN_EPS) * gn_ref[...]
        o_ref[...] = o.astype(o_ref.dtype)

    @pl.when(i < n_ctx // Q_BLOCK)
    def _():
        run(n_ctx)

    @pl.when(i >= n_ctx // Q_BLOCK)
    def _():
        run(n_all)


def _diff_attention(aq, ak, av, lam, gn, n_ctx):
    B, N, _ = aq.shape
    kv = pl.BlockSpec((None, N, 128), lambda b, h, i: (b, 0, h))
    qo = pl.BlockSpec((None, Q_BLOCK, 128), lambda b, h, i: (b, i, h))
    return pl.pallas_call(
        functools.partial(_attn_kernel, n_ctx=n_ctx, n_all=N),
        grid=(B, DA_HEADS, N // Q_BLOCK),
        in_specs=[pl.BlockSpec(memory_space=pltpu.SMEM), qo, kv, kv,
                  pl.BlockSpec((1, 128), lambda b, h, i: (0, h))],
        out_specs=qo,
        out_shape=jax.ShapeDtypeStruct((B, N, EV_SEG), BF16),
        compiler_params=_params(3),
        name="diff_attention",
    )(lam, aq, ak, av, gn)


def _ret_kernel(lg_ref, q_ref, k_ref, v_ref, g_ref, o_ref, s_ref, acc_ref, *, n_chunks, n_ctx_chunks):
    h = pl.program_id(1)
    C = RT_CHUNK
    ii = lax.broadcasted_iota(jnp.int32, (C, C), 0)
    jj = lax.broadcasted_iota(jnp.int32, (C, C), 1)
    idx = lax.broadcasted_iota(jnp.int32, (C, 1), 0).astype(F32)
    for d in range(2):
        lg = lg_ref[d, h]
        diff = ((ii - jj) if d == 0 else (jj - ii)).astype(F32)
        dec = jnp.where(diff >= 0, jnp.exp(lg * jnp.maximum(diff, 0.0)), 0.0)
        if d == 0:
            zeta = jnp.exp(lg * (C - 1.0 - idx))
            xi = jnp.exp(lg * (idx + 1.0))
        else:
            zeta = jnp.exp(lg * idx)
            xi = jnp.exp(lg * (C - idx))
        g_chunk = jnp.exp(lg * jnp.full((1, 1), float(C), F32))
        s_ref[...] = jnp.zeros_like(s_ref)

        def step(s, carry, d=d, dec=dec, zeta=zeta, xi=xi, g_chunk=g_chunk):
            if d == 0:
                c = s
            else:
                c = jnp.where(s < n_ctx_chunks, n_ctx_chunks - 1 - s, n_chunks + n_ctx_chunks - 1 - s)
            rows = pl.ds(pl.multiple_of(c * C, C), C)
            qc = q_ref[rows, :]
            kc = k_ref[rows, :]
            vc = v_ref[rows, :]
            state = s_ref[...]
            inner = _dot_nt(qc, kc) * dec
            o = _dot(_bf(inner), vc) + _dot(qc, _bf(state)) * xi
            kz = _bf(kc.astype(F32) * zeta)
            s_ref[...] = g_chunk * state + _dot_tn(kz, vc)
            if d == 0:
                acc_ref[rows, :] = o
            else:
                acc_ref[rows, :] += o
            return carry

        lax.fori_loop(0, n_chunks, step, 0)
    b = acc_ref[...]
    g = g_ref[...]
    b = b * lax.rsqrt(jnp.mean(b * b, -1, keepdims=True) + LN_EPS)
    o_ref[...] = (b * (g * _sigmoid(g))).astype(o_ref.dtype)


def _retention(bq, bk, bv, bg, log_gamma, n_ctx):
    B, N, _ = bq.shape
    blk = pl.BlockSpec((None, N, 128), lambda b, h: (b, 0, h))
    return pl.pallas_call(
        functools.partial(_ret_kernel, n_chunks=N // RT_CHUNK, n_ctx_chunks=n_ctx // RT_CHUNK),
        grid=(B, RT_HEADS),
        in_specs=[pl.BlockSpec(memory_space=pltpu.SMEM), blk, blk, blk, blk],
        out_specs=blk,
        out_shape=jax.ShapeDtypeStruct((B, N, EV_SEG), BF16),
        scratch_shapes=[pltpu.VMEM((RT_QK, 128), F32), pltpu.VMEM((N, 128), F32)],
        compiler_params=_params(2),
        name="retention",
    )(log_gamma, bq, bk, bv, bg)


def _post_kernel(*refs, mix_widths):
    n_mix = len(mix_widths)
    x_ref, mod_ref = refs[0], refs[1]
    mix_refs = refs[2:2 + n_mix]
    wo_ref, w1_ref, w3_ref, w2_ref, ln_ref, o_ref = refs[2 + n_mix:]
    x = x_ref[...]
    o = None
    off = 0
    for m_ref, w in zip(mix_refs, mix_widths):
        part = _dot(m_ref[...], wo_ref[off:off + w, :])
        o = part if o is None else o + part
        off += w
    x1 = _ln(ALPHA * x + mod_ref[2:3, :] * o, ln_ref[0:1, :], ln_ref[1:2, :])
    hm = _bf(x1 * (1.0 + mod_ref[4:5, :]) + mod_ref[3:4, :])
    u = _dot(hm, w1_ref[...])
    t = _dot(hm, w3_ref[...])
    z = _bf(u * _sigmoid(u) * t)
    f = _dot(z, w2_ref[...])
    o_ref[...] = _ln(ALPHA * x1 + mod_ref[5:6, :] * f, ln_ref[2:3, :], ln_ref[3:4, :])


def _post_mixer(xs, mod, mixes, wo, w1, w3, w2, ln, latent_only, n_ctx):
    B, N, _ = xs.shape
    t0 = n_ctx // TM if latent_only else 0
    nt = N // TM - t0
    seg = (lambda i: 1) if latent_only else (lambda i: jnp.minimum(i, 1))
    in_tile = lambda w, off=t0: pl.BlockSpec((None, TM, w), lambda b, i: (b, i + off, 0))
    widths = tuple(m.shape[-1] for m in mixes)
    mix_off = [t0 if m.shape[1] == N else 0 for m in mixes]
    return pl.pallas_call(
        functools.partial(_post_kernel, mix_widths=widths),
        grid=(B, nt),
        in_specs=[in_tile(D), pl.BlockSpec((None, None, 6, D), lambda b, i: (b, seg(i), 0, 0))]
                 + [in_tile(w, off) for w, off in zip(widths, mix_off)]
                 + [_resident(wo.shape), _resident(w1.shape), _resident(w3.shape), _resident(w2.shape),
                    _resident(ln.shape)],
        out_specs=pl.BlockSpec((None, TM, D), lambda b, i: (b, i, 0)),
        out_shape=jax.ShapeDtypeStruct((B, nt * TM, D), F32),
        compiler_params=_params(2),
        name="post_mixer",
    )(xs, mod, *mixes, wo, w1, w3, w2, ln)


PV_MU, PV_W0, PV_A0, PV_KK, PV_KA, PV_V0 = 0, 6, 8, 10, 11, 12


def _head_sum(x, e_ref):
    hi, lo = _split2(x)
    return _dot(hi, e_ref[...]) + _dot(lo, e_ref[...])


def _rwproj_kernel(*refs, has_vres, n_tiles):
    if has_vres:
        (xp_ref, xc_ref, xn_ref, mod_ref, vf_ref, pv_ref, wr_ref, wk_ref, wv_ref, w1_ref, w2_ref,
         a1_ref, a2_ref, g1_ref, g2_ref, e_ref, v1_ref, v2_ref,
         r_ref, v_ref, g_ref, kk_ref, lw_ref, kd_ref, ag_ref, hbuf, hs) = refs
    else:
        (xp_ref, xc_ref, xn_ref, mod_ref, pv_ref, wr_ref, wk_ref, wv_ref, w1_ref, w2_ref,
         a1_ref, a2_ref, g1_ref, g2_ref, e_ref,
         r_ref, v_ref, g_ref, kk_ref, lw_ref, kd_ref, ag_ref, hbuf, hs) = refs
    i = pl.program_id(1)
    one_scale = 1.0 + mod_ref[1:2, :]
    shift = mod_ref[0:1, :]
    hbuf[0:HALO, :] = xp_ref[...] * one_scale + shift
    hbuf[HALO:HALO + TM, :] = xc_ref[...] * one_scale + shift
    hbuf[HALO + TM:, :] = xn_ref[...] * one_scale + shift
    row = lax.broadcasted_iota(jnp.int32, (TM, 1), 0)
    q = D // 4

    @pl.when(i == 0)
    def _():
        hs[:, 0:2 * q] = jnp.where(row == 0, 0.0, hbuf[HALO - 1:HALO - 1 + TM, 0:2 * q])
        hs[:, 2 * q:] = jnp.where(row == TM - 1, 0.0, hbuf[HALO + 1:HALO + 1 + TM, 2 * q:])

    @pl.when(i > 0)
    def _():
        col = row & (GRID_W - 1)
        hs[:, 0:q] = jnp.where(col == 0, 0.0, hbuf[HALO - 1:HALO - 1 + TM, 0:q])
        hs[:, q:2 * q] = jnp.where(col == GRID_W - 1, 0.0, hbuf[HALO + 1:HALO + 1 + TM, q:2 * q])
        top = jnp.logical_and(i == 1, row < GRID_W)
        hs[:, 2 * q:3 * q] = jnp.where(top, 0.0, hbuf[0:TM, 2 * q:3 * q])
        bottom = jnp.logical_and(i == n_tiles - 1, row >= TM - GRID_W)
        hs[:, 3 * q:] = jnp.where(bottom, 0.0, hbuf[2 * HALO:2 * HALO + TM, 3 * q:])

    h = hbuf[HALO:HALO + TM, :]
    xx = hs[...] - h
    mix = lambda j: _bf(h + xx * pv_ref[PV_MU + j:PV_MU + j + 1, :])
    xr, xw, xk, xv, xa, xg = (mix(j) for j in range(6))
    r = _dot(xr, wr_ref[...])
    k = _dot(xk, wk_ref[...])
    v = _dot(xv, wv_ref[...])
    if has_vres:
        gate = _sigmoid(pv_ref[PV_V0:PV_V0 + 1, :] + _dot(_bf(_dot(xv, v1_ref[...])), v2_ref[...]))
        v = v + (vf_ref[...] - v) * gate
    g = _dot(_bf(_sigmoid(_dot(xg, g1_ref[...]))), g2_ref[...])
    kx = k * pv_ref[PV_KK:PV_KK + 1, :]
    kkn = kx * lax.rsqrt(jnp.maximum(_head_sum(kx * kx, e_ref), 1e-24))
    r_ref[...] = r
    v_ref[...] = v
    g_ref[...] = g
    kk_ref[...] = kkn
    tw = jnp.tanh(_dot(xw, w1_ref[...]))
    ta = _dot(xa, a1_ref[...])
    for d in range(2):
        cols = slice(d * LORA_PAD, (d + 1) * LORA_PAD)
        w_pre = pv_ref[PV_W0 + d:PV_W0 + d + 1, :] + _dot(_bf(tw[:, cols]), w2_ref[d])
        lw_ref[d] = -math.exp(-0.5) * _sigmoid(w_pre)
        ag = _sigmoid(pv_ref[PV_A0 + d:PV_A0 + d + 1, :] + _dot(_bf(ta[:, cols]), a2_ref[d]))
        ag_ref[d] = ag
        kd_ref[d] = k * (1.0 + (ag - 1.0) * pv_ref[PV_KA:PV_KA + 1, :])


def _rwkv_project(xs, mod, vf, p):
    B, N, _ = xs.shape
    nt = N // TM
    hp = TM // HALO
    n_halo = N // HALO
    tile = pl.BlockSpec((None, TM, D), lambda b, i: (b, i, 0))
    tile2 = pl.BlockSpec((2, None, TM, D), lambda b, i: (0, b, i, 0))
    in_specs = [pl.BlockSpec((None, HALO, D), lambda b, i: (b, jnp.maximum(i * hp - 1, 0), 0)),
                tile,
                pl.BlockSpec((None, HALO, D), lambda b, i: (b, jnp.minimum((i + 1) * hp, n_halo - 1), 0)),
                pl.BlockSpec((None, None, 6, D), lambda b, i: (b, jnp.minimum(i, 1), 0, 0))]
    args = [xs, xs, xs, mod]
    has_vres = vf is not None
    if has_vres:
        in_specs.append(tile)
        args.append(vf)
    names = ["pv", "wr", "wk", "wv", "w1", "w2", "a1", "a2", "g1", "g2", "e"] + (["v1", "v2"] if has_vres else [])
    for nm in names:
        in_specs.append(_resident(p[nm].shape))
        args.append(p[nm])
    one = jax.ShapeDtypeStruct((B, N, D), F32)
    two = jax.ShapeDtypeStruct((2, B, N, D), F32)
    return pl.pallas_call(
        functools.partial(_rwproj_kernel, has_vres=has_vres, n_tiles=nt),
        grid=(B, nt),
        in_specs=in_specs,
        out_specs=[tile, tile, tile, tile, tile2, tile2, tile2],
        out_shape=[one, one, one, one, two, two, two],
        scratch_shapes=[pltpu.VMEM((TM + 2 * HALO, D), F32), pltpu.VMEM((TM, D), F32)],
        compiler_params=_params(2),
        name="rwkv_project",
    )(*args)


def _bd(x, head):
    zero = jnp.zeros_like(x)
    return jnp.concatenate([jnp.where(head == hh, x, zero) for hh in range(4)], axis=0)


def _diag_blocks(full, head):
    out = full[3 * RW_HEAD:4 * RW_HEAD]
    for hh in (2, 1, 0):
        out = jnp.where(head == hh, full[hh * RW_HEAD:(hh + 1) * RW_HEAD], out)
    return out


def _scan_chunk(r_ref, v_ref, kk_ref, lw_ref, kd_ref, ag_ref, rows, reverse):
    L, W = SCAN_L, SCAN_W
    row = lax.broadcasted_iota(jnp.int32, (L, W), 0)
    lane = lax.broadcasted_iota(jnp.int32, (L, W), 1)
    pos = lane & (RW_HEAD - 1)
    head = lane >> 6
    strict = (pos > row) if reverse else (pos < row)
    incl = (pos >= row) if reverse else (pos <= row)
    ident = pos == row
    r64 = lax.broadcasted_iota(jnp.int32, (L, L), 0)
    c64 = lax.broadcasted_iota(jnp.int32, (L, L), 1)
    tri = jnp.where((c64 >= r64) if reverse else (c64 <= r64), 1.0, 0.0).astype(BF16)
    bd = lambda x: _bd(x, head)

    lw = lw_ref[rows, :]
    hi = _bf(lw)
    r1 = lw - hi.astype(F32)
    mid = _bf(r1)
    lo = _bf(r1 - mid.astype(F32))
    cum = _dot(tri, hi) + _dot(tri, mid) + _dot(tri, lo)
    yield
    gam = jnp.exp(cum)
    gam_inv = jnp.exp(-cum)
    kk = kk_ref[rows, :]
    a_t = _bf(-kk * jnp.exp(cum - lw))
    r_t = r_ref[rows, :] * gam
    b_t = kk * ag_ref[rows, :] * gam_inv
    k_t = kd_ref[rows, :] * gam_inv
    vb = _bf(v_ref[rows, :])
    g_last = gam[0:1, :] if reverse else gam[L - 1:L, :]
    bd_v = bd(vb)
    g4 = _dot_nt(jnp.concatenate([a_t, _bf(r_t)], axis=0),
                 jnp.concatenate([bd(_bf(b_t)), bd(_bf(k_t))], axis=0))
    yield
    n_m = jnp.where(strict, g4[0:L, 0:W], 0.0)
    p_m = jnp.where(strict, g4[0:L, W:2 * W], 0.0)
    rb = _bf(jnp.where(incl, g4[L:2 * L, 0:W], 0.0))
    rk = _bf(jnp.where(incl, g4[L:2 * L, W:2 * W], 0.0))
    pv = _dot(_bf(p_m), bd_v)
    t_m = jnp.where(ident, 1.0, 0.0)
    m = 1
    while m < L:
        later, earlier = (pos, row) if reverse else (row, pos)
        coupling = jnp.logical_and(jnp.logical_and((later & m) != 0, (earlier & m) == 0),
                                   (row // (2 * m)) == (pos // (2 * m)))
        n_off = jnp.where(coupling, n_m, 0.0)
        if m == 1:
            t_m = t_m + n_off
        else:
            t_b = _bf(t_m)
            half = _bf(_dot(t_b, bd(_bf(n_off))))
            yield
            t_m = t_m + _dot(half, bd(t_b))
            yield
        m *= 2
    t_m = _bf(t_m)
    tw = _dot(t_m, jnp.concatenate([bd(a_t), bd(_bf(pv))], axis=1))
    yield
    a_hat = _bf(tw[:, 0:W])
    w_b = _bf(tw[:, W:2 * W])
    q_hat = _bf(r_t + _dot(rb, bd(a_hat)))
    y_c = _dot(jnp.concatenate([rb, rk], axis=1), jnp.concatenate([bd(w_b), bd_v], axis=0))
    b_g = _bf(b_t * g_last)
    k_g = _bf(k_t * g_last)
    m_m = jnp.where(ident, g_last, 0.0) + _diag_blocks(_dot_tn(b_g, a_hat), head)
    c_m = _diag_blocks(_dot_tn(jnp.concatenate([b_g, k_g], axis=0), jnp.concatenate([w_b, vb], axis=0)), head)
    return q_hat, y_c, _split2(m_m), c_m


def _lockstep(gens):
    results = [None] * len(gens)
    active = list(enumerate(gens))
    while active:
        still = []
        for idx, g in active:
            try:
                next(g)
                still.append((idx, g))
            except StopIteration as stop:
                results[idx] = stop.value
        active = still
    return results


def _scan_kernel(rf_ref, vf_ref, kkf_ref, lwf_ref, kdf_ref, agf_ref,
                 rb_ref, vb_ref, kkb_ref, lwb_ref, kdb_ref, agb_ref,
                 yf_ref, yb_ref, hf_ref, hb_ref):
    @pl.when(pl.program_id(2) == 0)
    def _():
        hf_ref[...] = jnp.zeros_like(hf_ref)
        hb_ref[...] = jnp.zeros_like(hb_ref)

    L = SCAN_L
    n_chunks = TM // L
    fwd = (rf_ref, vf_ref, kkf_ref, lwf_ref, kdf_ref, agf_ref)
    bwd = (rb_ref, vb_ref, kkb_ref, lwb_ref, kdb_ref, agb_ref)
    f_rows = [slice(c * L, (c + 1) * L) for c in range(n_chunks)]
    b_rows = f_rows[::-1]
    pre = _lockstep([_scan_chunk(*fwd, rows, False) for rows in f_rows]
                    + [_scan_chunk(*bwd, rows, True) for rows in b_rows])
    head = lax.broadcasted_iota(jnp.int32, (L, SCAN_W), 1) >> 6
    for i in range(n_chunks):
        for h_ref, y_ref, rows, (q_hat, y_c, (m_hi, m_lo), c_m) in (
                (hf_ref, yf_ref, f_rows[i], pre[i]), (hb_ref, yb_ref, b_rows[i], pre[n_chunks + i])):
            h_hi, h_lo = _split2(h_ref[...])
            bd_hi = _bd(h_hi, head)
            y_ref[rows, :] = _dot(q_hat, bd_hi) + y_c
            h_ref[...] = _dot(m_hi, bd_hi) + _dot(m_hi, _bd(h_lo, head)) + _dot(m_lo, bd_hi) + c_m


def _rwkv_scan(r, v, kk, lw, kd, ag, n_ctx):
    B, N, _ = r.shape
    nt = N // TM
    nc = n_ctx // TM

    def back(s):
        return jnp.where(s < nc, nc - 1 - s, nt + nc - 1 - s)

    f_one = pl.BlockSpec((None, TM, SCAN_W), lambda b, g, s: (b, s, g))
    b_one = pl.BlockSpec((None, TM, SCAN_W), lambda b, g, s: (b, back(s), g))
    f_two = pl.BlockSpec((None, None, TM, SCAN_W), lambda b, g, s: (0, b, s, g))
    b_two = pl.BlockSpec((None, None, TM, SCAN_W), lambda b, g, s: (1, b, back(s), g))
    out = jax.ShapeDtypeStruct((B, N, D), F32)
    return pl.pallas_call(
        _scan_kernel,
        grid=(B, D // SCAN_W, nt),
        in_specs=[f_one, f_one, f_one, f_two, f_two, f_two, b_one, b_one, b_one, b_two, b_two, b_two],
        out_specs=[f_one, b_one],
        out_shape=[out, out],
        scratch_shapes=[pltpu.VMEM((RW_HEAD, SCAN_W), F32), pltpu.VMEM((RW_HEAD, SCAN_W), F32)],
        compiler_params=_params(3),
        name="rwkv_scan",
    )(r, v, kk, lw, kd, ag, r, v, kk, lw, kd, ag)


def _rwread_kernel(yf_ref, yb_ref, r_ref, v_ref, g_ref, kd0_ref, kd1_ref, vec_ref, e_ref, o_ref):
    inv = 1.0 / RW_HEAD
    y = yf_ref[...] + yb_ref[...]
    mu = _head_sum(y, e_ref) * inv
    yc = y - mu
    var = _head_sum(yc * yc, e_ref) * inv
    yn = yc * lax.rsqrt(var + RW_GN_EPS) * vec_ref[1:2, :] + vec_ref[2:3, :]
    bonus = _head_sum(r_ref[...] * (kd0_ref[...] + kd1_ref[...]) * vec_ref[0:1, :], e_ref) * v_ref[...]
    o_ref[...] = _bf((yn + bonus) * g_ref[...])


def _rwkv_readout(yf, yb, r, v, g, kd, vec, e, latent_only, n_ctx):
    B, N, _ = r.shape
    t0 = n_ctx // TM if latent_only else 0
    nt = N // TM - t0
    one = pl.BlockSpec((None, TM, D), lambda b, i: (b, i + t0, 0))
    pick = lambda d: pl.BlockSpec((None, None, TM, D), lambda b, i: (d, b, i + t0, 0))
    return pl.pallas_call(
        _rwread_kernel,
        grid=(B, nt),
        in_specs=[one, one, one, one, one, pick(0), pick(1), _resident(vec.shape), _resident(e.shape)],
        out_specs=pl.BlockSpec((None, TM, D), lambda b, i: (b, i, 0)),
        out_shape=jax.ShapeDtypeStruct((B, nt * TM, D), BF16),
        compiler_params=_params(2),
        name="rwkv_readout",
    )(yf, yb, r, v, g, kd, kd, vec, e)


def _rope_tables(n_lat, n_ctx):
    t = jnp.arange(n_lat)
    rowp = (t // GRID_W).astype(F32)
    colp = (t % GRID_W).astype(F32)

    def table(dim):
        n_freq = dim // 4
        inv = ROPE_BASE ** (-jnp.arange(n_freq, dtype=F32) / n_freq)
        ang = jnp.concatenate([rowp[:, None] * inv, colp[:, None] * inv], -1)
        cos, sin = jnp.cos(ang), jnp.sin(ang)
        reps = 128 // dim
        cos_t = jnp.tile(jnp.concatenate([cos, cos], -1), (1, reps))
        sin_t = jnp.tile(jnp.concatenate([-sin, sin], -1), (1, reps))
        cos_t = jnp.concatenate([jnp.ones((n_ctx, 128), F32), cos_t], 0)
        sin_t = jnp.concatenate([jnp.zeros((n_ctx, 128), F32), sin_t], 0)
        return cos_t, sin_t

    ca, sa = table(DA_QK)
    cb, sb = table(RT_QK)
    return ca, sa, cb, sb


def _pad_cols(w, n):
    return jnp.pad(w, ((0, 0), (0, n - w.shape[1])))


def _pad_rows(w, n):
    return jnp.pad(w, ((0, n - w.shape[0]), (0, 0)))


def kernel(x, c, ctx, c_ctx, mod_w, mod_b, ln1_g, ln1_b, ln2_g, ln2_b, ffn_w1, ffn_w3, ffn_w2, ev_w_in, ev_w_out, da_lam_q1, da_lam_k1, da_lam_q2, da_lam_k2, da_gn_g, rt_decay_logit, rw_mu, rw_wr, rw_wk, rw_wv, rw_wo, rw_w0, rw_w1, rw_w2, rw_a0, rw_a1, rw_a2, rw_v0, rw_v1, rw_v2, rw_g1, rw_g2, rw_kk, rw_ka, rw_rk, rw_lnx_g, rw_lnx_b):
    B, T, _ = x.shape
    n_ctx = ctx.shape[1]
    assert n_ctx == TM and T % TM == 0 and x.shape[2] == D and B <= 15
    xs = jnp.concatenate([ctx, x], axis=1)

    cc = jnp.zeros((16, D), F32).at[:B].set(c).at[B].set(c_ctx)
    mod_all = _adaln(cc, mod_w, mod_b)
    m_lat = mod_all[:, :B].reshape(DEPTH, B, 1, 6, D)
    m_ctx = jnp.broadcast_to(mod_all[:, B].reshape(DEPTH, 1, 1, 6, D), (DEPTH, B, 1, 6, D))
    mod_tab = jnp.concatenate([m_ctx, m_lat], axis=2)

    tabs = _rope_tables(T, n_ctx)
    lane_head = np.arange(D) // RW_HEAD
    head_ones = jnp.asarray(lane_head[:, None] == lane_head[None, :], BF16)

    vf = None
    for l in range(DEPTH):
        last = l == DEPTH - 1
        mod = mod_tab[l]
        ln = jnp.stack([ln1_g[l], ln1_b[l], ln2_g[l], ln2_b[l]])
        if l % 2 == 0:
            e = l // 2
            lam_init = 0.8 - 0.6 * math.exp(-0.3 * l)
            lam = (jnp.exp(jnp.sum(da_lam_q1[e] * da_lam_k1[e])) - jnp.exp(jnp.sum(da_lam_q2[e] * da_lam_k2[e]))
                   + lam_init).reshape(1).astype(F32)
            gn = (da_gn_g[e] * (1.0 - lam_init)).reshape(1, -1)
            log_gamma = jnp.log(jax.nn.sigmoid(rt_decay_logit[e].astype(F32)))
            aq, ak, av, bq, bk, bv, bg = _even_project(xs, mod, _bf(ev_w_in[e]), tabs)
            a_mix = _diff_attention(aq, ak, av, lam, gn, n_ctx)
            b_mix = _retention(bq, bk, bv, bg, log_gamma, n_ctx)
            mixes = (a_mix, b_mix)
            wo = _bf(ev_w_out[e])
        else:
            j = l // 2
            has_vres = j > 0
            pvec = jnp.zeros((16, D), F32)
            pvec = pvec.at[PV_MU:PV_MU + 6].set(rw_mu[j]).at[PV_W0:PV_W0 + 2].set(rw_w0[j])
            pvec = pvec.at[PV_A0:PV_A0 + 2].set(rw_a0[j]).at[PV_KK].set(rw_kk[j]).at[PV_KA].set(rw_ka[j])
            p = {
                "wr": _bf(rw_wr[j]), "wk": _bf(rw_wk[j]), "wv": _bf(rw_wv[j]),
                "w1": _bf(jnp.concatenate([_pad_cols(rw_w1[j, d], LORA_PAD) for d in range(2)], 1)),
                "w2": _bf(jnp.stack([_pad_rows(rw_w2[j, d], LORA_PAD) for d in range(2)])),
                "a1": _bf(jnp.concatenate([_pad_cols(rw_a1[j, d], LORA_PAD) for d in range(2)], 1)),
                "a2": _bf(jnp.stack([_pad_rows(rw_a2[j, d], LORA_PAD) for d in range(2)])),
                "g1": _bf(_pad_cols(rw_g1[j], GATE_PAD)), "g2": _bf(_pad_rows(rw_g2[j], GATE_PAD)),
                "e": head_ones,
            }
            if has_vres:
                pvec = pvec.at[PV_V0].set(rw_v0[j - 1])
                p["v1"] = _bf(_pad_cols(rw_v1[j - 1], LORA_PAD))
                p["v2"] = _bf(_pad_rows(rw_v2[j - 1], LORA_PAD))
            p["pv"] = pvec
            r, v, g, kk, lw, kd, ag = _rwkv_project(xs, mod, vf if has_vres else None, p)
            if not has_vres:
                vf = v
            yf, yb = _rwkv_scan(r, v, kk, lw, kd, ag, n_ctx)
            vec = jnp.zeros((8, D), F32).at[0].set(rw_rk[j].reshape(-1)).at[1].set(rw_lnx_g[j]).at[2].set(rw_lnx_b[j])
            mixes = (_rwkv_readout(yf, yb, r, v, g, kd, vec, head_ones, last, n_ctx),)
            wo = _bf(rw_wo[j])
        xs = _post_mixer(xs, mod, mixes, wo, _bf(ffn_w1[l]), _bf(ffn_w3[l]), _bf(ffn_w2[l]), ln, last, n_ctx)
    return xs
```

```python
import functools
import math

import jax
import jax.numpy as jnp
import numpy as np
from jax import lax
from jax.experimental import pallas as pl
from jax.experimental.pallas import tpu as pltpu

F32 = jnp.float32
BF16 = jnp.bfloat16

D = 1024
DEPTH = 4
GRID_W = 64
ALPHA = (2.0 * DEPTH) ** 0.25
LN_EPS = 1e-6
ROPE_BASE = 10000.0
D_FF = 2816
DA_QK = 64
DA_HEADS = 4
RT_QK = 128
RT_HEADS = 4
RT_CHUNK = 128
Q_BLOCK = 128
EV_SEG = 512
RW_HEAD = 64
RW_GN_EPS = 64e-5
LORA_PAD = 128
GATE_PAD = 256
TM = 256
HALO = GRID_W
SCAN_L = 64
SCAN_W = 4 * RW_HEAD
PAIR_W = 2 * RW_HEAD
VMEM_LIMIT = 56 * 1024 * 1024


def _dot(a, b):
    return jnp.dot(a, b, preferred_element_type=F32)


def _dot_nt(a, b):
    return lax.dot_general(a, b, (((1,), (1,)), ((), ())), preferred_element_type=F32)


def _dot_tn(a, b):
    return lax.dot_general(a, b, (((0,), (0,)), ((), ())), preferred_element_type=F32)


def _bf(x):
    return x.astype(BF16)


def _split2(x):
    hi = _bf(x)
    return hi, _bf(x - hi.astype(F32))


def _sigmoid(x):
    return 1.0 / (1.0 + jnp.exp(-x))


def _ln(x, g, b):
    mu = jnp.mean(x, -1, keepdims=True)
    xc = x - mu
    var = jnp.mean(xc * xc, -1, keepdims=True)
    return xc * lax.rsqrt(var + LN_EPS) * g + b


def _params(n_grid):
    return pltpu.CompilerParams(dimension_semantics=("arbitrary",) * n_grid,
                                vmem_limit_bytes=VMEM_LIMIT)


def _resident(shape):
    nd = len(shape)
    return pl.BlockSpec(shape, lambda *_: (0,) * nd, pipeline_mode=pl.Buffered(1))


def _mod_kernel(c_ref, w_ref, b_ref, o_ref):
    c = c_ref[...]
    act = _bf(c * _sigmoid(c))
    o_ref[...] = _dot(act, _bf(w_ref[...])) + b_ref[...]


def _adaln(cc, mod_w, mod_b):
    tn = 1536
    nt = mod_w.shape[2] // tn
    return pl.pallas_call(
        _mod_kernel,
        grid=(DEPTH, nt),
        in_specs=[pl.BlockSpec((16, D), lambda l, j: (0, 0)),
                  pl.BlockSpec((None, D, tn), lambda l, j: (l, 0, j)),
                  pl.BlockSpec((None, 1, tn), lambda l, j: (l, 0, j))],
        out_specs=pl.BlockSpec((None, 16, tn), lambda l, j: (l, 0, j)),
        out_shape=jax.ShapeDtypeStruct((DEPTH, 16, mod_w.shape[2]), F32),
        compiler_params=_params(2),
        name="adaln",
    )(cc, mod_w, mod_b.reshape(DEPTH, 1, -1))


def _evproj_kernel(x_ref, mod_ref, w_ref, ca_ref, sa_ref, cb_ref, sb_ref,
                   aq_ref, ak_ref, av_ref, bq_ref, bk_ref, bv_ref, bg_ref):
    h = _bf(x_ref[...] * (1.0 + mod_ref[1:2, :]) + mod_ref[0:1, :])
    ca, sa, cb, sb = ca_ref[...], sa_ref[...], cb_ref[...], sb_ref[...]
    lane = lax.broadcasted_iota(jnp.int32, (TM, 128), 1)
    first_half = (lane & (DA_QK // 2)) == 0

    def seg(j):
        return _dot(h, w_ref[:, j * EV_SEG:(j + 1) * EV_SEG])

    def rope_a(p, out_ref, scale):
        for j in range(EV_SEG // 128):
            blk = p[:, j * 128:(j + 1) * 128]
            sw = jnp.where(first_half, pltpu.roll(blk, 96, 1), pltpu.roll(blk, 32, 1))
            out_ref[:, j * 128:(j + 1) * 128] = _bf((blk * ca + sw * sa) * scale)

    def rope_b(p, out_ref):
        for j in range(EV_SEG // 128):
            blk = p[:, j * 128:(j + 1) * 128]
            out_ref[:, j * 128:(j + 1) * 128] = _bf(blk * cb + pltpu.roll(blk, 64, 1) * sb)

    rope_a(seg(0), aq_ref, DA_QK ** -0.5)
    rope_a(seg(1), ak_ref, 1.0)
    av_ref[...] = _bf(seg(2))
    rope_b(seg(3), bq_ref)
    rope_b(seg(4) * (RT_QK ** -0.5), bk_ref)
    bv_ref[...] = _bf(seg(5))
    bg_ref[...] = seg(6)


def _even_project(xs, mod, w_in, tabs):
    B, N, _ = xs.shape
    nt = N // TM
    tile = lambda w: pl.BlockSpec((None, TM, w), lambda b, i: (b, i, 0))
    tab = pl.BlockSpec((TM, 128), lambda b, i: (i, 0))
    outs = [jax.ShapeDtypeStruct((B, N, EV_SEG), BF16)] * 6 + [jax.ShapeDtypeStruct((B, N, EV_SEG), F32)]
    return pl.pallas_call(
        _evproj_kernel,
        grid=(B, nt),
        in_specs=[tile(D),
                  pl.BlockSpec((None, None, 6, D), lambda b, i: (b, jnp.minimum(i, 1), 0, 0)),
                  _resident(w_in.shape), tab, tab, tab, tab],
        out_specs=[tile(EV_SEG)] * 7,
        out_shape=outs,
        compiler_params=_params(2),
        name="even_project",
    )(xs, mod, w_in, *tabs)


def _attn_block(q, k_ref, v_ref, nk, lam, gn):
    lane = lax.broadcasted_iota(jnp.int32, q.shape, 1)
    zero = jnp.zeros_like(q)
    qq = jnp.concatenate([jnp.where(lane < DA_QK, q, zero), jnp.where(lane >= DA_QK, q, zero)], axis=0)
    s = _dot_nt(qq, k_ref[0:nk, :])
    yield
    e = jnp.exp(s - jnp.max(s, -1, keepdims=True))
    inv = 1.0 / jnp.sum(e, -1, keepdims=True)
    eb = _bf(e)
    yield
    ev = _dot(eb, v_ref[0:nk, :])
    o = ev[0:Q_BLOCK] * inv[0:Q_BLOCK] - ev[Q_BLOCK:2 * Q_BLOCK] * (lam * inv[Q_BLOCK:2 * Q_BLOCK])
    return o * lax.rsqrt(jnp.mean(o * o, -1, keepdims=True) + LN_EPS) * gn


def _attn_kernel(lam_ref, q_ref, k_ref, v_ref, gn_ref, o_ref, *, n_ctx, n_all):
    i = pl.program_id(2)
    lam = lam_ref[0]
    gn = gn_ref[...]
    subs = [slice(j * Q_BLOCK, (j + 1) * Q_BLOCK) for j in range(TM // Q_BLOCK)]

    def run(nk):
        outs = _lockstep([_attn_block(q_ref[rows, :], k_ref, v_ref, nk, lam, gn) for rows in subs])
        for rows, o in zip(subs, outs):
            o_ref[rows, :] = o.astype(o_ref.dtype)

    @pl.when(i < n_ctx // TM)
    def _():
        run(n_ctx)

    @pl.when(i >= n_ctx // TM)
    def _():
        run(n_all)


def _diff_attention(aq, ak, av, lam, gn, n_ctx):
    B, N, _ = aq.shape
    kv = pl.BlockSpec((None, N, 128), lambda b, h, i: (b, 0, h))
    qo = pl.BlockSpec((None, TM, 128), lambda b, h, i: (b, i, h))
    return pl.pallas_call(
        functools.partial(_attn_kernel, n_ctx=n_ctx, n_all=N),
        grid=(B, DA_HEADS, N // TM),
        in_specs=[pl.BlockSpec(memory_space=pltpu.SMEM), qo, kv, kv,
                  pl.BlockSpec((1, 128), lambda b, h, i: (0, h))],
        out_specs=qo,
        out_shape=jax.ShapeDtypeStruct((B, N, EV_SEG), BF16),
        compiler_params=_params(3),
        name="diff_attention",
    )(lam, aq, ak, av, gn)


def _ret_kernel(lg_ref, q_ref, k_ref, v_ref, g_ref, o_ref, sf_ref, sb_ref, accf_ref, accb_ref,
                *, n_chunks, n_ctx_chunks):
    h = pl.program_id(1)
    C = RT_CHUNK
    ii = lax.broadcasted_iota(jnp.int32, (C, C), 0)
    jj = lax.broadcasted_iota(jnp.int32, (C, C), 1)
    idx = lax.broadcasted_iota(jnp.int32, (C, 1), 0).astype(F32)
    consts = []
    for d in range(2):
        lg = lg_ref[d, h]
        diff = ((ii - jj) if d == 0 else (jj - ii)).astype(F32)
        dec = jnp.where(diff >= 0, jnp.exp(lg * jnp.maximum(diff, 0.0)), 0.0)
        if d == 0:
            zeta = jnp.exp(lg * (C - 1.0 - idx))
            xi = jnp.exp(lg * (idx + 1.0))
        else:
            zeta = jnp.exp(lg * idx)
            xi = jnp.exp(lg * (C - idx))
        consts.append((dec, zeta, xi, jnp.exp(lg * jnp.full((1, 1), float(C), F32))))
    sf_ref[...] = jnp.zeros_like(sf_ref)
    sb_ref[...] = jnp.zeros_like(sb_ref)

    def chunk(s, d, s_ref, acc_ref):
        dec, zeta, xi, g_chunk = consts[d]
        if d == 0:
            c = s
        else:
            c = jnp.where(s < n_ctx_chunks, n_ctx_chunks - 1 - s, n_chunks + n_ctx_chunks - 1 - s)
        rows = pl.ds(pl.multiple_of(c * C, C), C)
        qc = q_ref[rows, :]
        kc = k_ref[rows, :]
        vc = v_ref[rows, :]
        state = s_ref[...]
        inner = _dot_nt(qc, kc)
        cross = _dot(qc, _bf(state))
        kz = _bf(kc.astype(F32) * zeta)
        s_ref[...] = g_chunk * state + _dot_tn(kz, vc)
        yield
        acc_ref[rows, :] = _dot(_bf(inner * dec), vc) + cross * xi

    def step(s, carry):
        _lockstep([chunk(s, 0, sf_ref, accf_ref), chunk(s, 1, sb_ref, accb_ref)])
        return carry

    lax.fori_loop(0, n_chunks, step, 0, unroll=2)
    b = accf_ref[...] + accb_ref[...]
    g = g_ref[...]
    b = b * lax.rsqrt(jnp.mean(b * b, -1, keepdims=True) + LN_EPS)
    o_ref[...] = (b * (g * _sigmoid(g))).astype(o_ref.dtype)


def _retention(bq, bk, bv, bg, log_gamma, n_ctx):
    B, N, _ = bq.shape
    blk = pl.BlockSpec((None, N, 128), lambda b, h: (b, 0, h))
    return pl.pallas_call(
        functools.partial(_ret_kernel, n_chunks=N // RT_CHUNK, n_ctx_chunks=n_ctx // RT_CHUNK),
        grid=(B, RT_HEADS),
        in_specs=[pl.BlockSpec(memory_space=pltpu.SMEM), blk, blk, blk, blk],
        out_specs=blk,
        out_shape=jax.ShapeDtypeStruct((B, N, EV_SEG), BF16),
        scratch_shapes=[pltpu.VMEM((RT_QK, 128), F32), pltpu.VMEM((RT_QK, 128), F32),
                        pltpu.VMEM((N, 128), F32), pltpu.VMEM((N, 128), F32)],
        compiler_params=_params(2),
        name="retention",
    )(log_gamma, bq, bk, bv, bg)


def _post_kernel(*refs, mix_widths):
    n_mix = len(mix_widths)
    x_ref, mod_ref = refs[0], refs[1]
    mix_refs = refs[2:2 + n_mix]
    wo_ref, w1_ref, w3_ref, w2_ref, ln_ref, o_ref = refs[2 + n_mix:]
    x = x_ref[...]
    o = None
    off = 0
    for m_ref, w in zip(mix_refs, mix_widths):
        part = _dot(m_ref[...], wo_ref[off:off + w, :])
        o = part if o is None else o + part
        off += w
    x1 = _ln(ALPHA * x + mod_ref[2:3, :] * o, ln_ref[0:1, :], ln_ref[1:2, :])
    hm = _bf(x1 * (1.0 + mod_ref[4:5, :]) + mod_ref[3:4, :])
    u = _dot(hm, w1_ref[...])
    t = _dot(hm, w3_ref[...])
    z = _bf(u * _sigmoid(u) * t)
    f = _dot(z, w2_ref[...])
    o_ref[...] = _ln(ALPHA * x1 + mod_ref[5:6, :] * f, ln_ref[2:3, :], ln_ref[3:4, :])


def _post_mixer(xs, mod, mixes, wo, w1, w3, w2, ln, latent_only, n_ctx):
    B, N, _ = xs.shape
    t0 = n_ctx // TM if latent_only else 0
    nt = N // TM - t0
    seg = (lambda i: 1) if latent_only else (lambda i: jnp.minimum(i, 1))
    in_tile = lambda w, off=t0: pl.BlockSpec((None, TM, w), lambda b, i: (b, i + off, 0))
    widths = tuple(m.shape[-1] for m in mixes)
    mix_off = [t0 if m.shape[1] == N else 0 for m in mixes]
    return pl.pallas_call(
        functools.partial(_post_kernel, mix_widths=widths),
        grid=(B, nt),
        in_specs=[in_tile(D), pl.BlockSpec((None, None, 6, D), lambda b, i: (b, seg(i), 0, 0))]
                 + [in_tile(w, off) for w, off in zip(widths, mix_off)]
                 + [_resident(wo.shape), _resident(w1.shape), _resident(w3.shape), _resident(w2.shape),
                    _resident(ln.shape)],
        out_specs=pl.BlockSpec((None, TM, D), lambda b, i: (b, i, 0)),
        out_shape=jax.ShapeDtypeStruct((B, nt * TM, D), F32),
        compiler_params=_params(2),
        name="post_mixer",
    )(xs, mod, *mixes, wo, w1, w3, w2, ln)


PV_MU, PV_W0, PV_A0, PV_KK, PV_KA, PV_V0 = 0, 6, 8, 10, 11, 12


def _head_sum(x, e_ref):
    rows = x.shape[0]
    hi, lo = _split2(x)
    cols = []
    for j in range(x.shape[1] // SCAN_W):
        sl = slice(j * SCAN_W, (j + 1) * SCAN_W)
        both = _dot(jnp.concatenate([hi[:, sl], lo[:, sl]], axis=0), e_ref[...])
        cols.append(both[0:rows] + both[rows:2 * rows])
    return jnp.concatenate(cols, axis=1)


def _rwproj_kernel(*refs, has_vres, n_tiles):
    if has_vres:
        (xp_ref, xc_ref, xn_ref, mod_ref, vf_ref, pv_ref, wr_ref, wk_ref, wv_ref, w1_ref, w2_ref,
         a1_ref, a2_ref, g1_ref, g2_ref, e_ref, v1_ref, v2_ref,
         r_ref, v_ref, g_ref, kk_ref, lw_ref, kd_ref, ag_ref, hbuf, hs) = refs
    else:
        (xp_ref, xc_ref, xn_ref, mod_ref, pv_ref, wr_ref, wk_ref, wv_ref, w1_ref, w2_ref,
         a1_ref, a2_ref, g1_ref, g2_ref, e_ref,
         r_ref, v_ref, g_ref, kk_ref, lw_ref, kd_ref, ag_ref, hbuf, hs) = refs
    i = pl.program_id(1)
    one_scale = 1.0 + mod_ref[1:2, :]
    shift = mod_ref[0:1, :]
    hbuf[0:HALO, :] = xp_ref[...] * one_scale + shift
    hbuf[HALO:HALO + TM, :] = xc_ref[...] * one_scale + shift
    hbuf[HALO + TM:, :] = xn_ref[...] * one_scale + shift
    row = lax.broadcasted_iota(jnp.int32, (TM, 1), 0)
    q = D // 4

    @pl.when(i == 0)
    def _():
        hs[:, 0:2 * q] = jnp.where(row == 0, 0.0, hbuf[HALO - 1:HALO - 1 + TM, 0:2 * q])
        hs[:, 2 * q:] = jnp.where(row == TM - 1, 0.0, hbuf[HALO + 1:HALO + 1 + TM, 2 * q:])

    @pl.when(i > 0)
    def _():
        col = row & (GRID_W - 1)
        hs[:, 0:q] = jnp.where(col == 0, 0.0, hbuf[HALO - 1:HALO - 1 + TM, 0:q])
        hs[:, q:2 * q] = jnp.where(col == GRID_W - 1, 0.0, hbuf[HALO + 1:HALO + 1 + TM, q:2 * q])
        top = jnp.logical_and(i == 1, row < GRID_W)
        hs[:, 2 * q:3 * q] = jnp.where(top, 0.0, hbuf[0:TM, 2 * q:3 * q])
        bottom = jnp.logical_and(i == n_tiles - 1, row >= TM - GRID_W)
        hs[:, 3 * q:] = jnp.where(bottom, 0.0, hbuf[2 * HALO:2 * HALO + TM, 3 * q:])

    h = hbuf[HALO:HALO + TM, :]
    xx = hs[...] - h
    mix = lambda j: _bf(h + xx * pv_ref[PV_MU + j:PV_MU + j + 1, :])
    xr, xw, xk, xv, xa, xg = (mix(j) for j in range(6))
    r = _dot(xr, wr_ref[...])
    k = _dot(xk, wk_ref[...])
    v = _dot(xv, wv_ref[...])
    if has_vres:
        gate = _sigmoid(pv_ref[PV_V0:PV_V0 + 1, :] + _dot(_bf(_dot(xv, v1_ref[...])), v2_ref[...]))
        v = v + (vf_ref[...].astype(F32) - v) * gate
    g = _dot(_bf(_sigmoid(_dot(xg, g1_ref[...]))), g2_ref[...])
    kx = k * pv_ref[PV_KK:PV_KK + 1, :]
    kkn = kx * lax.rsqrt(jnp.maximum(_head_sum(kx * kx, e_ref), 1e-24))
    r_ref[...] = _bf(r)
    v_ref[...] = _bf(v)
    g_ref[...] = _bf(g)
    kk_ref[...] = _bf(kkn)
    tw = jnp.tanh(_dot(xw, w1_ref[...]))
    ta = _dot(xa, a1_ref[...])
    for d in range(2):
        cols = slice(d * LORA_PAD, (d + 1) * LORA_PAD)
        w_pre = pv_ref[PV_W0 + d:PV_W0 + d + 1, :] + _dot(_bf(tw[:, cols]), w2_ref[d])
        lw_ref[d] = -math.exp(-0.5) * _sigmoid(w_pre)
        ag = _sigmoid(pv_ref[PV_A0 + d:PV_A0 + d + 1, :] + _dot(_bf(ta[:, cols]), a2_ref[d]))
        ag_ref[d] = _bf(ag)
        kd_ref[d] = _bf(k * (1.0 + (ag - 1.0) * pv_ref[PV_KA:PV_KA + 1, :]))


def _rwkv_project(xs, mod, vf, p):
    B, N, _ = xs.shape
    nt = N // TM
    hp = TM // HALO
    n_halo = N // HALO
    tile = pl.BlockSpec((None, TM, D), lambda b, i: (b, i, 0))
    tile2 = pl.BlockSpec((2, None, TM, D), lambda b, i: (0, b, i, 0))
    in_specs = [pl.BlockSpec((None, HALO, D), lambda b, i: (b, jnp.maximum(i * hp - 1, 0), 0)),
                tile,
                pl.BlockSpec((None, HALO, D), lambda b, i: (b, jnp.minimum((i + 1) * hp, n_halo - 1), 0)),
                pl.BlockSpec((None, None, 6, D), lambda b, i: (b, jnp.minimum(i, 1), 0, 0))]
    args = [xs, xs, xs, mod]
    has_vres = vf is not None
    if has_vres:
        in_specs.append(tile)
        args.append(vf)
    names = ["pv", "wr", "wk", "wv", "w1", "w2", "a1", "a2", "g1", "g2", "e"] + (["v1", "v2"] if has_vres else [])
    for nm in names:
        in_specs.append(_resident(p[nm].shape))
        args.append(p[nm])
    one = jax.ShapeDtypeStruct((B, N, D), BF16)
    two = jax.ShapeDtypeStruct((2, B, N, D), BF16)
    return pl.pallas_call(
        functools.partial(_rwproj_kernel, has_vres=has_vres, n_tiles=nt),
        grid=(B, nt),
        in_specs=in_specs,
        out_specs=[tile, tile, tile, tile, tile2, tile2, tile2],
        out_shape=[one, one, one, one, jax.ShapeDtypeStruct((2, B, N, D), F32), two, two],
        scratch_shapes=[pltpu.VMEM((TM + 2 * HALO, D), F32), pltpu.VMEM((TM, D), F32)],
        compiler_params=_params(2),
        name="rwkv_project",
    )(*args)


def _bd(x, head):
    zero = jnp.zeros_like(x)
    return jnp.concatenate([jnp.where(head == hh, x, zero) for hh in range(PAIR_W // RW_HEAD)], axis=0)


def _diag_blocks(full, head):
    return jnp.where(head == 0, full[0:RW_HEAD], full[RW_HEAD:2 * RW_HEAD])


def _scan_chunk(r_ref, v_ref, kk_ref, lw_ref, kd_ref, ag_ref, rows, cols, reverse):
    L, W = SCAN_L, PAIR_W
    row = lax.broadcasted_iota(jnp.int32, (L, W), 0)
    lane = lax.broadcasted_iota(jnp.int32, (L, W), 1)
    pos = lane & (RW_HEAD - 1)
    head = lane >> 6
    strict = (pos > row) if reverse else (pos < row)
    incl = (pos >= row) if reverse else (pos <= row)
    ident = pos == row
    r64 = lax.broadcasted_iota(jnp.int32, (L, L), 0)
    c64 = lax.broadcasted_iota(jnp.int32, (L, L), 1)
    tri = jnp.where((c64 >= r64) if reverse else (c64 <= r64), 1.0, 0.0).astype(BF16)
    bd = lambda x: _bd(x, head)

    load = lambda ref: ref[rows, cols].astype(F32)
    lw = load(lw_ref)
    hi = _bf(lw)
    r1 = lw - hi.astype(F32)
    mid = _bf(r1)
    lo = _bf(r1 - mid.astype(F32))
    cum = _dot(tri, hi) + _dot(tri, mid) + _dot(tri, lo)
    yield
    gam = jnp.exp(cum)
    gam_inv = jnp.exp(-cum)
    kk = load(kk_ref)
    a_t = _bf(-kk * jnp.exp(cum - lw))
    r_t = load(r_ref) * gam
    b_t = kk * load(ag_ref) * gam_inv
    k_t = load(kd_ref) * gam_inv
    vb = v_ref[rows, cols]
    g_last = gam[0:1, :] if reverse else gam[L - 1:L, :]
    bd_v = bd(vb)
    g4 = _dot_nt(jnp.concatenate([a_t, _bf(r_t)], axis=0),
                 jnp.concatenate([bd(_bf(b_t)), bd(_bf(k_t))], axis=0))
    yield
    n_m = jnp.where(strict, g4[0:L, 0:W], 0.0)
    p_m = jnp.where(strict, g4[0:L, W:2 * W], 0.0)
    rb = _bf(jnp.where(incl, g4[L:2 * L, 0:W], 0.0))
    rk = _bf(jnp.where(incl, g4[L:2 * L, W:2 * W], 0.0))
    pv = _dot(_bf(p_m), bd_v)
    t_m = jnp.where(ident, 1.0, 0.0)
    m = 1
    while m < L:
        later, earlier = (pos, row) if reverse else (row, pos)
        coupling = jnp.logical_and(jnp.logical_and((later & m) != 0, (earlier & m) == 0),
                                   (row // (2 * m)) == (pos // (2 * m)))
        n_off = jnp.where(coupling, n_m, 0.0)
        if m == 1:
            t_m = t_m + n_off
        else:
            t_b = _bf(t_m)
            half = _bf(_dot(t_b, bd(_bf(n_off))))
            yield
            t_m = t_m + _dot(half, bd(t_b))
            yield
        m *= 2
    t_m = _bf(t_m)
    tw = _dot(t_m, jnp.concatenate([bd(a_t), bd(_bf(pv))], axis=1))
    yield
    a_hat = _bf(tw[:, 0:W])
    w_b = _bf(tw[:, W:2 * W])
    q_hat = _bf(r_t + _dot(rb, bd(a_hat)))
    y_c = _dot(jnp.concatenate([rb, rk], axis=1), jnp.concatenate([bd(w_b), bd_v], axis=0))
    b_g = _bf(b_t * g_last)
    k_g = _bf(k_t * g_last)
    m_m = jnp.where(ident, g_last, 0.0) + _diag_blocks(_dot_tn(b_g, a_hat), head)
    c_m = _diag_blocks(_dot_tn(jnp.concatenate([b_g, k_g], axis=0), jnp.concatenate([w_b, vb], axis=0)), head)
    return q_hat, y_c, _bf(m_m), c_m


def _lockstep(gens):
    results = [None] * len(gens)
    active = list(enumerate(gens))
    while active:
        still = []
        for idx, g in active:
            try:
                next(g)
                still.append((idx, g))
            except StopIteration as stop:
                results[idx] = stop.value
        active = still
    return results


def _scan_kernel(rf_ref, vf_ref, kkf_ref, lwf_ref, kdf_ref, agf_ref,
                 rb_ref, vb_ref, kkb_ref, lwb_ref, kdb_ref, agb_ref,
                 yf_ref, yb_ref, hf_ref, hb_ref):
    @pl.when(pl.program_id(2) == 0)
    def _():
        hf_ref[...] = jnp.zeros_like(hf_ref)
        hb_ref[...] = jnp.zeros_like(hb_ref)

    L = SCAN_L
    n_chunks = TM // L
    fwd = (rf_ref, vf_ref, kkf_ref, lwf_ref, kdf_ref, agf_ref)
    bwd = (rb_ref, vb_ref, kkb_ref, lwb_ref, kdb_ref, agb_ref)
    f_rows = [slice(c * L, (c + 1) * L) for c in range(n_chunks)]
    b_rows = f_rows[::-1]
    pairs = [slice(p * PAIR_W, (p + 1) * PAIR_W) for p in range(SCAN_W // PAIR_W)]
    jobs = ([(fwd, hf_ref, yf_ref, rows, cols, False) for cols in pairs for rows in f_rows]
            + [(bwd, hb_ref, yb_ref, rows, cols, True) for cols in pairs for rows in b_rows])
    pre = _lockstep([_scan_chunk(*refs, rows, cols, rev) for refs, _, _, rows, cols, rev in jobs])
    head = lax.broadcasted_iota(jnp.int32, (L, PAIR_W), 1) >> 6
    for i in range(n_chunks):
        for j in range(i, len(jobs), n_chunks):
            _, h_ref, y_ref, rows, cols, _ = jobs[j]
            q_hat, y_c, m_b, c_m = pre[j]
            both = _dot(jnp.concatenate([q_hat, m_b], axis=0), _bd(_bf(h_ref[:, cols]), head))
            y_ref[rows, cols] = both[0:L] + y_c
            h_ref[:, cols] = both[L:2 * L] + c_m


def _rwkv_scan(r, v, kk, lw, kd, ag, n_ctx):
    B, N, _ = r.shape
    nt = N // TM
    nc = n_ctx // TM

    def back(s):
        return jnp.where(s < nc, nc - 1 - s, nt + nc - 1 - s)

    f_one = pl.BlockSpec((None, TM, SCAN_W), lambda b, g, s: (b, s, g))
    b_one = pl.BlockSpec((None, TM, SCAN_W), lambda b, g, s: (b, back(s), g))
    f_two = pl.BlockSpec((None, None, TM, SCAN_W), lambda b, g, s: (0, b, s, g))
    b_two = pl.BlockSpec((None, None, TM, SCAN_W), lambda b, g, s: (1, b, back(s), g))
    out = jax.ShapeDtypeStruct((B, N, D), F32)
    return pl.pallas_call(
        _scan_kernel,
        grid=(B, D // SCAN_W, nt),
        in_specs=[f_one, f_one, f_one, f_two, f_two, f_two, b_one, b_one, b_one, b_two, b_two, b_two],
        out_specs=[f_one, b_one],
        out_shape=[out, out],
        scratch_shapes=[pltpu.VMEM((RW_HEAD, SCAN_W), F32), pltpu.VMEM((RW_HEAD, SCAN_W), F32)],
        compiler_params=_params(3),
        name="rwkv_scan",
    )(r, v, kk, lw, kd, ag, r, v, kk, lw, kd, ag)


def _rwread_kernel(yf_ref, yb_ref, r_ref, v_ref, g_ref, kd0_ref, kd1_ref, vec_ref, e_ref, o_ref):
    inv = 1.0 / RW_HEAD
    y = yf_ref[...] + yb_ref[...]
    mu = _head_sum(y, e_ref) * inv
    yc = y - mu
    var = _head_sum(yc * yc, e_ref) * inv
    yn = yc * lax.rsqrt(var + RW_GN_EPS) * vec_ref[1:2, :] + vec_ref[2:3, :]
    f32 = lambda ref: ref[...].astype(F32)
    bonus = _head_sum(f32(r_ref) * (f32(kd0_ref) + f32(kd1_ref)) * vec_ref[0:1, :], e_ref) * f32(v_ref)
    o_ref[...] = _bf((yn + bonus) * f32(g_ref))


def _rwkv_readout(yf, yb, r, v, g, kd, vec, e, latent_only, n_ctx):
    B, N, _ = r.shape
    t0 = n_ctx // TM if latent_only else 0
    nt = N // TM - t0
    one = pl.BlockSpec((None, TM, D), lambda b, i: (b, i + t0, 0))
    pick = lambda d: pl.BlockSpec((None, None, TM, D), lambda b, i: (d, b, i + t0, 0))
    return pl.pallas_call(
        _rwread_kernel,
        grid=(B, nt),
        in_specs=[one, one, one, one, one, pick(0), pick(1), _resident(vec.shape), _resident(e.shape)],
        out_specs=pl.BlockSpec((None, TM, D), lambda b, i: (b, i, 0)),
        out_shape=jax.ShapeDtypeStruct((B, nt * TM, D), BF16),
        compiler_params=_params(2),
        name="rwkv_readout",
    )(yf, yb, r, v, g, kd, kd, vec, e)


def _rope_tables(n_lat, n_ctx):
    t = jnp.arange(n_lat)
    rowp = (t // GRID_W).astype(F32)
    colp = (t % GRID_W).astype(F32)

    def table(dim):
        n_freq = dim // 4
        inv = ROPE_BASE ** (-jnp.arange(n_freq, dtype=F32) / n_freq)
        ang = jnp.concatenate([rowp[:, None] * inv, colp[:, None] * inv], -1)
        cos, sin = jnp.cos(ang), jnp.sin(ang)
        reps = 128 // dim
        cos_t = jnp.tile(jnp.concatenate([cos, cos], -1), (1, reps))
        sin_t = jnp.tile(jnp.concatenate([-sin, sin], -1), (1, reps))
        cos_t = jnp.concatenate([jnp.ones((n_ctx, 128), F32), cos_t], 0)
        sin_t = jnp.concatenate([jnp.zeros((n_ctx, 128), F32), sin_t], 0)
        return cos_t, sin_t

    ca, sa = table(DA_QK)
    cb, sb = table(RT_QK)
    return ca, sa, cb, sb


def _pad_cols(w, n):
    return jnp.pad(w, ((0, 0), (0, n - w.shape[1])))


def _pad_rows(w, n):
    return jnp.pad(w, ((0, n - w.shape[0]), (0, 0)))


def kernel(x, c, ctx, c_ctx, mod_w, mod_b, ln1_g, ln1_b, ln2_g, ln2_b, ffn_w1, ffn_w3, ffn_w2, ev_w_in, ev_w_out, da_lam_q1, da_lam_k1, da_lam_q2, da_lam_k2, da_gn_g, rt_decay_logit, rw_mu, rw_wr, rw_wk, rw_wv, rw_wo, rw_w0, rw_w1, rw_w2, rw_a0, rw_a1, rw_a2, rw_v0, rw_v1, rw_v2, rw_g1, rw_g2, rw_kk, rw_ka, rw_rk, rw_lnx_g, rw_lnx_b):
    B, T, _ = x.shape
    n_ctx = ctx.shape[1]
    assert n_ctx == TM and T % TM == 0 and x.shape[2] == D and B <= 15
    xs = jnp.concatenate([ctx, x], axis=1)

    cc = jnp.zeros((16, D), F32).at[:B].set(c).at[B].set(c_ctx)
    mod_all = _adaln(cc, mod_w, mod_b)
    m_lat = mod_all[:, :B].reshape(DEPTH, B, 1, 6, D)
    m_ctx = jnp.broadcast_to(mod_all[:, B].reshape(DEPTH, 1, 1, 6, D), (DEPTH, B, 1, 6, D))
    mod_tab = jnp.concatenate([m_ctx, m_lat], axis=2)

    tabs = _rope_tables(T, n_ctx)
    lane_head = np.arange(SCAN_W) // RW_HEAD
    head_ones = jnp.asarray(lane_head[:, None] == lane_head[None, :], BF16)

    vf = None
    for l in range(DEPTH):
        last = l == DEPTH - 1
        mod = mod_tab[l]
        ln = jnp.stack([ln1_g[l], ln1_b[l], ln2_g[l], ln2_b[l]])
        if l % 2 == 0:
            e = l // 2
            lam_init = 0.8 - 0.6 * math.exp(-0.3 * l)
            lam = (jnp.exp(jnp.sum(da_lam_q1[e] * da_lam_k1[e])) - jnp.exp(jnp.sum(da_lam_q2[e] * da_lam_k2[e]))
                   + lam_init).reshape(1).astype(F32)
            gn = (da_gn_g[e] * (1.0 - lam_init)).reshape(1, -1)
            log_gamma = jnp.log(jax.nn.sigmoid(rt_decay_logit[e].astype(F32)))
            aq, ak, av, bq, bk, bv, bg = _even_project(xs, mod, _bf(ev_w_in[e]), tabs)
            a_mix = _diff_attention(aq, ak, av, lam, gn, n_ctx)
            b_mix = _retention(bq, bk, bv, bg, log_gamma, n_ctx)
            mixes = (a_mix, b_mix)
            wo = _bf(ev_w_out[e])
        else:
            j = l // 2
            has_vres = j > 0
            pvec = jnp.zeros((16, D), F32)
            pvec = pvec.at[PV_MU:PV_MU + 6].set(rw_mu[j]).at[PV_W0:PV_W0 + 2].set(rw_w0[j])
            pvec = pvec.at[PV_A0:PV_A0 + 2].set(rw_a0[j]).at[PV_KK].set(rw_kk[j]).at[PV_KA].set(rw_ka[j])
            p = {
                "wr": _bf(rw_wr[j]), "wk": _bf(rw_wk[j]), "wv": _bf(rw_wv[j]),
                "w1": _bf(jnp.concatenate([_pad_cols(rw_w1[j, d], LORA_PAD) for d in range(2)], 1)),
                "w2": _bf(jnp.stack([_pad_rows(rw_w2[j, d], LORA_PAD) for d in range(2)])),
                "a1": _bf(jnp.concatenate([_pad_cols(rw_a1[j, d], LORA_PAD) for d in range(2)], 1)),
                "a2": _bf(jnp.stack([_pad_rows(rw_a2[j, d], LORA_PAD) for d in range(2)])),
                "g1": _bf(_pad_cols(rw_g1[j], GATE_PAD)), "g2": _bf(_pad_rows(rw_g2[j], GATE_PAD)),
                "e": head_ones,
            }
            if has_vres:
                pvec = pvec.at[PV_V0].set(rw_v0[j - 1])
                p["v1"] = _bf(_pad_cols(rw_v1[j - 1], LORA_PAD))
                p["v2"] = _bf(_pad_rows(rw_v2[j - 1], LORA_PAD))
            p["pv"] = pvec
            r, v, g, kk, lw, kd, ag = _rwkv_project(xs, mod, vf if has_vres else None, p)
            if not has_vres:
                vf = v
            yf, yb = _rwkv_scan(r, v, kk, lw, kd, ag, n_ctx)
            vec = jnp.zeros((8, D), F32).at[0].set(rw_rk[j].reshape(-1)).at[1].set(rw_lnx_g[j]).at[2].set(rw_lnx_b[j])
            mixes = (_rwkv_readout(yf, yb, r, v, g, kd, vec, head_ones, last, n_ctx),)
            wo = _bf(rw_wo[j])
        xs = _post_mixer(xs, mod, mixes, wo, _bf(ffn_w1[l]), _bf(ffn_w3[l]), _bf(ffn_w2[l]), ln, last, n_ctx)
    return xs
```

```python
import functools
import math

import jax
import jax.numpy as jnp
import numpy as np
from jax import lax
from jax.experimental import pallas as pl
from jax.experimental.pallas import tpu as pltpu

F32 = jnp.float32
BF16 = jnp.bfloat16

D = 1024
DEPTH = 4
GRID_W = 64
ALPHA = (2.0 * DEPTH) ** 0.25
LN_EPS = 1e-6
ROPE_BASE = 10000.0
D_FF = 2816
FF_CHUNKS = (768, 768, 768, 512)
DA_QK = 64
DA_HEADS = 4
RT_QK = 128
RT_HEADS = 4
RT_CHUNK = 128
Q_BLOCK = 128
EV_SEG = 512
RW_HEAD = 64
RW_GN_EPS = 64e-5
LORA_PAD = 128
GATE_PAD = 256
TM = 256
HALO = GRID_W
SCAN_L = 64
HEADS4_W = 4 * RW_HEAD
SCAN_W = 8 * RW_HEAD
PAIR_W = 2 * RW_HEAD
SCAN_ROUND = 4
VMEM_LIMIT = 56 * 1024 * 1024


def _dot(a, b):
    return jnp.dot(a, b, preferred_element_type=F32)


def _dot_nt(a, b):
    return lax.dot_general(a, b, (((1,), (1,)), ((), ())), preferred_element_type=F32)


def _dot_tn(a, b):
    return lax.dot_general(a, b, (((0,), (0,)), ((), ())), preferred_element_type=F32)


def _bf(x):
    return x.astype(BF16)


def _split2(x):
    hi = _bf(x)
    return hi, _bf(x - hi.astype(F32))


def _sigmoid(x):
    return 0.5 * jnp.tanh(0.5 * x) + 0.5


def _ln(x, g, b):
    mu = jnp.mean(x, -1, keepdims=True)
    xc = x - mu
    var = jnp.mean(xc * xc, -1, keepdims=True)
    return xc * lax.rsqrt(var + LN_EPS) * g + b


def _params(n_grid):
    return pltpu.CompilerParams(dimension_semantics=("arbitrary",) * n_grid,
                                vmem_limit_bytes=VMEM_LIMIT)


def _resident(shape):
    nd = len(shape)
    return pl.BlockSpec(shape, lambda *_: (0,) * nd, pipeline_mode=pl.Buffered(1))


def _mod_kernel(c_ref, w_ref, b_ref, o_ref):
    c = c_ref[...]
    act = _bf(c * _sigmoid(c))
    o_ref[...] = _dot(act, _bf(w_ref[...])) + b_ref[...]


def _adaln(cc, mod_w, mod_b):
    tn = 1536
    nt = mod_w.shape[2] // tn
    return pl.pallas_call(
        _mod_kernel,
        grid=(DEPTH, nt),
        in_specs=[pl.BlockSpec((16, D), lambda l, j: (0, 0)),
                  pl.BlockSpec((None, D, tn), lambda l, j: (l, 0, j)),
                  pl.BlockSpec((None, 1, tn), lambda l, j: (l, 0, j))],
        out_specs=pl.BlockSpec((None, 16, tn), lambda l, j: (l, 0, j)),
        out_shape=jax.ShapeDtypeStruct((DEPTH, 16, mod_w.shape[2]), F32),
        compiler_params=_params(2),
        name="adaln",
    )(cc, mod_w, mod_b.reshape(DEPTH, 1, -1))


def _evproj_kernel(x_ref, mod_ref, w_ref, ca_ref, sa_ref, cb_ref, sb_ref,
                   aq_ref, ak_ref, av_ref, bq_ref, bk_ref, bv_ref, bg_ref):
    h = _bf(x_ref[...] * (1.0 + mod_ref[1:2, :]) + mod_ref[0:1, :])
    ca, sa, cb, sb = ca_ref[...], sa_ref[...], cb_ref[...], sb_ref[...]
    lane = lax.broadcasted_iota(jnp.int32, (TM, 128), 1)
    first_half = (lane & (DA_QK // 2)) == 0

    def seg(j):
        return _dot(h, w_ref[:, j * EV_SEG:(j + 1) * EV_SEG])

    def rope_a(p, out_ref, scale):
        for j in range(EV_SEG // 128):
            blk = p[:, j * 128:(j + 1) * 128]
            sw = jnp.where(first_half, pltpu.roll(blk, 96, 1), pltpu.roll(blk, 32, 1))
            out_ref[:, j * 128:(j + 1) * 128] = _bf((blk * ca + sw * sa) * scale)

    def rope_b(p, out_ref):
        for j in range(EV_SEG // 128):
            blk = p[:, j * 128:(j + 1) * 128]
            out_ref[:, j * 128:(j + 1) * 128] = _bf(blk * cb + pltpu.roll(blk, 64, 1) * sb)

    rope_a(seg(0), aq_ref, DA_QK ** -0.5)
    rope_a(seg(1), ak_ref, 1.0)
    av_ref[...] = _bf(seg(2))
    rope_b(seg(3), bq_ref)
    rope_b(seg(4) * (RT_QK ** -0.5), bk_ref)
    bv_ref[...] = _bf(seg(5))
    bg_ref[...] = seg(6)


def _even_project(xs, mod, w_in, tabs):
    B, N, _ = xs.shape
    nt = N // TM
    tile = lambda w: pl.BlockSpec((None, TM, w), lambda b, i: (b, i, 0))
    tab = pl.BlockSpec((TM, 128), lambda b, i: (i, 0))
    outs = [jax.ShapeDtypeStruct((B, N, EV_SEG), BF16)] * 6 + [jax.ShapeDtypeStruct((B, N, EV_SEG), F32)]
    return pl.pallas_call(
        _evproj_kernel,
        grid=(B, nt),
        in_specs=[tile(D),
                  pl.BlockSpec((None, None, 6, D), lambda b, i: (b, jnp.minimum(i, 1), 0, 0)),
                  _resident(w_in.shape), tab, tab, tab, tab],
        out_specs=[tile(EV_SEG)] * 7,
        out_shape=outs,
        compiler_params=_params(2),
        name="even_project",
    )(xs, mod, w_in, *tabs)


def _attn_block(q, k_ref, v_ref, nk, lam, gn):
    lane = lax.broadcasted_iota(jnp.int32, q.shape, 1)
    zero = jnp.zeros_like(q)
    qq = jnp.concatenate([jnp.where(lane < DA_QK, q, zero), jnp.where(lane >= DA_QK, q, zero)], axis=0)
    s = _dot_nt(qq, k_ref[0:nk, :])
    yield
    e = jnp.exp(s - jnp.max(s, -1, keepdims=True))
    inv = 1.0 / jnp.sum(e, -1, keepdims=True)
    eb = _bf(e)
    yield
    ev = _dot(eb, v_ref[0:nk, :])
    o = ev[0:Q_BLOCK] * inv[0:Q_BLOCK] - ev[Q_BLOCK:2 * Q_BLOCK] * (lam * inv[Q_BLOCK:2 * Q_BLOCK])
    return o * lax.rsqrt(jnp.mean(o * o, -1, keepdims=True) + LN_EPS) * gn


def _attn_kernel(lam_ref, q_ref, k_ref, v_ref, gn_ref, o_ref, *, n_ctx, n_all):
    i = pl.program_id(2)
    lam = lam_ref[0]
    gn = gn_ref[...]
    subs = [slice(j * Q_BLOCK, (j + 1) * Q_BLOCK) for j in range(TM // Q_BLOCK)]

    def run(nk):
        outs = _lockstep([_attn_block(q_ref[rows, :], k_ref, v_ref, nk, lam, gn) for rows in subs])
        for rows, o in zip(subs, outs):
            o_ref[rows, :] = o.astype(o_ref.dtype)

    @pl.when(i < n_ctx // TM)
    def _():
        run(n_ctx)

    @pl.when(i >= n_ctx // TM)
    def _():
        run(n_all)


def _diff_attention(aq, ak, av, lam, gn, n_ctx):
    B, N, _ = aq.shape
    kv = pl.BlockSpec((None, N, 128), lambda b, h, i: (b, 0, h))
    qo = pl.BlockSpec((None, TM, 128), lambda b, h, i: (b, i, h))
    return pl.pallas_call(
        functools.partial(_attn_kernel, n_ctx=n_ctx, n_all=N),
        grid=(B, DA_HEADS, N // TM),
        in_specs=[pl.BlockSpec(memory_space=pltpu.SMEM), qo, kv, kv,
                  pl.BlockSpec((1, 128), lambda b, h, i: (0, h))],
        out_specs=qo,
        out_shape=jax.ShapeDtypeStruct((B, N, EV_SEG), BF16),
        compiler_params=_params(3),
        name="diff_attention",
    )(lam, aq, ak, av, gn)


def _ret_kernel(lg_ref, q_ref, k_ref, v_ref, g_ref, o_ref, sf_ref, sb_ref, accf_ref, accb_ref,
                *, n_chunks, n_ctx_chunks):
    h = pl.program_id(1)
    C = RT_CHUNK
    ii = lax.broadcasted_iota(jnp.int32, (C, C), 0)
    jj = lax.broadcasted_iota(jnp.int32, (C, C), 1)
    idx = lax.broadcasted_iota(jnp.int32, (C, 1), 0).astype(F32)
    consts = []
    for d in range(2):
        lg = lg_ref[d, h]
        diff = ((ii - jj) if d == 0 else (jj - ii)).astype(F32)
        dec = jnp.where(diff >= 0, jnp.exp(lg * jnp.maximum(diff, 0.0)), 0.0)
        if d == 0:
            zeta = jnp.exp(lg * (C - 1.0 - idx))
            xi = jnp.exp(lg * (idx + 1.0))
        else:
            zeta = jnp.exp(lg * idx)
            xi = jnp.exp(lg * (C - idx))
        consts.append((dec, zeta, xi, jnp.exp(lg * jnp.full((1, 1), float(C), F32))))
    sf_ref[...] = jnp.zeros_like(sf_ref)
    sb_ref[...] = jnp.zeros_like(sb_ref)

    def chunk(s, d, s_ref, acc_ref):
        dec, zeta, xi, g_chunk = consts[d]
        if d == 0:
            c = s
        else:
            c = jnp.where(s < n_ctx_chunks, n_ctx_chunks - 1 - s, n_chunks + n_ctx_chunks - 1 - s)
        rows = pl.ds(pl.multiple_of(c * C, C), C)
        qc = q_ref[rows, :]
        kc = k_ref[rows, :]
        vc = v_ref[rows, :]
        state = s_ref[...]
        inner = _dot_nt(qc, kc)
        cross = _dot(qc, _bf(state))
        kz = _bf(kc.astype(F32) * zeta)
        s_ref[...] = g_chunk * state + _dot_tn(kz, vc)
        yield
        acc_ref[rows, :] = _dot(_bf(inner * dec), vc) + cross * xi

    def step(s, carry):
        _lockstep([chunk(s, 0, sf_ref, accf_ref), chunk(s, 1, sb_ref, accb_ref)])
        return carry

    lax.fori_loop(0, n_chunks, step, 0, unroll=2)
    b = accf_ref[...] + accb_ref[...]
    g = g_ref[...]
    b = b * lax.rsqrt(jnp.mean(b * b, -1, keepdims=True) + LN_EPS)
    o_ref[...] = (b * (g * _sigmoid(g))).astype(o_ref.dtype)


def _retention(bq, bk, bv, bg, log_gamma, n_ctx):
    B, N, _ = bq.shape
    blk = pl.BlockSpec((None, N, 128), lambda b, h: (b, 0, h))
    return pl.pallas_call(
        functools.partial(_ret_kernel, n_chunks=N // RT_CHUNK, n_ctx_chunks=n_ctx // RT_CHUNK),
        grid=(B, RT_HEADS),
        in_specs=[pl.BlockSpec(memory_space=pltpu.SMEM), blk, blk, blk, blk],
        out_specs=blk,
        out_shape=jax.ShapeDtypeStruct((B, N, EV_SEG), BF16),
        scratch_shapes=[pltpu.VMEM((RT_QK, 128), F32), pltpu.VMEM((RT_QK, 128), F32),
                        pltpu.VMEM((N, 128), F32), pltpu.VMEM((N, 128), F32)],
        compiler_params=_params(2),
        name="retention",
    )(log_gamma, bq, bk, bv, bg)


def _post_kernel(*refs, mix_widths, n_sub, has_ctx):
    n_mix = len(mix_widths)
    x_ref, mod_ref = refs[0], refs[1]
    mix_refs = refs[2:2 + n_mix]
    wo_ref, w1_ref, w3_ref, w2_ref, ln_ref, o_ref, x1_ref, hm_ref = refs[2 + n_mix:]
    i = pl.program_id(1)
    o = None
    off = 0
    for m_ref, w in zip(mix_refs, mix_widths):
        part = _dot(m_ref[...], wo_ref[off:off + w, :])
        o = part if o is None else o + part
        off += w
    subs = [slice(j * TM, (j + 1) * TM) for j in range(n_sub)]
    mods = []
    for j, rows in enumerate(subs):
        seg = jnp.where(i == 0, 0, 1) if (has_ctx and j == 0) else 1
        m = mod_ref[seg]
        mods.append(m)
        x1 = _ln(ALPHA * x_ref[rows, :] + m[2:3, :] * o[rows, :], ln_ref[0:1, :], ln_ref[1:2, :])
        x1_ref[rows, :] = x1
        hm_ref[rows, :] = _bf(x1 * (1.0 + m[4:5, :]) + m[3:4, :])
    hm = hm_ref[...]
    f = None
    off = 0
    for w in FF_CHUNKS:
        cols = slice(off, off + w)
        u = _dot(hm, w1_ref[:, cols])
        t = _dot(hm, w3_ref[:, cols])
        part = _dot(_bf(u * _sigmoid(u) * t), w2_ref[cols, :])
        f = part if f is None else f + part
        off += w
    for m, rows in zip(mods, subs):
        o_ref[rows, :] = _ln(ALPHA * x1_ref[rows, :] + m[5:6, :] * f[rows, :], ln_ref[2:3, :], ln_ref[3:4, :])


def _post_mixer(xs, mod, mixes, wo, w1, w3, w2, ln, latent_only, n_ctx):
    B, N, _ = xs.shape
    rows_out = N - n_ctx if latent_only else N
    n_sub = 1 if latent_only else max(s for s in (3, 2, 1) if N % (s * TM) == 0)
    tm = n_sub * TM
    t0 = n_ctx // tm if latent_only else 0
    nt = rows_out // tm
    in_tile = lambda w, off=t0: pl.BlockSpec((None, tm, w), lambda b, i: (b, i + off, 0))
    widths = tuple(m.shape[-1] for m in mixes)
    mix_off = [t0 if m.shape[1] == N else 0 for m in mixes]
    return pl.pallas_call(
        functools.partial(_post_kernel, mix_widths=widths, n_sub=n_sub, has_ctx=not latent_only),
        grid=(B, nt),
        in_specs=[in_tile(D), pl.BlockSpec((None, 2, 6, D), lambda b, i: (b, 0, 0, 0))]
                 + [in_tile(w, off) for w, off in zip(widths, mix_off)]
                 + [_resident(wo.shape), _resident(w1.shape), _resident(w3.shape), _resident(w2.shape),
                    _resident(ln.shape)],
        out_specs=pl.BlockSpec((None, tm, D), lambda b, i: (b, i, 0)),
        out_shape=jax.ShapeDtypeStruct((B, rows_out, D), F32),
        scratch_shapes=[pltpu.VMEM((tm, D), F32), pltpu.VMEM((tm, D), BF16)],
        compiler_params=_params(2),
        name="post_mixer",
    )(xs, mod, *mixes, wo, w1, w3, w2, ln)


PV_MU, PV_W0, PV_A0, PV_KK, PV_KA, PV_V0 = 0, 6, 8, 10, 11, 12


def _head_sum(x, e_ref):
    rows = x.shape[0]
    hi, lo = _split2(x)
    cols = []
    for j in range(x.shape[1] // HEADS4_W):
        sl = slice(j * HEADS4_W, (j + 1) * HEADS4_W)
        both = _dot(jnp.concatenate([hi[:, sl], lo[:, sl]], axis=0), e_ref[...])
        cols.append(both[0:rows] + both[rows:2 * rows])
    return jnp.concatenate(cols, axis=1)


def _rwproj_kernel(*refs, has_vres, n_tiles):
    if has_vres:
        (xp_ref, xc_ref, xn_ref, mod_ref, vf_ref, pv_ref, wr_ref, wk_ref, wv_ref, w1_ref, w2_ref,
         a1_ref, a2_ref, g1_ref, g2_ref, e_ref, v1_ref, v2_ref,
         r_ref, v_ref, g_ref, kk_ref, lw_ref, kd_ref, ag_ref, hbuf, hs) = refs
    else:
        (xp_ref, xc_ref, xn_ref, mod_ref, pv_ref, wr_ref, wk_ref, wv_ref, w1_ref, w2_ref,
         a1_ref, a2_ref, g1_ref, g2_ref, e_ref,
         r_ref, v_ref, g_ref, kk_ref, lw_ref, kd_ref, ag_ref, hbuf, hs) = refs
    i = pl.program_id(1)
    one_scale = 1.0 + mod_ref[1:2, :]
    shift = mod_ref[0:1, :]
    hbuf[0:HALO, :] = xp_ref[...] * one_scale + shift
    hbuf[HALO:HALO + TM, :] = xc_ref[...] * one_scale + shift
    hbuf[HALO + TM:, :] = xn_ref[...] * one_scale + shift
    row = lax.broadcasted_iota(jnp.int32, (TM, 1), 0)
    q = D // 4

    @pl.when(i == 0)
    def _():
        hs[:, 0:2 * q] = jnp.where(row == 0, 0.0, hbuf[HALO - 1:HALO - 1 + TM, 0:2 * q])
        hs[:, 2 * q:] = jnp.where(row == TM - 1, 0.0, hbuf[HALO + 1:HALO + 1 + TM, 2 * q:])

    @pl.when(i > 0)
    def _():
        col = row & (GRID_W - 1)
        hs[:, 0:q] = jnp.where(col == 0, 0.0, hbuf[HALO - 1:HALO - 1 + TM, 0:q])
        hs[:, q:2 * q] = jnp.where(col == GRID_W - 1, 0.0, hbuf[HALO + 1:HALO + 1 + TM, q:2 * q])
        top = jnp.logical_and(i == 1, row < GRID_W)
        hs[:, 2 * q:3 * q] = jnp.where(top, 0.0, hbuf[0:TM, 2 * q:3 * q])
        bottom = jnp.logical_and(i == n_tiles - 1, row >= TM - GRID_W)
        hs[:, 3 * q:] = jnp.where(bottom, 0.0, hbuf[2 * HALO:2 * HALO + TM, 3 * q:])

    h = hbuf[HALO:HALO + TM, :]
    xx = hs[...] - h
    mix = lambda j: _bf(h + xx * pv_ref[PV_MU + j:PV_MU + j + 1, :])
    xr, xw, xk, xv, xa, xg = (mix(j) for j in range(6))
    r = _dot(xr, wr_ref[...])
    k = _dot(xk, wk_ref[...])
    v = _dot(xv, wv_ref[...])
    if has_vres:
        gate = _sigmoid(pv_ref[PV_V0:PV_V0 + 1, :] + _dot(_bf(_dot(xv, v1_ref[...])), v2_ref[...]))
        v = v + (vf_ref[...].astype(F32) - v) * gate
    g = _dot(_bf(_sigmoid(_dot(xg, g1_ref[...]))), g2_ref[...])
    kx = k * pv_ref[PV_KK:PV_KK + 1, :]
    kkn = kx * lax.rsqrt(jnp.maximum(_head_sum(kx * kx, e_ref), 1e-24))
    r_ref[...] = _bf(r)
    v_ref[...] = _bf(v)
    g_ref[...] = _bf(g)
    kk_ref[...] = _bf(kkn)
    tw = jnp.tanh(_dot(xw, w1_ref[...]))
    ta = _dot(xa, a1_ref[...])
    for d in range(2):
        cols = slice(d * LORA_PAD, (d + 1) * LORA_PAD)
        w_pre = pv_ref[PV_W0 + d:PV_W0 + d + 1, :] + _dot(_bf(tw[:, cols]), w2_ref[d])
        lw_ref[d] = -math.exp(-0.5) * _sigmoid(w_pre)
        ag = _sigmoid(pv_ref[PV_A0 + d:PV_A0 + d + 1, :] + _dot(_bf(ta[:, cols]), a2_ref[d]))
        ag_ref[d] = _bf(ag)
        kd_ref[d] = _bf(k * (1.0 + (ag - 1.0) * pv_ref[PV_KA:PV_KA + 1, :]))


def _rwkv_project(xs, mod, vf, p):
    B, N, _ = xs.shape
    nt = N // TM
    hp = TM // HALO
    n_halo = N // HALO
    tile = pl.BlockSpec((None, TM, D), lambda b, i: (b, i, 0))
    tile2 = pl.BlockSpec((2, None, TM, D), lambda b, i: (0, b, i, 0))
    in_specs = [pl.BlockSpec((None, HALO, D), lambda b, i: (b, jnp.maximum(i * hp - 1, 0), 0)),
                tile,
                pl.BlockSpec((None, HALO, D), lambda b, i: (b, jnp.minimum((i + 1) * hp, n_halo - 1), 0)),
                pl.BlockSpec((None, None, 6, D), lambda b, i: (b, jnp.minimum(i, 1), 0, 0))]
    args = [xs, xs, xs, mod]
    has_vres = vf is not None
    if has_vres:
        in_specs.append(tile)
        args.append(vf)
    names = ["pv", "wr", "wk", "wv", "w1", "w2", "a1", "a2", "g1", "g2", "e"] + (["v1", "v2"] if has_vres else [])
    for nm in names:
        in_specs.append(_resident(p[nm].shape))
        args.append(p[nm])
    one = jax.ShapeDtypeStruct((B, N, D), BF16)
    two = jax.ShapeDtypeStruct((2, B, N, D), BF16)
    return pl.pallas_call(
        functools.partial(_rwproj_kernel, has_vres=has_vres, n_tiles=nt),
        grid=(B, nt),
        in_specs=in_specs,
        out_specs=[tile, tile, tile, tile, tile2, tile2, tile2],
        out_shape=[one, one, one, one, jax.ShapeDtypeStruct((2, B, N, D), F32), two, two],
        scratch_shapes=[pltpu.VMEM((TM + 2 * HALO, D), F32), pltpu.VMEM((TM, D), F32)],
        compiler_params=_params(2),
        name="rwkv_project",
    )(*args)


def _bd(x, head):
    zero = jnp.zeros_like(x)
    return jnp.concatenate([jnp.where(head == hh, x, zero) for hh in range(PAIR_W // RW_HEAD)], axis=0)


def _diag_blocks(full, head):
    return jnp.where(head == 0, full[0:RW_HEAD], full[RW_HEAD:2 * RW_HEAD])


def _scan_chunk(r_ref, v_ref, kk_ref, lw_ref, kd_ref, ag_ref, rows, cols, reverse):
    L, W = SCAN_L, PAIR_W
    row = lax.broadcasted_iota(jnp.int32, (L, W), 0)
    lane = lax.broadcasted_iota(jnp.int32, (L, W), 1)
    pos = lane & (RW_HEAD - 1)
    head = lane >> 6
    strict = (pos > row) if reverse else (pos < row)
    incl = (pos >= row) if reverse else (pos <= row)
    ident = pos == row
    bd = lambda x: _bd(x, head)

    load = lambda ref: ref[rows, cols].astype(F32)
    lw = load(lw_ref)
    cum = lw
    step = 1
    while step < L:
        if reverse:
            cum = cum + jnp.where(row < L - step, pltpu.roll(cum, L - step, 0), 0.0)
        else:
            cum = cum + jnp.where(row >= step, pltpu.roll(cum, step, 0), 0.0)
        step *= 2
    gam = jnp.exp(cum)
    gam_inv = jnp.exp(-cum)
    kk = load(kk_ref)
    a_t = _bf(-kk * jnp.exp(cum - lw))
    r_t = load(r_ref) * gam
    b_t = kk * load(ag_ref) * gam_inv
    k_t = load(kd_ref) * gam_inv
    vb = v_ref[rows, cols]
    g_last = gam[0:1, :] if reverse else gam[L - 1:L, :]
    bd_v = bd(vb)
    g4 = _dot_nt(jnp.concatenate([a_t, _bf(r_t)], axis=0),
                 jnp.concatenate([bd(_bf(b_t)), bd(_bf(k_t))], axis=0))
    yield
    n_m = jnp.where(strict, g4[0:L, 0:W], 0.0)
    p_m = jnp.where(strict, g4[0:L, W:2 * W], 0.0)
    rb = _bf(jnp.where(incl, g4[L:2 * L, 0:W], 0.0))
    rk = _bf(jnp.where(incl, g4[L:2 * L, W:2 * W], 0.0))
    pv = _dot(_bf(p_m), bd_v)
    t_m = jnp.where(ident, 1.0, 0.0)
    m = 1
    while m < L:
        later, earlier = (pos, row) if reverse else (row, pos)
        coupling = jnp.logical_and(jnp.logical_and((later & m) != 0, (earlier & m) == 0),
                                   (row // (2 * m)) == (pos // (2 * m)))
        n_off = jnp.where(coupling, n_m, 0.0)
        if m == 1:
            t_m = t_m + n_off
        else:
            t_b = _bf(t_m)
            half = _bf(_dot(t_b, bd(_bf(n_off))))
            yield
            t_m = t_m + _dot(half, bd(t_b))
            yield
        m *= 2
    t_m = _bf(t_m)
    tw = _dot(t_m, jnp.concatenate([bd(a_t), bd(_bf(pv))], axis=1))
    yield
    a_hat = _bf(tw[:, 0:W])
    w_b = _bf(tw[:, W:2 * W])
    q_hat = _bf(r_t + _dot(rb, bd(a_hat)))
    y_c = _dot(jnp.concatenate([rb, rk], axis=1), jnp.concatenate([bd(w_b), bd_v], axis=0))
    b_g = _bf(b_t * g_last)
    k_g = _bf(k_t * g_last)
    m_m = jnp.where(ident, g_last, 0.0) + _diag_blocks(_dot_tn(b_g, a_hat), head)
    c_m = _diag_blocks(_dot_tn(jnp.concatenate([b_g, k_g], axis=0), jnp.concatenate([w_b, vb], axis=0)), head)
    return q_hat, y_c, _bf(m_m), c_m


def _lockstep(gens):
    results = [None] * len(gens)
    active = list(enumerate(gens))
    while active:
        still = []
        for idx, g in active:
            try:
                next(g)
                still.append((idx, g))
            except StopIteration as stop:
                results[idx] = stop.value
        active = still
    return results


def _scan_kernel(rf_ref, vf_ref, kkf_ref, lwf_ref, kdf_ref, agf_ref,
                 rb_ref, vb_ref, kkb_ref, lwb_ref, kdb_ref, agb_ref,
                 yf_ref, yb_ref, hf_ref, hb_ref):
    @pl.when(pl.program_id(2) == 0)
    def _():
        hf_ref[...] = jnp.zeros_like(hf_ref)
        hb_ref[...] = jnp.zeros_like(hb_ref)

    L = SCAN_L
    n_chunks = TM // L
    fwd = (rf_ref, vf_ref, kkf_ref, lwf_ref, kdf_ref, agf_ref)
    bwd = (rb_ref, vb_ref, kkb_ref, lwb_ref, kdb_ref, agb_ref)
    f_rows = [slice(c * L, (c + 1) * L) for c in range(n_chunks)]
    b_rows = f_rows[::-1]
    pairs = [slice(p * PAIR_W, (p + 1) * PAIR_W) for p in range(SCAN_W // PAIR_W)]
    chains = ([(fwd, hf_ref, yf_ref, f_rows, cols, False) for cols in pairs]
              + [(bwd, hb_ref, yb_ref, b_rows, cols, True) for cols in pairs])
    head = lax.broadcasted_iota(jnp.int32, (L, PAIR_W), 1) >> 6
    for first in range(0, n_chunks, SCAN_ROUND):
        steps = range(first, first + SCAN_ROUND)
        jobs = [(chain, i) for i in steps for chain in chains]
        pre = _lockstep([_scan_chunk(*refs, rows[i], cols, rev) for (refs, _, _, rows, cols, rev), i in jobs])
        for ((_, h_ref, y_ref, rows, cols, _), i), (q_hat, y_c, m_b, c_m) in zip(jobs, pre):
            both = _dot(jnp.concatenate([q_hat, m_b], axis=0), _bd(_bf(h_ref[:, cols]), head))
            y_ref[rows[i], cols] = both[0:L] + y_c
            h_ref[:, cols] = both[L:2 * L] + c_m


def _rwkv_scan(r, v, kk, lw, kd, ag, n_ctx):
    B, N, _ = r.shape
    nt = N // TM
    nc = n_ctx // TM

    def back(s):
        return jnp.where(s < nc, nc - 1 - s, nt + nc - 1 - s)

    f_one = pl.BlockSpec((None, TM, SCAN_W), lambda b, g, s: (b, s, g))
    b_one = pl.BlockSpec((None, TM, SCAN_W), lambda b, g, s: (b, back(s), g))
    f_two = pl.BlockSpec((None, None, TM, SCAN_W), lambda b, g, s: (0, b, s, g))
    b_two = pl.BlockSpec((None, None, TM, SCAN_W), lambda b, g, s: (1, b, back(s), g))
    out = jax.ShapeDtypeStruct((B, N, D), F32)
    return pl.pallas_call(
        _scan_kernel,
        grid=(B, D // SCAN_W, nt),
        in_specs=[f_one, f_one, f_one, f_two, f_two, f_two, b_one, b_one, b_one, b_two, b_two, b_two],
        out_specs=[f_one, b_one],
        out_shape=[out, out],
        scratch_shapes=[pltpu.VMEM((RW_HEAD, SCAN_W), F32), pltpu.VMEM((RW_HEAD, SCAN_W), F32)],
        compiler_params=_params(3),
        name="rwkv_scan",
    )(r, v, kk, lw, kd, ag, r, v, kk, lw, kd, ag)


def _rwread_kernel(yf_ref, yb_ref, r_ref, v_ref, g_ref, kd0_ref, kd1_ref, vec_ref, e_ref, o_ref):
    inv = 1.0 / RW_HEAD
    y = yf_ref[...] + yb_ref[...]
    mu = _head_sum(y, e_ref) * inv
    yc = y - mu
    var = _head_sum(yc * yc, e_ref) * inv
    yn = yc * lax.rsqrt(var + RW_GN_EPS) * vec_ref[1:2, :] + vec_ref[2:3, :]
    f32 = lambda ref: ref[...].astype(F32)
    bonus = _head_sum(f32(r_ref) * (f32(kd0_ref) + f32(kd1_ref)) * vec_ref[0:1, :], e_ref) * f32(v_ref)
    o_ref[...] = _bf((yn + bonus) * f32(g_ref))


def _rwkv_readout(yf, yb, r, v, g, kd, vec, e, latent_only, n_ctx):
    B, N, _ = r.shape
    t0 = n_ctx // TM if latent_only else 0
    nt = N // TM - t0
    one = pl.BlockSpec((None, TM, D), lambda b, i: (b, i + t0, 0))
    pick = lambda d: pl.BlockSpec((None, None, TM, D), lambda b, i: (d, b, i + t0, 0))
    return pl.pallas_call(
        _rwread_kernel,
        grid=(B, nt),
        in_specs=[one, one, one, one, one, pick(0), pick(1), _resident(vec.shape), _resident(e.shape)],
        out_specs=pl.BlockSpec((None, TM, D), lambda b, i: (b, i, 0)),
        out_shape=jax.ShapeDtypeStruct((B, nt * TM, D), BF16),
        compiler_params=_params(2),
        name="rwkv_readout",
    )(yf, yb, r, v, g, kd, kd, vec, e)


def _rope_tables(n_lat, n_ctx):
    t = jnp.arange(n_lat)
    rowp = (t // GRID_W).astype(F32)
    colp = (t % GRID_W).astype(F32)

    def table(dim):
        n_freq = dim // 4
        inv = ROPE_BASE ** (-jnp.arange(n_freq, dtype=F32) / n_freq)
        ang = jnp.concatenate([rowp[:, None] * inv, colp[:, None] * inv], -1)
        cos, sin = jnp.cos(ang), jnp.sin(ang)
        reps = 128 // dim
        cos_t = jnp.tile(jnp.concatenate([cos, cos], -1), (1, reps))
        sin_t = jnp.tile(jnp.concatenate([-sin, sin], -1), (1, reps))
        cos_t = jnp.concatenate([jnp.ones((n_ctx, 128), F32), cos_t], 0)
        sin_t = jnp.concatenate([jnp.zeros((n_ctx, 128), F32), sin_t], 0)
        return cos_t, sin_t

    ca, sa = table(DA_QK)
    cb, sb = table(RT_QK)
    return ca, sa, cb, sb


def _pad_cols(w, n):
    return jnp.pad(w, ((0, 0), (0, n - w.shape[1])))


def _pad_rows(w, n):
    return jnp.pad(w, ((0, n - w.shape[0]), (0, 0)))


def kernel(x, c, ctx, c_ctx, mod_w, mod_b, ln1_g, ln1_b, ln2_g, ln2_b, ffn_w1, ffn_w3, ffn_w2, ev_w_in, ev_w_out, da_lam_q1, da_lam_k1, da_lam_q2, da_lam_k2, da_gn_g, rt_decay_logit, rw_mu, rw_wr, rw_wk, rw_wv, rw_wo, rw_w0, rw_w1, rw_w2, rw_a0, rw_a1, rw_a2, rw_v0, rw_v1, rw_v2, rw_g1, rw_g2, rw_kk, rw_ka, rw_rk, rw_lnx_g, rw_lnx_b):
    B, T, _ = x.shape
    n_ctx = ctx.shape[1]
    assert n_ctx == TM and T % TM == 0 and x.shape[2] == D and B <= 15
    xs = jnp.concatenate([ctx, x], axis=1)

    cc = jnp.zeros((16, D), F32).at[:B].set(c).at[B].set(c_ctx)
    mod_all = _adaln(cc, mod_w, mod_b)
    m_lat = mod_all[:, :B].reshape(DEPTH, B, 1, 6, D)
    m_ctx = jnp.broadcast_to(mod_all[:, B].reshape(DEPTH, 1, 1, 6, D), (DEPTH, B, 1, 6, D))
    mod_tab = jnp.concatenate([m_ctx, m_lat], axis=2)

    tabs = _rope_tables(T, n_ctx)
    lane_head = np.arange(HEADS4_W) // RW_HEAD
    head_ones = jnp.asarray(lane_head[:, None] == lane_head[None, :], BF16)

    vf = None
    for l in range(DEPTH):
        last = l == DEPTH - 1
        mod = mod_tab[l]
        ln = jnp.stack([ln1_g[l], ln1_b[l], ln2_g[l], ln2_b[l]])
        if l % 2 == 0:
            e = l // 2
            lam_init = 0.8 - 0.6 * math.exp(-0.3 * l)
            lam = (jnp.exp(jnp.sum(da_lam_q1[e] * da_lam_k1[e])) - jnp.exp(jnp.sum(da_lam_q2[e] * da_lam_k2[e]))
                   + lam_init).reshape(1).astype(F32)
            gn = (da_gn_g[e] * (1.0 - lam_init)).reshape(1, -1)
            log_gamma = jnp.log(jax.nn.sigmoid(rt_decay_logit[e].astype(F32)))
            aq, ak, av, bq, bk, bv, bg = _even_project(xs, mod, _bf(ev_w_in[e]), tabs)
            a_mix = _diff_attention(aq, ak, av, lam, gn, n_ctx)
            b_mix = _retention(bq, bk, bv, bg, log_gamma, n_ctx)
            mixes = (a_mix, b_mix)
            wo = _bf(ev_w_out[e])
        else:
            j = l // 2
            has_vres = j > 0
            pvec = jnp.zeros((16, D), F32)
            pvec = pvec.at[PV_MU:PV_MU + 6].set(rw_mu[j]).at[PV_W0:PV_W0 + 2].set(rw_w0[j])
            pvec = pvec.at[PV_A0:PV_A0 + 2].set(rw_a0[j]).at[PV_KK].set(rw_kk[j]).at[PV_KA].set(rw_ka[j])
            p = {
                "wr": _bf(rw_wr[j]), "wk": _bf(rw_wk[j]), "wv": _bf(rw_wv[j]),
                "w1": _bf(jnp.concatenate([_pad_cols(rw_w1[j, d], LORA_PAD) for d in range(2)], 1)),
                "w2": _bf(jnp.stack([_pad_rows(rw_w2[j, d], LORA_PAD) for d in range(2)])),
                "a1": _bf(jnp.concatenate([_pad_cols(rw_a1[j, d], LORA_PAD) for d in range(2)], 1)),
                "a2": _bf(jnp.stack([_pad_rows(rw_a2[j, d], LORA_PAD) for d in range(2)])),
                "g1": _bf(_pad_cols(rw_g1[j], GATE_PAD)), "g2": _bf(_pad_rows(rw_g2[j], GATE_PAD)),
                "e": head_ones,
            }
            if has_vres:
                pvec = pvec.at[PV_V0].set(rw_v0[j - 1])
                p["v1"] = _bf(_pad_cols(rw_v1[j - 1], LORA_PAD))
                p["v2"] = _bf(_pad_rows(rw_v2[j - 1], LORA_PAD))
            p["pv"] = pvec
            r, v, g, kk, lw, kd, ag = _rwkv_project(xs, mod, vf if has_vres else None, p)
            if not has_vres:
                vf = v
            yf, yb = _rwkv_scan(r, v, kk, lw, kd, ag, n_ctx)
            vec = jnp.zeros((8, D), F32).at[0].set(rw_rk[j].reshape(-1)).at[1].set(rw_lnx_g[j]).at[2].set(rw_lnx_b[j])
            mixes = (_rwkv_readout(yf, yb, r, v, g, kd, vec, head_ones, last, n_ctx),)
            wo = _bf(rw_wo[j])
        xs = _post_mixer(xs, mod, mixes, wo, _bf(ffn_w1[l]), _bf(ffn_w3[l]), _bf(ffn_w2[l]), ln, last, n_ctx)
    return xs
```

```python
import functools
import math

import jax
import jax.numpy as jnp
import numpy as np
from jax import lax
from jax.experimental import pallas as pl
from jax.experimental.pallas import tpu as pltpu

F32 = jnp.float32
BF16 = jnp.bfloat16

D = 1024
DEPTH = 4
GRID_W = 64
ALPHA = (2.0 * DEPTH) ** 0.25
LN_EPS = 1e-6
ROPE_BASE = 10000.0
D_FF = 2816
FF_CHUNKS = (768, 768, 768, 512)
DA_QK = 64
DA_HEADS = 4
RT_QK = 128
RT_HEADS = 4
RT_CHUNK = 128
Q_BLOCK = 128
ATT_ROWS = 768
EV_SEG = 512
RW_HEAD = 64
RW_GN_EPS = 64e-5
LORA_PAD = 128
GATE_PAD = 256
TM = 256
HALO = GRID_W
SCAN_L = 64
HEADS4_W = 4 * RW_HEAD
SCAN_W = 16 * RW_HEAD
PAIR_W = 2 * RW_HEAD
SCAN_ROUND = 4
VMEM_LIMIT = 56 * 1024 * 1024


def _dot(a, b):
    return jnp.dot(a, b, preferred_element_type=F32)


def _dot_nt(a, b):
    return lax.dot_general(a, b, (((1,), (1,)), ((), ())), preferred_element_type=F32)


def _dot_tn(a, b):
    return lax.dot_general(a, b, (((0,), (0,)), ((), ())), preferred_element_type=F32)


def _bf(x):
    return x.astype(BF16)


def _split2(x):
    hi = _bf(x)
    return hi, _bf(x - hi.astype(F32))


def _sigmoid(x):
    return 0.5 * jnp.tanh(0.5 * x) + 0.5


def _ln(x, g, b):
    mu = jnp.mean(x, -1, keepdims=True)
    xc = x - mu
    var = jnp.mean(xc * xc, -1, keepdims=True)
    return xc * lax.rsqrt(var + LN_EPS) * g + b


def _params(n_grid):
    return pltpu.CompilerParams(dimension_semantics=("arbitrary",) * n_grid,
                                vmem_limit_bytes=VMEM_LIMIT)


def _resident(shape):
    nd = len(shape)
    return pl.BlockSpec(shape, lambda *_: (0,) * nd, pipeline_mode=pl.Buffered(1))


def _mod_kernel(c_ref, w_ref, b_ref, o_ref):
    c = c_ref[...]
    act = _bf(c * _sigmoid(c))
    o_ref[...] = _dot(act, _bf(w_ref[...])) + b_ref[...]


def _adaln(cc, mod_w, mod_b):
    tn = 1536
    nt = mod_w.shape[2] // tn
    return pl.pallas_call(
        _mod_kernel,
        grid=(DEPTH, nt),
        in_specs=[pl.BlockSpec((16, D), lambda l, j: (0, 0)),
                  pl.BlockSpec((None, D, tn), lambda l, j: (l, 0, j)),
                  pl.BlockSpec((None, 1, tn), lambda l, j: (l, 0, j))],
        out_specs=pl.BlockSpec((None, 16, tn), lambda l, j: (l, 0, j)),
        out_shape=jax.ShapeDtypeStruct((DEPTH, 16, mod_w.shape[2]), F32),
        compiler_params=_params(2),
        name="adaln",
    )(cc, mod_w, mod_b.reshape(DEPTH, 1, -1))


def _evproj_kernel(x_ref, mod_ref, w_ref, ca_ref, sa_ref, cb_ref, sb_ref,
                   aq_ref, ak_ref, av_ref, bq_ref, bk_ref, bv_ref, bg_ref):
    h = _bf(x_ref[...] * (1.0 + mod_ref[1:2, :]) + mod_ref[0:1, :])
    ca, sa, cb, sb = ca_ref[...], sa_ref[...], cb_ref[...], sb_ref[...]
    lane = lax.broadcasted_iota(jnp.int32, (TM, 128), 1)
    first_half = (lane & (DA_QK // 2)) == 0

    def seg(j):
        return _dot(h, w_ref[:, j * EV_SEG:(j + 1) * EV_SEG])

    def rope_a(p, out_ref, scale):
        for j in range(EV_SEG // 128):
            blk = p[:, j * 128:(j + 1) * 128]
            sw = jnp.where(first_half, pltpu.roll(blk, 96, 1), pltpu.roll(blk, 32, 1))
            out_ref[:, j * 128:(j + 1) * 128] = _bf((blk * ca + sw * sa) * scale)

    def rope_b(p, out_ref):
        for j in range(EV_SEG // 128):
            blk = p[:, j * 128:(j + 1) * 128]
            out_ref[:, j * 128:(j + 1) * 128] = _bf(blk * cb + pltpu.roll(blk, 64, 1) * sb)

    rope_a(seg(0), aq_ref, DA_QK ** -0.5 * math.log2(math.e))
    rope_a(seg(1), ak_ref, 1.0)
    av_ref[...] = _bf(seg(2))
    rope_b(seg(3), bq_ref)
    rope_b(seg(4) * (RT_QK ** -0.5), bk_ref)
    bv_ref[...] = _bf(seg(5))
    bg_ref[...] = seg(6)


def _even_project(xs, mod, w_in, tabs):
    B, N, _ = xs.shape
    nt = N // TM
    tile = lambda w: pl.BlockSpec((None, TM, w), lambda b, i: (b, i, 0))
    tab = pl.BlockSpec((TM, 128), lambda b, i: (i, 0))
    outs = [jax.ShapeDtypeStruct((B, N, EV_SEG), BF16)] * 6 + [jax.ShapeDtypeStruct((B, N, EV_SEG), F32)]
    return pl.pallas_call(
        _evproj_kernel,
        grid=(B, nt),
        in_specs=[tile(D),
                  pl.BlockSpec((None, None, 6, D), lambda b, i: (b, jnp.minimum(i, 1), 0, 0)),
                  _resident(w_in.shape), tab, tab, tab, tab],
        out_specs=[tile(EV_SEG)] * 7,
        out_shape=outs,
        compiler_params=_params(2),
        name="even_project",
    )(xs, mod, w_in, *tabs)


def _attn_block(q, k_ref, v_ref, nk, lam, gn):
    lane = lax.broadcasted_iota(jnp.int32, q.shape, 1)
    zero = jnp.zeros_like(q)
    qq = jnp.concatenate([jnp.where(lane < DA_QK, q, zero), jnp.where(lane >= DA_QK, q, zero)], axis=0)
    s = _dot_nt(qq, k_ref[0:nk, :])
    yield
    e = jnp.exp2(s - jnp.max(s, -1, keepdims=True))
    inv = 1.0 / jnp.sum(e, -1, keepdims=True)
    eb = _bf(e)
    yield
    ev = _dot(eb, v_ref[0:nk, :])
    o = ev[0:Q_BLOCK] * inv[0:Q_BLOCK] - ev[Q_BLOCK:2 * Q_BLOCK] * (lam * inv[Q_BLOCK:2 * Q_BLOCK])
    return o * lax.rsqrt(jnp.mean(o * o, -1, keepdims=True) + LN_EPS) * gn


def _attn_kernel(lam_ref, q_ref, k_ref, v_ref, gn_ref, o_ref, *, n_ctx, n_all):
    i = pl.program_id(2)
    lam = lam_ref[0]
    gn = gn_ref[...]
    subs = [slice(j * Q_BLOCK, (j + 1) * Q_BLOCK) for j in range(ATT_ROWS // Q_BLOCK)]
    n_ctx_subs = n_ctx // Q_BLOCK

    def run(first_tile):
        nks = [n_ctx if (first_tile and j < n_ctx_subs) else n_all for j in range(len(subs))]
        outs = _lockstep([_attn_block(q_ref[rows, :], k_ref, v_ref, nk, lam, gn) for rows, nk in zip(subs, nks)])
        for rows, o in zip(subs, outs):
            o_ref[rows, :] = o.astype(o_ref.dtype)

    @pl.when(i == 0)
    def _():
        run(True)

    @pl.when(i > 0)
    def _():
        run(False)


def _diff_attention(aq, ak, av, lam, gn, n_ctx):
    B, N, _ = aq.shape
    assert N % ATT_ROWS == 0 and n_ctx <= ATT_ROWS and n_ctx % Q_BLOCK == 0
    kv = pl.BlockSpec((None, N, 128), lambda b, h, i: (b, 0, h))
    qo = pl.BlockSpec((None, ATT_ROWS, 128), lambda b, h, i: (b, i, h))
    return pl.pallas_call(
        functools.partial(_attn_kernel, n_ctx=n_ctx, n_all=N),
        grid=(B, DA_HEADS, N // ATT_ROWS),
        in_specs=[pl.BlockSpec(memory_space=pltpu.SMEM), qo, kv, kv,
                  pl.BlockSpec((1, 128), lambda b, h, i: (0, h))],
        out_specs=qo,
        out_shape=jax.ShapeDtypeStruct((B, N, EV_SEG), BF16),
        compiler_params=_params(3),
        name="diff_attention",
    )(lam, aq, ak, av, gn)


def _ret_kernel(lg_ref, q_ref, k_ref, v_ref, g_ref, o_ref, sf_ref, sb_ref, accf_ref, accb_ref,
                *, n_chunks, n_ctx_chunks):
    h = pl.program_id(1)
    C = RT_CHUNK
    ii = lax.broadcasted_iota(jnp.int32, (C, C), 0)
    jj = lax.broadcasted_iota(jnp.int32, (C, C), 1)
    idx = lax.broadcasted_iota(jnp.int32, (C, 1), 0).astype(F32)
    consts = []
    for d in range(2):
        lg = lg_ref[d, h]
        diff = ((ii - jj) if d == 0 else (jj - ii)).astype(F32)
        dec = jnp.where(diff >= 0, jnp.exp(lg * jnp.maximum(diff, 0.0)), 0.0)
        if d == 0:
            zeta = jnp.exp(lg * (C - 1.0 - idx))
            xi = jnp.exp(lg * (idx + 1.0))
        else:
            zeta = jnp.exp(lg * idx)
            xi = jnp.exp(lg * (C - idx))
        consts.append((dec, zeta, xi, jnp.exp(lg * jnp.full((1, 1), float(C), F32))))
    sf_ref[...] = jnp.zeros_like(sf_ref)
    sb_ref[...] = jnp.zeros_like(sb_ref)

    def chunk(s, d, s_ref, acc_ref):
        dec, zeta, xi, g_chunk = consts[d]
        if d == 0:
            c = s
        else:
            c = jnp.where(s < n_ctx_chunks, n_ctx_chunks - 1 - s, n_chunks + n_ctx_chunks - 1 - s)
        rows = pl.ds(pl.multiple_of(c * C, C), C)
        qc = q_ref[rows, :]
        kc = k_ref[rows, :]
        vc = v_ref[rows, :]
        state = s_ref[...]
        inner = _dot_nt(qc, kc)
        cross = _dot(qc, _bf(state))
        kz = _bf(kc.astype(F32) * zeta)
        s_ref[...] = g_chunk * state + _dot_tn(kz, vc)
        yield
        acc_ref[rows, :] = _dot(_bf(inner * dec), vc) + cross * xi

    def step(s, carry):
        _lockstep([chunk(s, 0, sf_ref, accf_ref), chunk(s, 1, sb_ref, accb_ref)])
        return carry

    lax.fori_loop(0, n_chunks, step, 0, unroll=2)
    b = accf_ref[...] + accb_ref[...]
    g = g_ref[...]
    b = b * lax.rsqrt(jnp.mean(b * b, -1, keepdims=True) + LN_EPS)
    o_ref[...] = (b * (g * _sigmoid(g))).astype(o_ref.dtype)


def _retention(bq, bk, bv, bg, log_gamma, n_ctx):
    B, N, _ = bq.shape
    blk = pl.BlockSpec((None, N, 128), lambda b, h: (b, 0, h))
    return pl.pallas_call(
        functools.partial(_ret_kernel, n_chunks=N // RT_CHUNK, n_ctx_chunks=n_ctx // RT_CHUNK),
        grid=(B, RT_HEADS),
        in_specs=[pl.BlockSpec(memory_space=pltpu.SMEM), blk, blk, blk, blk],
        out_specs=blk,
        out_shape=jax.ShapeDtypeStruct((B, N, EV_SEG), BF16),
        scratch_shapes=[pltpu.VMEM((RT_QK, 128), F32), pltpu.VMEM((RT_QK, 128), F32),
                        pltpu.VMEM((N, 128), F32), pltpu.VMEM((N, 128), F32)],
        compiler_params=_params(2),
        name="retention",
    )(log_gamma, bq, bk, bv, bg)


def _post_kernel(*refs, mix_widths, n_sub, has_ctx):
    n_mix = len(mix_widths)
    x_ref, mod_ref = refs[0], refs[1]
    mix_refs = refs[2:2 + n_mix]
    wo_ref, w1_ref, w3_ref, w2_ref, ln_ref, o_ref, x1_ref, hm_ref = refs[2 + n_mix:]
    i = pl.program_id(1)
    o = None
    off = 0
    for m_ref, w in zip(mix_refs, mix_widths):
        part = _dot(m_ref[...], wo_ref[off:off + w, :])
        o = part if o is None else o + part
        off += w
    subs = [slice(j * TM, (j + 1) * TM) for j in range(n_sub)]
    mods = []
    for j, rows in enumerate(subs):
        seg = jnp.where(i == 0, 0, 1) if (has_ctx and j == 0) else 1
        m = mod_ref[seg]
        mods.append(m)
        x1 = _ln(ALPHA * x_ref[rows, :] + m[2:3, :] * o[rows, :], ln_ref[0:1, :], ln_ref[1:2, :])
        x1_ref[rows, :] = x1
        hm_ref[rows, :] = _bf(x1 * (1.0 + m[4:5, :]) + m[3:4, :])
    hm = hm_ref[...]
    f = None
    off = 0
    for w in FF_CHUNKS:
        cols = slice(off, off + w)
        u = _dot(hm, w1_ref[:, cols])
        t = _dot(hm, w3_ref[:, cols])
        part = _dot(_bf(u * _sigmoid(u) * t), w2_ref[cols, :])
        f = part if f is None else f + part
        off += w
    for m, rows in zip(mods, subs):
        o_ref[rows, :] = _ln(ALPHA * x1_ref[rows, :] + m[5:6, :] * f[rows, :], ln_ref[2:3, :], ln_ref[3:4, :])


def _post_mixer(xs, mod, mixes, wo, w1, w3, w2, ln, latent_only, n_ctx):
    B, N, _ = xs.shape
    rows_out = N - n_ctx if latent_only else N
    n_sub = 1 if latent_only else max(s for s in (3, 2, 1) if N % (s * TM) == 0)
    tm = n_sub * TM
    t0 = n_ctx // tm if latent_only else 0
    nt = rows_out // tm
    in_tile = lambda w, off=t0: pl.BlockSpec((None, tm, w), lambda b, i: (b, i + off, 0))
    widths = tuple(m.shape[-1] for m in mixes)
    mix_off = [t0 if m.shape[1] == N else 0 for m in mixes]
    return pl.pallas_call(
        functools.partial(_post_kernel, mix_widths=widths, n_sub=n_sub, has_ctx=not latent_only),
        grid=(B, nt),
        in_specs=[in_tile(D), pl.BlockSpec((None, 2, 6, D), lambda b, i: (b, 0, 0, 0))]
                 + [in_tile(w, off) for w, off in zip(widths, mix_off)]
                 + [_resident(wo.shape), _resident(w1.shape), _resident(w3.shape), _resident(w2.shape),
                    _resident(ln.shape)],
        out_specs=pl.BlockSpec((None, tm, D), lambda b, i: (b, i, 0)),
        out_shape=jax.ShapeDtypeStruct((B, rows_out, D), F32),
        scratch_shapes=[pltpu.VMEM((tm, D), F32), pltpu.VMEM((tm, D), BF16)],
        compiler_params=_params(2),
        name="post_mixer",
    )(xs, mod, *mixes, wo, w1, w3, w2, ln)


PV_MU, PV_W0, PV_A0, PV_KK, PV_KA, PV_V0 = 0, 6, 8, 10, 11, 12


def _head_sum(x, e_ref):
    rows = x.shape[0]
    hi, lo = _split2(x)
    cols = []
    for j in range(x.shape[1] // HEADS4_W):
        sl = slice(j * HEADS4_W, (j + 1) * HEADS4_W)
        both = _dot(jnp.concatenate([hi[:, sl], lo[:, sl]], axis=0), e_ref[...])
        cols.append(both[0:rows] + both[rows:2 * rows])
    return jnp.concatenate(cols, axis=1)


def _rwproj_kernel(*refs, has_vres, n_tiles):
    if has_vres:
        (xp_ref, xc_ref, xn_ref, mod_ref, vf_ref, pv_ref, wr_ref, wk_ref, wv_ref, w1_ref, w2_ref,
         a1_ref, a2_ref, g1_ref, g2_ref, e_ref, v1_ref, v2_ref,
         r_ref, v_ref, g_ref, kk_ref, lw_ref, kd_ref, ag_ref, hbuf, hs) = refs
    else:
        (xp_ref, xc_ref, xn_ref, mod_ref, pv_ref, wr_ref, wk_ref, wv_ref, w1_ref, w2_ref,
         a1_ref, a2_ref, g1_ref, g2_ref, e_ref,
         r_ref, v_ref, g_ref, kk_ref, lw_ref, kd_ref, ag_ref, hbuf, hs) = refs
    i = pl.program_id(1)
    one_scale = 1.0 + mod_ref[1:2, :]
    shift = mod_ref[0:1, :]
    hbuf[0:HALO, :] = xp_ref[...] * one_scale + shift
    hbuf[HALO:HALO + TM, :] = xc_ref[...] * one_scale + shift
    hbuf[HALO + TM:, :] = xn_ref[...] * one_scale + shift
    row = lax.broadcasted_iota(jnp.int32, (TM, 1), 0)
    q = D // 4

    @pl.when(i == 0)
    def _():
        hs[:, 0:2 * q] = jnp.where(row == 0, 0.0, hbuf[HALO - 1:HALO - 1 + TM, 0:2 * q])
        hs[:, 2 * q:] = jnp.where(row == TM - 1, 0.0, hbuf[HALO + 1:HALO + 1 + TM, 2 * q:])

    @pl.when(i > 0)
    def _():
        col = row & (GRID_W - 1)
        hs[:, 0:q] = jnp.where(col == 0, 0.0, hbuf[HALO - 1:HALO - 1 + TM, 0:q])
        hs[:, q:2 * q] = jnp.where(col == GRID_W - 1, 0.0, hbuf[HALO + 1:HALO + 1 + TM, q:2 * q])
        top = jnp.logical_and(i == 1, row < GRID_W)
        hs[:, 2 * q:3 * q] = jnp.where(top, 0.0, hbuf[0:TM, 2 * q:3 * q])
        bottom = jnp.logical_and(i == n_tiles - 1, row >= TM - GRID_W)
        hs[:, 3 * q:] = jnp.where(bottom, 0.0, hbuf[2 * HALO:2 * HALO + TM, 3 * q:])

    h = hbuf[HALO:HALO + TM, :]
    xx = hs[...] - h
    mix = lambda j: _bf(h + xx * pv_ref[PV_MU + j:PV_MU + j + 1, :])
    xr, xw, xk, xv, xa, xg = (mix(j) for j in range(6))
    r = _dot(xr, wr_ref[...])
    k = _dot(xk, wk_ref[...])
    v = _dot(xv, wv_ref[...])
    if has_vres:
        gate = _sigmoid(pv_ref[PV_V0:PV_V0 + 1, :] + _dot(_bf(_dot(xv, v1_ref[...])), v2_ref[...]))
        v = v + (vf_ref[...].astype(F32) - v) * gate
    g = _dot(_bf(_sigmoid(_dot(xg, g1_ref[...]))), g2_ref[...])
    kx = k * pv_ref[PV_KK:PV_KK + 1, :]
    kkn = kx * lax.rsqrt(jnp.maximum(_head_sum(kx * kx, e_ref), 1e-24))
    r_ref[...] = _bf(r)
    v_ref[...] = _bf(v)
    g_ref[...] = _bf(g)
    kk_ref[...] = _bf(kkn)
    tw = jnp.tanh(_dot(xw, w1_ref[...]))
    ta = _dot(xa, a1_ref[...])
    for d in range(2):
        cols = slice(d * LORA_PAD, (d + 1) * LORA_PAD)
        w_pre = pv_ref[PV_W0 + d:PV_W0 + d + 1, :] + _dot(_bf(tw[:, cols]), w2_ref[d])
        lw_ref[d] = -math.exp(-0.5) * _sigmoid(w_pre)
        ag = _sigmoid(pv_ref[PV_A0 + d:PV_A0 + d + 1, :] + _dot(_bf(ta[:, cols]), a2_ref[d]))
        ag_ref[d] = _bf(ag)
        kd_ref[d] = _bf(k * (1.0 + (ag - 1.0) * pv_ref[PV_KA:PV_KA + 1, :]))


def _rwkv_project(xs, mod, vf, p):
    B, N, _ = xs.shape
    nt = N // TM
    hp = TM // HALO
    n_halo = N // HALO
    tile = pl.BlockSpec((None, TM, D), lambda b, i: (b, i, 0))
    tile2 = pl.BlockSpec((2, None, TM, D), lambda b, i: (0, b, i, 0))
    in_specs = [pl.BlockSpec((None, HALO, D), lambda b, i: (b, jnp.maximum(i * hp - 1, 0), 0)),
                tile,
                pl.BlockSpec((None, HALO, D), lambda b, i: (b, jnp.minimum((i + 1) * hp, n_halo - 1), 0)),
                pl.BlockSpec((None, None, 6, D), lambda b, i: (b, jnp.minimum(i, 1), 0, 0))]
    args = [xs, xs, xs, mod]
    has_vres = vf is not None
    if has_vres:
        in_specs.append(tile)
        args.append(vf)
    names = ["pv", "wr", "wk", "wv", "w1", "w2", "a1", "a2", "g1", "g2", "e"] + (["v1", "v2"] if has_vres else [])
    for nm in names:
        in_specs.append(_resident(p[nm].shape))
        args.append(p[nm])
    one = jax.ShapeDtypeStruct((B, N, D), BF16)
    two = jax.ShapeDtypeStruct((2, B, N, D), BF16)
    return pl.pallas_call(
        functools.partial(_rwproj_kernel, has_vres=has_vres, n_tiles=nt),
        grid=(B, nt),
        in_specs=in_specs,
        out_specs=[tile, tile, tile, tile, tile2, tile2, tile2],
        out_shape=[one, one, one, one, jax.ShapeDtypeStruct((2, B, N, D), F32), two, two],
        scratch_shapes=[pltpu.VMEM((TM + 2 * HALO, D), F32), pltpu.VMEM((TM, D), F32)],
        compiler_params=_params(2),
        name="rwkv_project",
    )(*args)


def _bd(x, head):
    zero = jnp.zeros_like(x)
    return jnp.concatenate([jnp.where(head == hh, x, zero) for hh in range(PAIR_W // RW_HEAD)], axis=0)


def _diag_blocks(full, head):
    return jnp.where(head == 0, full[0:RW_HEAD], full[RW_HEAD:2 * RW_HEAD])


def _scan_chunk(r_ref, v_ref, kk_ref, lw_ref, kd_ref, ag_ref, rows, cols, reverse):
    L, W = SCAN_L, PAIR_W
    row = lax.broadcasted_iota(jnp.int32, (L, W), 0)
    lane = lax.broadcasted_iota(jnp.int32, (L, W), 1)
    pos = lane & (RW_HEAD - 1)
    head = lane >> 6
    strict = (pos > row) if reverse else (pos < row)
    incl = (pos >= row) if reverse else (pos <= row)
    ident = pos == row
    bd = lambda x: _bd(x, head)

    load = lambda ref: ref[rows, cols].astype(F32)
    lw = load(lw_ref)
    cum = lw
    step = 1
    while step < L:
        if reverse:
            cum = cum + jnp.where(row < L - step, pltpu.roll(cum, L - step, 0), 0.0)
        else:
            cum = cum + jnp.where(row >= step, pltpu.roll(cum, step, 0), 0.0)
        step *= 2
    gam = jnp.exp(cum)
    gam_inv = jnp.exp(-cum)
    kk = load(kk_ref)
    a_t = _bf(-kk * jnp.exp(cum - lw))
    r_t = load(r_ref) * gam
    b_t = kk * load(ag_ref) * gam_inv
    k_t = load(kd_ref) * gam_inv
    vb = v_ref[rows, cols]
    g_last = gam[0:1, :] if reverse else gam[L - 1:L, :]
    bd_v = bd(vb)
    g4 = _dot_nt(jnp.concatenate([a_t, _bf(r_t)], axis=0),
                 jnp.concatenate([bd(_bf(b_t)), bd(_bf(k_t))], axis=0))
    yield
    n_m = jnp.where(strict, g4[0:L, 0:W], 0.0)
    p_m = jnp.where(strict, g4[0:L, W:2 * W], 0.0)
    rb = _bf(jnp.where(incl, g4[L:2 * L, 0:W], 0.0))
    rk = _bf(jnp.where(incl, g4[L:2 * L, W:2 * W], 0.0))
    pv = _dot(_bf(p_m), bd_v)
    t_m = jnp.where(ident, 1.0, 0.0)
    m = 1
    while m < L:
        later, earlier = (pos, row) if reverse else (row, pos)
        coupling = jnp.logical_and(jnp.logical_and((later & m) != 0, (earlier & m) == 0),
                                   (row // (2 * m)) == (pos // (2 * m)))
        n_off = jnp.where(coupling, n_m, 0.0)
        if m == 1:
            t_m = t_m + n_off
        else:
            t_b = _bf(t_m)
            half = _bf(_dot(t_b, bd(_bf(n_off))))
            yield
            t_m = t_m + _dot(half, bd(t_b))
            yield
        m *= 2
    t_m = _bf(t_m)
    tw = _dot(t_m, jnp.concatenate([bd(a_t), bd(_bf(pv))], axis=1))
    yield
    a_hat = _bf(tw[:, 0:W])
    w_b = _bf(tw[:, W:2 * W])
    q_hat = _bf(r_t + _dot(rb, bd(a_hat)))
    y_c = _dot(jnp.concatenate([rb, rk], axis=1), jnp.concatenate([bd(w_b), bd_v], axis=0))
    b_g = _bf(b_t * g_last)
    k_g = _bf(k_t * g_last)
    m_m = jnp.where(ident, g_last, 0.0) + _diag_blocks(_dot_tn(b_g, a_hat), head)
    c_m = _diag_blocks(_dot_tn(jnp.concatenate([b_g, k_g], axis=0), jnp.concatenate([w_b, vb], axis=0)), head)
    return q_hat, y_c, _bf(m_m), c_m


def _lockstep(gens):
    results = [None] * len(gens)
    active = list(enumerate(gens))
    while active:
        still = []
        for idx, g in active:
            try:
                next(g)
                still.append((idx, g))
            except StopIteration as stop:
                results[idx] = stop.value
        active = still
    return results


def _scan_kernel(rf_ref, vf_ref, kkf_ref, lwf_ref, kdf_ref, agf_ref,
                 rb_ref, vb_ref, kkb_ref, lwb_ref, kdb_ref, agb_ref,
                 yf_ref, yb_ref, hf_ref, hb_ref):
    @pl.when(pl.program_id(2) == 0)
    def _():
        hf_ref[...] = jnp.zeros_like(hf_ref)
        hb_ref[...] = jnp.zeros_like(hb_ref)

    L = SCAN_L
    n_chunks = TM // L
    fwd = (rf_ref, vf_ref, kkf_ref, lwf_ref, kdf_ref, agf_ref)
    bwd = (rb_ref, vb_ref, kkb_ref, lwb_ref, kdb_ref, agb_ref)
    f_rows = [slice(c * L, (c + 1) * L) for c in range(n_chunks)]
    b_rows = f_rows[::-1]
    pairs = [slice(p * PAIR_W, (p + 1) * PAIR_W) for p in range(SCAN_W // PAIR_W)]
    chains = ([(fwd, hf_ref, yf_ref, f_rows, cols, False) for cols in pairs]
              + [(bwd, hb_ref, yb_ref, b_rows, cols, True) for cols in pairs])
    head = lax.broadcasted_iota(jnp.int32, (L, PAIR_W), 1) >> 6
    for first in range(0, n_chunks, SCAN_ROUND):
        steps = range(first, first + SCAN_ROUND)
        jobs = [(chain, i) for i in steps for chain in chains]
        pre = _lockstep([_scan_chunk(*refs, rows[i], cols, rev) for (refs, _, _, rows, cols, rev), i in jobs])
        for ((_, h_ref, y_ref, rows, cols, _), i), (q_hat, y_c, m_b, c_m) in zip(jobs, pre):
            both = _dot(jnp.concatenate([q_hat, m_b], axis=0), _bd(_bf(h_ref[:, cols]), head))
            y_ref[rows[i], cols] = (both[0:L] + y_c).astype(y_ref.dtype)
            h_ref[:, cols] = both[L:2 * L] + c_m


def _rwkv_scan(r, v, kk, lw, kd, ag, n_ctx):
    B, N, _ = r.shape
    nt = N // TM
    nc = n_ctx // TM

    def back(s):
        return jnp.where(s < nc, nc - 1 - s, nt + nc - 1 - s)

    f_one = pl.BlockSpec((None, TM, SCAN_W), lambda b, g, s: (b, s, g))
    b_one = pl.BlockSpec((None, TM, SCAN_W), lambda b, g, s: (b, back(s), g))
    f_two = pl.BlockSpec((None, None, TM, SCAN_W), lambda b, g, s: (0, b, s, g))
    b_two = pl.BlockSpec((None, None, TM, SCAN_W), lambda b, g, s: (1, b, back(s), g))
    out = jax.ShapeDtypeStruct((B, N, D), BF16)
    return pl.pallas_call(
        _scan_kernel,
        grid=(B, D // SCAN_W, nt),
        in_specs=[f_one, f_one, f_one, f_two, f_two, f_two, b_one, b_one, b_one, b_two, b_two, b_two],
        out_specs=[f_one, b_one],
        out_shape=[out, out],
        scratch_shapes=[pltpu.VMEM((RW_HEAD, SCAN_W), F32), pltpu.VMEM((RW_HEAD, SCAN_W), F32)],
        compiler_params=_params(3),
        name="rwkv_scan",
    )(r, v, kk, lw, kd, ag, r, v, kk, lw, kd, ag)


def _rwread_kernel(yf_ref, yb_ref, r_ref, v_ref, g_ref, kd0_ref, kd1_ref, vec_ref, e_ref, o_ref):
    inv = 1.0 / RW_HEAD
    y = yf_ref[...].astype(F32) + yb_ref[...].astype(F32)
    mu = _head_sum(y, e_ref) * inv
    yc = y - mu
    var = _head_sum(yc * yc, e_ref) * inv
    yn = yc * lax.rsqrt(var + RW_GN_EPS) * vec_ref[1:2, :] + vec_ref[2:3, :]
    f32 = lambda ref: ref[...].astype(F32)
    bonus = _head_sum(f32(r_ref) * (f32(kd0_ref) + f32(kd1_ref)) * vec_ref[0:1, :], e_ref) * f32(v_ref)
    o_ref[...] = _bf((yn + bonus) * f32(g_ref))


def _rwkv_readout(yf, yb, r, v, g, kd, vec, e, latent_only, n_ctx):
    B, N, _ = r.shape
    t0 = n_ctx // TM if latent_only else 0
    nt = N // TM - t0
    one = pl.BlockSpec((None, TM, D), lambda b, i: (b, i + t0, 0))
    pick = lambda d: pl.BlockSpec((None, None, TM, D), lambda b, i: (d, b, i + t0, 0))
    return pl.pallas_call(
        _rwread_kernel,
        grid=(B, nt),
        in_specs=[one, one, one, one, one, pick(0), pick(1), _resident(vec.shape), _resident(e.shape)],
        out_specs=pl.BlockSpec((None, TM, D), lambda b, i: (b, i, 0)),
        out_shape=jax.ShapeDtypeStruct((B, nt * TM, D), BF16),
        compiler_params=_params(2),
        name="rwkv_readout",
    )(yf, yb, r, v, g, kd, kd, vec, e)


def _rope_tables(n_lat, n_ctx):
    t = jnp.arange(n_lat)
    rowp = (t // GRID_W).astype(F32)
    colp = (t % GRID_W).astype(F32)

    def table(dim):
        n_freq = dim // 4
        inv = ROPE_BASE ** (-jnp.arange(n_freq, dtype=F32) / n_freq)
        ang = jnp.concatenate([rowp[:, None] * inv, colp[:, None] * inv], -1)
        cos, sin = jnp.cos(ang), jnp.sin(ang)
        reps = 128 // dim
        cos_t = jnp.tile(jnp.concatenate([cos, cos], -1), (1, reps))
        sin_t = jnp.tile(jnp.concatenate([-sin, sin], -1), (1, reps))
        cos_t = jnp.concatenate([jnp.ones((n_ctx, 128), F32), cos_t], 0)
        sin_t = jnp.concatenate([jnp.zeros((n_ctx, 128), F32), sin_t], 0)
        return cos_t, sin_t

    ca, sa = table(DA_QK)
    cb, sb = table(RT_QK)
    return ca, sa, cb, sb


def _pad_cols(w, n):
    return jnp.pad(w, ((0, 0), (0, n - w.shape[1])))


def _pad_rows(w, n):
    return jnp.pad(w, ((0, n - w.shape[0]), (0, 0)))


def kernel(x, c, ctx, c_ctx, mod_w, mod_b, ln1_g, ln1_b, ln2_g, ln2_b, ffn_w1, ffn_w3, ffn_w2, ev_w_in, ev_w_out, da_lam_q1, da_lam_k1, da_lam_q2, da_lam_k2, da_gn_g, rt_decay_logit, rw_mu, rw_wr, rw_wk, rw_wv, rw_wo, rw_w0, rw_w1, rw_w2, rw_a0, rw_a1, rw_a2, rw_v0, rw_v1, rw_v2, rw_g1, rw_g2, rw_kk, rw_ka, rw_rk, rw_lnx_g, rw_lnx_b):
    B, T, _ = x.shape
    n_ctx = ctx.shape[1]
    assert n_ctx == TM and T % TM == 0 and x.shape[2] == D and B <= 15
    xs = jnp.concatenate([ctx, x], axis=1)

    cc = jnp.zeros((16, D), F32).at[:B].set(c).at[B].set(c_ctx)
    mod_all = _adaln(cc, mod_w, mod_b)
    m_lat = mod_all[:, :B].reshape(DEPTH, B, 1, 6, D)
    m_ctx = jnp.broadcast_to(mod_all[:, B].reshape(DEPTH, 1, 1, 6, D), (DEPTH, B, 1, 6, D))
    mod_tab = jnp.concatenate([m_ctx, m_lat], axis=2)

    tabs = _rope_tables(T, n_ctx)
    lane_head = np.arange(HEADS4_W) // RW_HEAD
    head_ones = jnp.asarray(lane_head[:, None] == lane_head[None, :], BF16)

    vf = None
    for l in range(DEPTH):
        last = l == DEPTH - 1
        mod = mod_tab[l]
        ln = jnp.stack([ln1_g[l], ln1_b[l], ln2_g[l], ln2_b[l]])
        if l % 2 == 0:
            e = l // 2
            lam_init = 0.8 - 0.6 * math.exp(-0.3 * l)
            lam = (jnp.exp(jnp.sum(da_lam_q1[e] * da_lam_k1[e])) - jnp.exp(jnp.sum(da_lam_q2[e] * da_lam_k2[e]))
                   + lam_init).reshape(1).astype(F32)
            gn = (da_gn_g[e] * (1.0 - lam_init)).reshape(1, -1)
            log_gamma = jnp.log(jax.nn.sigmoid(rt_decay_logit[e].astype(F32)))
            aq, ak, av, bq, bk, bv, bg = _even_project(xs, mod, _bf(ev_w_in[e]), tabs)
            a_mix = _diff_attention(aq, ak, av, lam, gn, n_ctx)
            b_mix = _retention(bq, bk, bv, bg, log_gamma, n_ctx)
            mixes = (a_mix, b_mix)
            wo = _bf(ev_w_out[e])
        else:
            j = l // 2
            has_vres = j > 0
            pvec = jnp.zeros((16, D), F32)
            pvec = pvec.at[PV_MU:PV_MU + 6].set(rw_mu[j]).at[PV_W0:PV_W0 + 2].set(rw_w0[j])
            pvec = pvec.at[PV_A0:PV_A0 + 2].set(rw_a0[j]).at[PV_KK].set(rw_kk[j]).at[PV_KA].set(rw_ka[j])
            p = {
                "wr": _bf(rw_wr[j]), "wk": _bf(rw_wk[j]), "wv": _bf(rw_wv[j]),
                "w1": _bf(jnp.concatenate([_pad_cols(rw_w1[j, d], LORA_PAD) for d in range(2)], 1)),
                "w2": _bf(jnp.stack([_pad_rows(rw_w2[j, d], LORA_PAD) for d in range(2)])),
                "a1": _bf(jnp.concatenate([_pad_cols(rw_a1[j, d], LORA_PAD) for d in range(2)], 1)),
                "a2": _bf(jnp.stack([_pad_rows(rw_a2[j, d], LORA_PAD) for d in range(2)])),
                "g1": _bf(_pad_cols(rw_g1[j], GATE_PAD)), "g2": _bf(_pad_rows(rw_g2[j], GATE_PAD)),
                "e": head_ones,
            }
            if has_vres:
                pvec = pvec.at[PV_V0].set(rw_v0[j - 1])
                p["v1"] = _bf(_pad_cols(rw_v1[j - 1], LORA_PAD))
                p["v2"] = _bf(_pad_rows(rw_v2[j - 1], LORA_PAD))
            p["pv"] = pvec
            r, v, g, kk, lw, kd, ag = _rwkv_project(xs, mod, vf if has_vres else None, p)
            if not has_vres:
                vf = v
            yf, yb = _rwkv_scan(r, v, kk, lw, kd, ag, n_ctx)
            vec = jnp.zeros((8, D), F32).at[0].set(rw_rk[j].reshape(-1)).at[1].set(rw_lnx_g[j]).at[2].set(rw_lnx_b[j])
            mixes = (_rwkv_readout(yf, yb, r, v, g, kd, vec, head_ones, last, n_ctx),)
            wo = _bf(rw_wo[j])
        xs = _post_mixer(xs, mod, mixes, wo, _bf(ffn_w1[l]), _bf(ffn_w3[l]), _bf(ffn_w2[l]), ln, last, n_ctx)
    return xs
```

```python
import functools
import math

import jax
import jax.numpy as jnp
import numpy as np
from jax import lax
from jax.experimental import pallas as pl
from jax.experimental.pallas import tpu as pltpu

F32 = jnp.float32
BF16 = jnp.bfloat16

D = 1024
DEPTH = 4
GRID_W = 64
ALPHA = (2.0 * DEPTH) ** 0.25
LN_EPS = 1e-6
ROPE_BASE = 10000.0
D_FF = 2816
FF_CHUNKS = (768, 768, 768, 512)
DA_QK = 64
DA_HEADS = 4
RT_QK = 128
RT_HEADS = 4
RT_CHUNK = 128
Q_BLOCK = 128
ATT_ROWS = 768
EV_SEG = 512
RW_HEAD = 64
RW_GN_EPS = 64e-5
LORA_PAD = 128
GATE_PAD = 256
TM = 256
HALO = GRID_W
SCAN_L = 64
HEADS4_W = 4 * RW_HEAD
SCAN_W = 16 * RW_HEAD
PAIR_W = 2 * RW_HEAD
SCAN_ROUND = 4
VMEM_LIMIT = 56 * 1024 * 1024


def _dot(a, b):
    return jnp.dot(a, b, preferred_element_type=F32)


def _dot_nt(a, b):
    return lax.dot_general(a, b, (((1,), (1,)), ((), ())), preferred_element_type=F32)


def _dot_tn(a, b):
    return lax.dot_general(a, b, (((0,), (0,)), ((), ())), preferred_element_type=F32)


def _bf(x):
    return x.astype(BF16)


def _split2(x):
    hi = _bf(x)
    return hi, _bf(x - hi.astype(F32))


def _sigmoid(x):
    return 0.5 * jnp.tanh(0.5 * x) + 0.5


def _ln(x, g, b):
    mu = jnp.mean(x, -1, keepdims=True)
    xc = x - mu
    var = jnp.mean(xc * xc, -1, keepdims=True)
    return xc * lax.rsqrt(var + LN_EPS) * g + b


def _params(n_grid):
    return pltpu.CompilerParams(dimension_semantics=("arbitrary",) * n_grid,
                                vmem_limit_bytes=VMEM_LIMIT)


def _resident(shape):
    nd = len(shape)
    return pl.BlockSpec(shape, lambda *_: (0,) * nd, pipeline_mode=pl.Buffered(1))


def _mod_kernel(c_ref, w_ref, b_ref, o_ref):
    c = c_ref[...]
    act = _bf(c * _sigmoid(c))
    o_ref[...] = _dot(act, _bf(w_ref[...])) + b_ref[...]


def _adaln(cc, mod_w, mod_b):
    tn = 1536
    nt = mod_w.shape[2] // tn
    return pl.pallas_call(
        _mod_kernel,
        grid=(DEPTH, nt),
        in_specs=[pl.BlockSpec((16, D), lambda l, j: (0, 0)),
                  pl.BlockSpec((None, D, tn), lambda l, j: (l, 0, j)),
                  pl.BlockSpec((None, 1, tn), lambda l, j: (l, 0, j))],
        out_specs=pl.BlockSpec((None, 16, tn), lambda l, j: (l, 0, j)),
        out_shape=jax.ShapeDtypeStruct((DEPTH, 16, mod_w.shape[2]), F32),
        compiler_params=_params(2),
        name="adaln",
    )(cc, mod_w, mod_b.reshape(DEPTH, 1, -1))


def _evproj_kernel(x_ref, mod_ref, w_ref, ca_ref, sa_ref, cb_ref, sb_ref,
                   aq_ref, ak_ref, av_ref, bq_ref, bk_ref, bv_ref, bg_ref, h_ref):
    rows_all = x_ref.shape[0]
    for j in range(rows_all // TM):
        m = mod_ref[jnp.where(pl.program_id(1) == 0, 0, 1) if j == 0 else 1]
        rows = slice(j * TM, (j + 1) * TM)
        h_ref[rows, :] = _bf(x_ref[rows, :] * (1.0 + m[1:2, :]) + m[0:1, :])
    h = h_ref[...]
    ca, sa, cb, sb = ca_ref[...], sa_ref[...], cb_ref[...], sb_ref[...]
    lane = lax.broadcasted_iota(jnp.int32, (rows_all, 128), 1)
    first_half = (lane & (DA_QK // 2)) == 0

    def seg(j):
        return _dot(h, w_ref[:, j * EV_SEG:(j + 1) * EV_SEG])

    def rope_a(p, out_ref, scale):
        for j in range(EV_SEG // 128):
            blk = p[:, j * 128:(j + 1) * 128]
            sw = jnp.where(first_half, pltpu.roll(blk, 96, 1), pltpu.roll(blk, 32, 1))
            out_ref[:, j * 128:(j + 1) * 128] = _bf((blk * ca + sw * sa) * scale)

    def rope_b(p, out_ref):
        for j in range(EV_SEG // 128):
            blk = p[:, j * 128:(j + 1) * 128]
            out_ref[:, j * 128:(j + 1) * 128] = _bf(blk * cb + pltpu.roll(blk, 64, 1) * sb)

    rope_a(seg(0), aq_ref, DA_QK ** -0.5 * math.log2(math.e))
    rope_a(seg(1), ak_ref, 1.0)
    av_ref[...] = _bf(seg(2))
    rope_b(seg(3), bq_ref)
    rope_b(seg(4) * (RT_QK ** -0.5), bk_ref)
    bv_ref[...] = _bf(seg(5))
    bg_ref[...] = seg(6)


def _even_project(xs, mod, w_in, tabs):
    B, N, _ = xs.shape
    tm = TM * max(s for s in (3, 2, 1) if N % (s * TM) == 0)
    nt = N // tm
    tile = lambda w: pl.BlockSpec((None, tm, w), lambda b, i: (b, i, 0))
    tab = pl.BlockSpec((tm, 128), lambda b, i: (i, 0))
    outs = [jax.ShapeDtypeStruct((B, N, EV_SEG), BF16)] * 6 + [jax.ShapeDtypeStruct((B, N, EV_SEG), F32)]
    return pl.pallas_call(
        _evproj_kernel,
        grid=(B, nt),
        in_specs=[tile(D), pl.BlockSpec((None, 2, 6, D), lambda b, i: (b, 0, 0, 0)),
                  _resident(w_in.shape), tab, tab, tab, tab],
        out_specs=[tile(EV_SEG)] * 7,
        out_shape=outs,
        scratch_shapes=[pltpu.VMEM((tm, D), BF16)],
        compiler_params=_params(2),
        name="even_project",
    )(xs, mod, w_in, *tabs)


def _attn_block(q, k_ref, v_ref, nk, lam, gn):
    lane = lax.broadcasted_iota(jnp.int32, q.shape, 1)
    zero = jnp.zeros_like(q)
    qq = jnp.concatenate([jnp.where(lane < DA_QK, q, zero), jnp.where(lane >= DA_QK, q, zero)], axis=0)
    s = _dot_nt(qq, k_ref[0:nk, :])
    yield
    e = jnp.exp2(s - jnp.max(s, -1, keepdims=True))
    inv = 1.0 / jnp.sum(e, -1, keepdims=True)
    eb = _bf(e)
    yield
    ev = _dot(eb, v_ref[0:nk, :])
    o = ev[0:Q_BLOCK] * inv[0:Q_BLOCK] - ev[Q_BLOCK:2 * Q_BLOCK] * (lam * inv[Q_BLOCK:2 * Q_BLOCK])
    return o * lax.rsqrt(jnp.mean(o * o, -1, keepdims=True) + LN_EPS) * gn


def _attn_kernel(lam_ref, q_ref, k_ref, v_ref, gn_ref, o_ref, *, n_ctx, n_all):
    i = pl.program_id(2)
    lam = lam_ref[0]
    gn = gn_ref[...]
    subs = [slice(j * Q_BLOCK, (j + 1) * Q_BLOCK) for j in range(ATT_ROWS // Q_BLOCK)]
    n_ctx_subs = n_ctx // Q_BLOCK

    def run(first_tile):
        nks = [n_ctx if (first_tile and j < n_ctx_subs) else n_all for j in range(len(subs))]
        outs = _lockstep([_attn_block(q_ref[rows, :], k_ref, v_ref, nk, lam, gn) for rows, nk in zip(subs, nks)])
        for rows, o in zip(subs, outs):
            o_ref[rows, :] = o.astype(o_ref.dtype)

    @pl.when(i == 0)
    def _():
        run(True)

    @pl.when(i > 0)
    def _():
        run(False)


def _diff_attention(aq, ak, av, lam, gn, n_ctx):
    B, N, _ = aq.shape
    assert N % ATT_ROWS == 0 and n_ctx <= ATT_ROWS and n_ctx % Q_BLOCK == 0
    kv = pl.BlockSpec((None, N, 128), lambda b, h, i: (b, 0, h))
    qo = pl.BlockSpec((None, ATT_ROWS, 128), lambda b, h, i: (b, i, h))
    return pl.pallas_call(
        functools.partial(_attn_kernel, n_ctx=n_ctx, n_all=N),
        grid=(B, DA_HEADS, N // ATT_ROWS),
        in_specs=[pl.BlockSpec(memory_space=pltpu.SMEM), qo, kv, kv,
                  pl.BlockSpec((1, 128), lambda b, h, i: (0, h))],
        out_specs=qo,
        out_shape=jax.ShapeDtypeStruct((B, N, EV_SEG), BF16),
        compiler_params=_params(3),
        name="diff_attention",
    )(lam, aq, ak, av, gn)


def _ret_kernel(lg_ref, q_ref, k_ref, v_ref, g_ref, o_ref, sf_ref, sb_ref, accf_ref, accb_ref,
                *, n_chunks, n_ctx_chunks):
    C = RT_CHUNK
    ii = lax.broadcasted_iota(jnp.int32, (C, C), 0)
    jj = lax.broadcasted_iota(jnp.int32, (C, C), 1)
    idx = lax.broadcasted_iota(jnp.int32, (C, 1), 0).astype(F32)
    consts = {}
    for h in range(RT_HEADS):
        for d in range(2):
            lg = lg_ref[d, h]
            diff = ((ii - jj) if d == 0 else (jj - ii)).astype(F32)
            dec = jnp.where(diff >= 0, jnp.exp(lg * jnp.maximum(diff, 0.0)), 0.0)
            if d == 0:
                zeta = jnp.exp(lg * (C - 1.0 - idx))
                xi = jnp.exp(lg * (idx + 1.0))
            else:
                zeta = jnp.exp(lg * idx)
                xi = jnp.exp(lg * (C - idx))
            consts[h, d] = (dec, zeta, xi, jnp.exp(lg * jnp.full((1, 1), float(C), F32)))
    sf_ref[...] = jnp.zeros_like(sf_ref)
    sb_ref[...] = jnp.zeros_like(sb_ref)

    def chunk(s, h, d, s_ref, acc_ref):
        dec, zeta, xi, g_chunk = consts[h, d]
        cols = slice(h * RT_QK, (h + 1) * RT_QK)
        if d == 0:
            c = s
        else:
            c = jnp.where(s < n_ctx_chunks, n_ctx_chunks - 1 - s, n_chunks + n_ctx_chunks - 1 - s)
        rows = pl.ds(pl.multiple_of(c * C, C), C)
        qc = q_ref[rows, cols]
        kc = k_ref[rows, cols]
        vc = v_ref[rows, cols]
        state = s_ref[:, cols]
        inner = _dot_nt(qc, kc)
        cross = _dot(qc, _bf(state))
        kz = _bf(kc.astype(F32) * zeta)
        s_ref[:, cols] = g_chunk * state + _dot_tn(kz, vc)
        yield
        acc_ref[rows, cols] = _dot(_bf(inner * dec), vc) + cross * xi

    def step(s, carry):
        _lockstep([chunk(s, h, d, s_ref, acc_ref) for h in range(RT_HEADS)
                   for d, (s_ref, acc_ref) in enumerate(((sf_ref, accf_ref), (sb_ref, accb_ref)))])
        return carry

    lax.fori_loop(0, n_chunks, step, 0)
    for h in range(RT_HEADS):
        cols = slice(h * RT_QK, (h + 1) * RT_QK)
        b = accf_ref[:, cols] + accb_ref[:, cols]
        g = g_ref[:, cols]
        b = b * lax.rsqrt(jnp.mean(b * b, -1, keepdims=True) + LN_EPS)
        o_ref[:, cols] = (b * (g * _sigmoid(g))).astype(o_ref.dtype)


def _retention(bq, bk, bv, bg, log_gamma, n_ctx):
    B, N, W = bq.shape
    blk = pl.BlockSpec((None, N, W), lambda b: (b, 0, 0))
    return pl.pallas_call(
        functools.partial(_ret_kernel, n_chunks=N // RT_CHUNK, n_ctx_chunks=n_ctx // RT_CHUNK),
        grid=(B,),
        in_specs=[pl.BlockSpec(memory_space=pltpu.SMEM), blk, blk, blk, blk],
        out_specs=blk,
        out_shape=jax.ShapeDtypeStruct((B, N, W), BF16),
        scratch_shapes=[pltpu.VMEM((RT_QK, W), F32), pltpu.VMEM((RT_QK, W), F32),
                        pltpu.VMEM((N, W), F32), pltpu.VMEM((N, W), F32)],
        compiler_params=_params(1),
        name="retention",
    )(log_gamma, bq, bk, bv, bg)


def _post_kernel(*refs, mix_widths, n_sub, has_ctx):
    n_mix = len(mix_widths)
    x_ref, mod_ref = refs[0], refs[1]
    mix_refs = refs[2:2 + n_mix]
    wo_ref, w1_ref, w3_ref, w2_ref, ln_ref, o_ref, x1_ref, hm_ref = refs[2 + n_mix:]
    i = pl.program_id(1)
    o = None
    off = 0
    for m_ref, w in zip(mix_refs, mix_widths):
        part = _dot(m_ref[...], wo_ref[off:off + w, :])
        o = part if o is None else o + part
        off += w
    subs = [slice(j * TM, (j + 1) * TM) for j in range(n_sub)]
    mods = []
    for j, rows in enumerate(subs):
        seg = jnp.where(i == 0, 0, 1) if (has_ctx and j == 0) else 1
        m = mod_ref[seg]
        mods.append(m)
        x1 = _ln(ALPHA * x_ref[rows, :] + m[2:3, :] * o[rows, :], ln_ref[0:1, :], ln_ref[1:2, :])
        x1_ref[rows, :] = x1
        hm_ref[rows, :] = _bf(x1 * (1.0 + m[4:5, :]) + m[3:4, :])
    hm = hm_ref[...]
    f = None
    off = 0
    for w in FF_CHUNKS:
        cols = slice(off, off + w)
        u = _dot(hm, w1_ref[:, cols])
        t = _dot(hm, w3_ref[:, cols])
        part = _dot(_bf(u * _sigmoid(u) * t), w2_ref[cols, :])
        f = part if f is None else f + part
        off += w
    for m, rows in zip(mods, subs):
        o_ref[rows, :] = _ln(ALPHA * x1_ref[rows, :] + m[5:6, :] * f[rows, :], ln_ref[2:3, :], ln_ref[3:4, :])


def _post_mixer(xs, mod, mixes, wo, w1, w3, w2, ln, latent_only, n_ctx):
    B, N, _ = xs.shape
    rows_out = N - n_ctx if latent_only else N
    n_sub = 1 if latent_only else max(s for s in (3, 2, 1) if N % (s * TM) == 0)
    tm = n_sub * TM
    t0 = n_ctx // tm if latent_only else 0
    nt = rows_out // tm
    in_tile = lambda w, off=t0: pl.BlockSpec((None, tm, w), lambda b, i: (b, i + off, 0))
    widths = tuple(m.shape[-1] for m in mixes)
    mix_off = [t0 if m.shape[1] == N else 0 for m in mixes]
    return pl.pallas_call(
        functools.partial(_post_kernel, mix_widths=widths, n_sub=n_sub, has_ctx=not latent_only),
        grid=(B, nt),
        in_specs=[in_tile(D), pl.BlockSpec((None, 2, 6, D), lambda b, i: (b, 0, 0, 0))]
                 + [in_tile(w, off) for w, off in zip(widths, mix_off)]
                 + [_resident(wo.shape), _resident(w1.shape), _resident(w3.shape), _resident(w2.shape),
                    _resident(ln.shape)],
        out_specs=pl.BlockSpec((None, tm, D), lambda b, i: (b, i, 0)),
        out_shape=jax.ShapeDtypeStruct((B, rows_out, D), F32),
        scratch_shapes=[pltpu.VMEM((tm, D), F32), pltpu.VMEM((tm, D), BF16)],
        compiler_params=_params(2),
        name="post_mixer",
    )(xs, mod, *mixes, wo, w1, w3, w2, ln)


PV_MU, PV_W0, PV_A0, PV_KK, PV_KA, PV_V0 = 0, 6, 8, 10, 11, 12


def _head_sum(x, e_ref):
    rows = x.shape[0]
    hi, lo = _split2(x)
    cols = []
    for j in range(x.shape[1] // HEADS4_W):
        sl = slice(j * HEADS4_W, (j + 1) * HEADS4_W)
        both = _dot(jnp.concatenate([hi[:, sl], lo[:, sl]], axis=0), e_ref[...])
        cols.append(both[0:rows] + both[rows:2 * rows])
    return jnp.concatenate(cols, axis=1)


def _rwproj_kernel(*refs, has_vres, n_tiles):
    if has_vres:
        (xp_ref, xc_ref, xn_ref, mod_ref, vf_ref, pv_ref, wr_ref, wk_ref, wv_ref, w1_ref, w2_ref,
         a1_ref, a2_ref, g1_ref, g2_ref, e_ref, v1_ref, v2_ref,
         r_ref, v_ref, g_ref, kk_ref, lw_ref, kd_ref, ag_ref, hbuf, hs) = refs
    else:
        (xp_ref, xc_ref, xn_ref, mod_ref, pv_ref, wr_ref, wk_ref, wv_ref, w1_ref, w2_ref,
         a1_ref, a2_ref, g1_ref, g2_ref, e_ref,
         r_ref, v_ref, g_ref, kk_ref, lw_ref, kd_ref, ag_ref, hbuf, hs) = refs
    i = pl.program_id(1)
    one_scale = 1.0 + mod_ref[1:2, :]
    shift = mod_ref[0:1, :]
    hbuf[0:HALO, :] = xp_ref[...] * one_scale + shift
    hbuf[HALO:HALO + TM, :] = xc_ref[...] * one_scale + shift
    hbuf[HALO + TM:, :] = xn_ref[...] * one_scale + shift
    row = lax.broadcasted_iota(jnp.int32, (TM, 1), 0)
    q = D // 4

    @pl.when(i == 0)
    def _():
        hs[:, 0:2 * q] = jnp.where(row == 0, 0.0, hbuf[HALO - 1:HALO - 1 + TM, 0:2 * q])
        hs[:, 2 * q:] = jnp.where(row == TM - 1, 0.0, hbuf[HALO + 1:HALO + 1 + TM, 2 * q:])

    @pl.when(i > 0)
    def _():
        col = row & (GRID_W - 1)
        hs[:, 0:q] = jnp.where(col == 0, 0.0, hbuf[HALO - 1:HALO - 1 + TM, 0:q])
        hs[:, q:2 * q] = jnp.where(col == GRID_W - 1, 0.0, hbuf[HALO + 1:HALO + 1 + TM, q:2 * q])
        top = jnp.logical_and(i == 1, row < GRID_W)
        hs[:, 2 * q:3 * q] = jnp.where(top, 0.0, hbuf[0:TM, 2 * q:3 * q])
        bottom = jnp.logical_and(i == n_tiles - 1, row >= TM - GRID_W)
        hs[:, 3 * q:] = jnp.where(bottom, 0.0, hbuf[2 * HALO:2 * HALO + TM, 3 * q:])

    h = hbuf[HALO:HALO + TM, :]
    xx = hs[...] - h
    mix = lambda j: _bf(h + xx * pv_ref[PV_MU + j:PV_MU + j + 1, :])
    xr, xw, xk, xv, xa, xg = (mix(j) for j in range(6))
    r = _dot(xr, wr_ref[...])
    k = _dot(xk, wk_ref[...])
    v = _dot(xv, wv_ref[...])
    if has_vres:
        gate = _sigmoid(pv_ref[PV_V0:PV_V0 + 1, :] + _dot(_bf(_dot(xv, v1_ref[...])), v2_ref[...]))
        v = v + (vf_ref[...].astype(F32) - v) * gate
    g = _dot(_bf(_sigmoid(_dot(xg, g1_ref[...]))), g2_ref[...])
    kx = k * pv_ref[PV_KK:PV_KK + 1, :]
    kkn = kx * lax.rsqrt(jnp.maximum(_head_sum(kx * kx, e_ref), 1e-24))
    r_ref[...] = _bf(r)
    v_ref[...] = _bf(v)
    g_ref[...] = _bf(g)
    kk_ref[...] = _bf(kkn)
    tw = jnp.tanh(_dot(xw, w1_ref[...]))
    ta = _dot(xa, a1_ref[...])
    for d in range(2):
        cols = slice(d * LORA_PAD, (d + 1) * LORA_PAD)
        w_pre = pv_ref[PV_W0 + d:PV_W0 + d + 1, :] + _dot(_bf(tw[:, cols]), w2_ref[d])
        lw_ref[d] = -math.exp(-0.5) * _sigmoid(w_pre)
        ag = _sigmoid(pv_ref[PV_A0 + d:PV_A0 + d + 1, :] + _dot(_bf(ta[:, cols]), a2_ref[d]))
        ag_ref[d] = _bf(ag)
        kd_ref[d] = _bf(k * (1.0 + (ag - 1.0) * pv_ref[PV_KA:PV_KA + 1, :]))


def _rwkv_project(xs, mod, vf, p):
    B, N, _ = xs.shape
    nt = N // TM
    hp = TM // HALO
    n_halo = N // HALO
    tile = pl.BlockSpec((None, TM, D), lambda b, i: (b, i, 0))
    tile2 = pl.BlockSpec((2, None, TM, D), lambda b, i: (0, b, i, 0))
    in_specs = [pl.BlockSpec((None, HALO, D), lambda b, i: (b, jnp.maximum(i * hp - 1, 0), 0)),
                tile,
                pl.BlockSpec((None, HALO, D), lambda b, i: (b, jnp.minimum((i + 1) * hp, n_halo - 1), 0)),
                pl.BlockSpec((None, None, 6, D), lambda b, i: (b, jnp.minimum(i, 1), 0, 0))]
    args = [xs, xs, xs, mod]
    has_vres = vf is not None
    if has_vres:
        in_specs.append(tile)
        args.append(vf)
    names = ["pv", "wr", "wk", "wv", "w1", "w2", "a1", "a2", "g1", "g2", "e"] + (["v1", "v2"] if has_vres else [])
    for nm in names:
        in_specs.append(_resident(p[nm].shape))
        args.append(p[nm])
    one = jax.ShapeDtypeStruct((B, N, D), BF16)
    two = jax.ShapeDtypeStruct((2, B, N, D), BF16)
    return pl.pallas_call(
        functools.partial(_rwproj_kernel, has_vres=has_vres, n_tiles=nt),
        grid=(B, nt),
        in_specs=in_specs,
        out_specs=[tile, tile, tile, tile, tile2, tile2, tile2],
        out_shape=[one, one, one, one, jax.ShapeDtypeStruct((2, B, N, D), F32), two, two],
        scratch_shapes=[pltpu.VMEM((TM + 2 * HALO, D), F32), pltpu.VMEM((TM, D), F32)],
        compiler_params=_params(2),
        name="rwkv_project",
    )(*args)


def _bd(x, head):
    zero = jnp.zeros_like(x)
    return jnp.concatenate([jnp.where(head == hh, x, zero) for hh in range(PAIR_W // RW_HEAD)], axis=0)


def _diag_blocks(full, head):
    return jnp.where(head == 0, full[0:RW_HEAD], full[RW_HEAD:2 * RW_HEAD])


def _scan_chunk(r_ref, v_ref, kk_ref, lw_ref, kd_ref, ag_ref, rows, cols, reverse):
    L, W = SCAN_L, PAIR_W
    row = lax.broadcasted_iota(jnp.int32, (L, W), 0)
    lane = lax.broadcasted_iota(jnp.int32, (L, W), 1)
    pos = lane & (RW_HEAD - 1)
    head = lane >> 6
    strict = (pos > row) if reverse else (pos < row)
    incl = (pos >= row) if reverse else (pos <= row)
    ident = pos == row
    bd = lambda x: _bd(x, head)

    load = lambda ref: ref[rows, cols].astype(F32)
    lw = load(lw_ref)
    cum = lw
    step = 1
    while step < L:
        if reverse:
            cum = cum + jnp.where(row < L - step, pltpu.roll(cum, L - step, 0), 0.0)
        else:
            cum = cum + jnp.where(row >= step, pltpu.roll(cum, step, 0), 0.0)
        step *= 2
    gam = jnp.exp(cum)
    gam_inv = jnp.exp(-cum)
    kk = load(kk_ref)
    a_t = _bf(-kk * jnp.exp(cum - lw))
    r_t = load(r_ref) * gam
    b_t = kk * load(ag_ref) * gam_inv
    k_t = load(kd_ref) * gam_inv
    vb = v_ref[rows, cols]
    g_last = gam[0:1, :] if reverse else gam[L - 1:L, :]
    bd_v = bd(vb)
    g4 = _dot_nt(jnp.concatenate([a_t, _bf(r_t)], axis=0),
                 jnp.concatenate([bd(_bf(b_t)), bd(_bf(k_t))], axis=0))
    yield
    n_m = jnp.where(strict, g4[0:L, 0:W], 0.0)
    p_m = jnp.where(strict, g4[0:L, W:2 * W], 0.0)
    rb = _bf(jnp.where(incl, g4[L:2 * L, 0:W], 0.0))
    rk = _bf(jnp.where(incl, g4[L:2 * L, W:2 * W], 0.0))
    pv = _dot(_bf(p_m), bd_v)
    t_m = jnp.where(ident, 1.0, 0.0)
    m = 1
    while m < L:
        later, earlier = (pos, row) if reverse else (row, pos)
        coupling = jnp.logical_and(jnp.logical_and((later & m) != 0, (earlier & m) == 0),
                                   (row // (2 * m)) == (pos // (2 * m)))
        n_off = jnp.where(coupling, n_m, 0.0)
        if m == 1:
            t_m = t_m + n_off
        else:
            t_b = _bf(t_m)
            half = _bf(_dot(t_b, bd(_bf(n_off))))
            yield
            t_m = t_m + _dot(half, bd(t_b))
            yield
        m *= 2
    t_m = _bf(t_m)
    tw = _dot(t_m, jnp.concatenate([bd(a_t), bd(_bf(pv))], axis=1))
    yield
    a_hat = _bf(tw[:, 0:W])
    w_b = _bf(tw[:, W:2 * W])
    q_hat = _bf(r_t + _dot(rb, bd(a_hat)))
    y_c = _dot(jnp.concatenate([rb, rk], axis=1), jnp.concatenate([bd(w_b), bd_v], axis=0))
    b_g = _bf(b_t * g_last)
    k_g = _bf(k_t * g_last)
    m_m = jnp.where(ident, g_last, 0.0) + _diag_blocks(_dot_tn(b_g, a_hat), head)
    c_m = _diag_blocks(_dot_tn(jnp.concatenate([b_g, k_g], axis=0), jnp.concatenate([w_b, vb], axis=0)), head)
    return q_hat, y_c, _bf(m_m), c_m


def _lockstep(gens):
    results = [None] * len(gens)
    active = list(enumerate(gens))
    while active:
        still = []
        for idx, g in active:
            try:
                next(g)
                still.append((idx, g))
            except StopIteration as stop:
                results[idx] = stop.value
        active = still
    return results


def _scan_kernel(rf_ref, vf_ref, kkf_ref, lwf_ref, kdf_ref, agf_ref,
                 rb_ref, vb_ref, kkb_ref, lwb_ref, kdb_ref, agb_ref,
                 yf_ref, yb_ref, hf_ref, hb_ref):
    @pl.when(pl.program_id(2) == 0)
    def _():
        hf_ref[...] = jnp.zeros_like(hf_ref)
        hb_ref[...] = jnp.zeros_like(hb_ref)

    L = SCAN_L
    n_chunks = TM // L
    fwd = (rf_ref, vf_ref, kkf_ref, lwf_ref, kdf_ref, agf_ref)
    bwd = (rb_ref, vb_ref, kkb_ref, lwb_ref, kdb_ref, agb_ref)
    f_rows = [slice(c * L, (c + 1) * L) for c in range(n_chunks)]
    b_rows = f_rows[::-1]
    pairs = [slice(p * PAIR_W, (p + 1) * PAIR_W) for p in range(SCAN_W // PAIR_W)]
    chains = ([(fwd, hf_ref, yf_ref, f_rows, cols, False) for cols in pairs]
              + [(bwd, hb_ref, yb_ref, b_rows, cols, True) for cols in pairs])
    head = lax.broadcasted_iota(jnp.int32, (L, PAIR_W), 1) >> 6
    for first in range(0, n_chunks, SCAN_ROUND):
        steps = range(first, first + SCAN_ROUND)
        jobs = [(chain, i) for i in steps for chain in chains]
        pre = _lockstep([_scan_chunk(*refs, rows[i], cols, rev) for (refs, _, _, rows, cols, rev), i in jobs])
        for ((_, h_ref, y_ref, rows, cols, _), i), (q_hat, y_c, m_b, c_m) in zip(jobs, pre):
            both = _dot(jnp.concatenate([q_hat, m_b], axis=0), _bd(_bf(h_ref[:, cols]), head))
            y_ref[rows[i], cols] = (both[0:L] + y_c).astype(y_ref.dtype)
            h_ref[:, cols] = both[L:2 * L] + c_m


def _rwkv_scan(r, v, kk, lw, kd, ag, n_ctx):
    B, N, _ = r.shape
    nt = N // TM
    nc = n_ctx // TM

    def back(s):
        return jnp.where(s < nc, nc - 1 - s, nt + nc - 1 - s)

    f_one = pl.BlockSpec((None, TM, SCAN_W), lambda b, g, s: (b, s, g))
    b_one = pl.BlockSpec((None, TM, SCAN_W), lambda b, g, s: (b, back(s), g))
    f_two = pl.BlockSpec((None, None, TM, SCAN_W), lambda b, g, s: (0, b, s, g))
    b_two = pl.BlockSpec((None, None, TM, SCAN_W), lambda b, g, s: (1, b, back(s), g))
    out = jax.ShapeDtypeStruct((B, N, D), BF16)
    return pl.pallas_call(
        _scan_kernel,
        grid=(B, D // SCAN_W, nt),
        in_specs=[f_one, f_one, f_one, f_two, f_two, f_two, b_one, b_one, b_one, b_two, b_two, b_two],
        out_specs=[f_one, b_one],
        out_shape=[out, out],
        scratch_shapes=[pltpu.VMEM((RW_HEAD, SCAN_W), F32), pltpu.VMEM((RW_HEAD, SCAN_W), F32)],
        compiler_params=_params(3),
        name="rwkv_scan",
    )(r, v, kk, lw, kd, ag, r, v, kk, lw, kd, ag)


def _rwread_kernel(yf_ref, yb_ref, r_ref, v_ref, g_ref, kd0_ref, kd1_ref, vec_ref, e_ref, o_ref):
    inv = 1.0 / RW_HEAD
    y = yf_ref[...].astype(F32) + yb_ref[...].astype(F32)
    mu = _head_sum(y, e_ref) * inv
    yc = y - mu
    var = _head_sum(yc * yc, e_ref) * inv
    yn = yc * lax.rsqrt(var + RW_GN_EPS) * vec_ref[1:2, :] + vec_ref[2:3, :]
    f32 = lambda ref: ref[...].astype(F32)
    bonus = _head_sum(f32(r_ref) * (f32(kd0_ref) + f32(kd1_ref)) * vec_ref[0:1, :], e_ref) * f32(v_ref)
    o_ref[...] = _bf((yn + bonus) * f32(g_ref))


def _rwkv_readout(yf, yb, r, v, g, kd, vec, e, latent_only, n_ctx):
    B, N, _ = r.shape
    t0 = n_ctx // TM if latent_only else 0
    nt = N // TM - t0
    one = pl.BlockSpec((None, TM, D), lambda b, i: (b, i + t0, 0))
    pick = lambda d: pl.BlockSpec((None, None, TM, D), lambda b, i: (d, b, i + t0, 0))
    return pl.pallas_call(
        _rwread_kernel,
        grid=(B, nt),
        in_specs=[one, one, one, one, one, pick(0), pick(1), _resident(vec.shape), _resident(e.shape)],
        out_specs=pl.BlockSpec((None, TM, D), lambda b, i: (b, i, 0)),
        out_shape=jax.ShapeDtypeStruct((B, nt * TM, D), BF16),
        compiler_params=_params(2),
        name="rwkv_readout",
    )(yf, yb, r, v, g, kd, kd, vec, e)


def _rope_tables(n_lat, n_ctx):
    t = jnp.arange(n_lat)
    rowp = (t // GRID_W).astype(F32)
    colp = (t % GRID_W).astype(F32)

    def table(dim):
        n_freq = dim // 4
        inv = ROPE_BASE ** (-jnp.arange(n_freq, dtype=F32) / n_freq)
        ang = jnp.concatenate([rowp[:, None] * inv, colp[:, None] * inv], -1)
        cos, sin = jnp.cos(ang), jnp.sin(ang)
        reps = 128 // dim
        cos_t = jnp.tile(jnp.concatenate([cos, cos], -1), (1, reps))
        sin_t = jnp.tile(jnp.concatenate([-sin, sin], -1), (1, reps))
        cos_t = jnp.concatenate([jnp.ones((n_ctx, 128), F32), cos_t], 0)
        sin_t = jnp.concatenate([jnp.zeros((n_ctx, 128), F32), sin_t], 0)
        return cos_t, sin_t

    ca, sa = table(DA_QK)
    cb, sb = table(RT_QK)
    return ca, sa, cb, sb


def _pad_cols(w, n):
    return jnp.pad(w, ((0, 0), (0, n - w.shape[1])))


def _pad_rows(w, n):
    return jnp.pad(w, ((0, n - w.shape[0]), (0, 0)))


def kernel(x, c, ctx, c_ctx, mod_w, mod_b, ln1_g, ln1_b, ln2_g, ln2_b, ffn_w1, ffn_w3, ffn_w2, ev_w_in, ev_w_out, da_lam_q1, da_lam_k1, da_lam_q2, da_lam_k2, da_gn_g, rt_decay_logit, rw_mu, rw_wr, rw_wk, rw_wv, rw_wo, rw_w0, rw_w1, rw_w2, rw_a0, rw_a1, rw_a2, rw_v0, rw_v1, rw_v2, rw_g1, rw_g2, rw_kk, rw_ka, rw_rk, rw_lnx_g, rw_lnx_b):
    B, T, _ = x.shape
    n_ctx = ctx.shape[1]
    assert n_ctx == TM and T % TM == 0 and x.shape[2] == D and B <= 15
    xs = jnp.concatenate([ctx, x], axis=1)

    cc = jnp.zeros((16, D), F32).at[:B].set(c).at[B].set(c_ctx)
    mod_all = _adaln(cc, mod_w, mod_b)
    m_lat = mod_all[:, :B].reshape(DEPTH, B, 1, 6, D)
    m_ctx = jnp.broadcast_to(mod_all[:, B].reshape(DEPTH, 1, 1, 6, D), (DEPTH, B, 1, 6, D))
    mod_tab = jnp.concatenate([m_ctx, m_lat], axis=2)

    tabs = _rope_tables(T, n_ctx)
    lane_head = np.arange(HEADS4_W) // RW_HEAD
    head_ones = jnp.asarray(lane_head[:, None] == lane_head[None, :], BF16)

    vf = None
    for l in range(DEPTH):
        last = l == DEPTH - 1
        mod = mod_tab[l]
        ln = jnp.stack([ln1_g[l], ln1_b[l], ln2_g[l], ln2_b[l]])
        if l % 2 == 0:
            e = l // 2
            lam_init = 0.8 - 0.6 * math.exp(-0.3 * l)
            lam = (jnp.exp(jnp.sum(da_lam_q1[e] * da_lam_k1[e])) - jnp.exp(jnp.sum(da_lam_q2[e] * da_lam_k2[e]))
                   + lam_init).reshape(1).astype(F32)
            gn = (da_gn_g[e] * (1.0 - lam_init)).reshape(1, -1)
            log_gamma = jnp.log(jax.nn.sigmoid(rt_decay_logit[e].astype(F32)))
            aq, ak, av, bq, bk, bv, bg = _even_project(xs, mod, _bf(ev_w_in[e]), tabs)
            a_mix = _diff_attention(aq, ak, av, lam, gn, n_ctx)
            b_mix = _retention(bq, bk, bv, bg, log_gamma, n_ctx)
            mixes = (a_mix, b_mix)
            wo = _bf(ev_w_out[e])
        else:
            j = l // 2
            has_vres = j > 0
            pvec = jnp.zeros((16, D), F32)
            pvec = pvec.at[PV_MU:PV_MU + 6].set(rw_mu[j]).at[PV_W0:PV_W0 + 2].set(rw_w0[j])
            pvec = pvec.at[PV_A0:PV_A0 + 2].set(rw_a0[j]).at[PV_KK].set(rw_kk[j]).at[PV_KA].set(rw_ka[j])
            p = {
                "wr": _bf(rw_wr[j]), "wk": _bf(rw_wk[j]), "wv": _bf(rw_wv[j]),
                "w1": _bf(jnp.concatenate([_pad_cols(rw_w1[j, d], LORA_PAD) for d in range(2)], 1)),
                "w2": _bf(jnp.stack([_pad_rows(rw_w2[j, d], LORA_PAD) for d in range(2)])),
                "a1": _bf(jnp.concatenate([_pad_cols(rw_a1[j, d], LORA_PAD) for d in range(2)], 1)),
                "a2": _bf(jnp.stack([_pad_rows(rw_a2[j, d], LORA_PAD) for d in range(2)])),
                "g1": _bf(_pad_cols(rw_g1[j], GATE_PAD)), "g2": _bf(_pad_rows(rw_g2[j], GATE_PAD)),
                "e": head_ones,
            }
            if has_vres:
                pvec = pvec.at[PV_V0].set(rw_v0[j - 1])
                p["v1"] = _bf(_pad_cols(rw_v1[j - 1], LORA_PAD))
                p["v2"] = _bf(_pad_rows(rw_v2[j - 1], LORA_PAD))
            p["pv"] = pvec
            r, v, g, kk, lw, kd, ag = _rwkv_project(xs, mod, vf if has_vres else None, p)
            if not has_vres:
                vf = v
            yf, yb = _rwkv_scan(r, v, kk, lw, kd, ag, n_ctx)
            vec = jnp.zeros((8, D), F32).at[0].set(rw_rk[j].reshape(-1)).at[1].set(rw_lnx_g[j]).at[2].set(rw_lnx_b[j])
            mixes = (_rwkv_readout(yf, yb, r, v, g, kd, vec, head_ones, last, n_ctx),)
            wo = _bf(rw_wo[j])
        xs = _post_mixer(xs, mod, mixes, wo, _bf(ffn_w1[l]), _bf(ffn_w3[l]), _bf(ffn_w2[l]), ln, last, n_ctx)
    return xs
```

```python
import functools
import math

import jax
import jax.numpy as jnp
import numpy as np
from jax import lax
from jax.experimental import pallas as pl
from jax.experimental.pallas import tpu as pltpu

F32 = jnp.float32
BF16 = jnp.bfloat16

D = 1024
DEPTH = 4
GRID_W = 64
ALPHA = (2.0 * DEPTH) ** 0.25
LN_EPS = 1e-6
ROPE_BASE = 10000.0
D_FF = 2816
FF_CHUNKS = (768, 768, 768, 512)
DA_QK = 64
DA_HEADS = 4
RT_QK = 128
RT_HEADS = 4
RT_CHUNK = 128
Q_BLOCK = 128
ATT_ROWS = 768
EV_SEG = 512
RW_HEAD = 64
RW_GN_EPS = 64e-5
LORA_PAD = 128
GATE_PAD = 256
TM = 256
HALO = GRID_W
SCAN_L = 64
HEADS4_W = 4 * RW_HEAD
SCAN_W = 16 * RW_HEAD
PAIR_W = 2 * RW_HEAD
SCAN_ROUND = 4
VMEM_LIMIT = 56 * 1024 * 1024


def _dot(a, b):
    return jnp.dot(a, b, preferred_element_type=F32)


def _dot_nt(a, b):
    return lax.dot_general(a, b, (((1,), (1,)), ((), ())), preferred_element_type=F32)


def _dot_tn(a, b):
    return lax.dot_general(a, b, (((0,), (0,)), ((), ())), preferred_element_type=F32)


def _bf(x):
    return x.astype(BF16)


def _split2(x):
    hi = _bf(x)
    return hi, _bf(x - hi.astype(F32))


def _sigmoid(x):
    return 0.5 * jnp.tanh(0.5 * x) + 0.5


def _ln(x, g, b):
    mu = jnp.mean(x, -1, keepdims=True)
    xc = x - mu
    var = jnp.mean(xc * xc, -1, keepdims=True)
    return xc * lax.rsqrt(var + LN_EPS) * g + b


def _params(n_grid):
    return pltpu.CompilerParams(dimension_semantics=("arbitrary",) * n_grid,
                                vmem_limit_bytes=VMEM_LIMIT)


def _resident(shape):
    nd = len(shape)
    return pl.BlockSpec(shape, lambda *_: (0,) * nd, pipeline_mode=pl.Buffered(1))


def _mod_kernel(c_ref, w_ref, b_ref, o_ref):
    c = c_ref[...]
    act = _bf(c * _sigmoid(c))
    o_ref[...] = _dot(act, _bf(w_ref[...])) + b_ref[...]


def _adaln(cc, mod_w, mod_b):
    tn = 1536
    nt = mod_w.shape[2] // tn
    return pl.pallas_call(
        _mod_kernel,
        grid=(DEPTH, nt),
        in_specs=[pl.BlockSpec((16, D), lambda l, j: (0, 0)),
                  pl.BlockSpec((None, D, tn), lambda l, j: (l, 0, j)),
                  pl.BlockSpec((None, 1, tn), lambda l, j: (l, 0, j))],
        out_specs=pl.BlockSpec((None, 16, tn), lambda l, j: (l, 0, j)),
        out_shape=jax.ShapeDtypeStruct((DEPTH, 16, mod_w.shape[2]), F32),
        compiler_params=_params(2),
        name="adaln",
    )(cc, mod_w, mod_b.reshape(DEPTH, 1, -1))


def _evproj_kernel(x_ref, mod_ref, w_ref, ca_ref, sa_ref, cb_ref, sb_ref,
                   aq_ref, ak_ref, av_ref, bq_ref, bk_ref, bv_ref, bg_ref, h_ref):
    rows_all = x_ref.shape[0]
    for j in range(rows_all // TM):
        m = mod_ref[jnp.where(pl.program_id(1) == 0, 0, 1) if j == 0 else 1]
        rows = slice(j * TM, (j + 1) * TM)
        h_ref[rows, :] = _bf(x_ref[rows, :] * (1.0 + m[1:2, :]) + m[0:1, :])
    h = h_ref[...]
    ca, sa, cb, sb = ca_ref[...], sa_ref[...], cb_ref[...], sb_ref[...]
    lane = lax.broadcasted_iota(jnp.int32, (rows_all, 128), 1)
    first_half = (lane & (DA_QK // 2)) == 0

    def seg(j):
        return _dot(h, w_ref[:, j * EV_SEG:(j + 1) * EV_SEG])

    def rope_a(p, out_ref, scale):
        for j in range(EV_SEG // 128):
            blk = p[:, j * 128:(j + 1) * 128]
            sw = jnp.where(first_half, pltpu.roll(blk, 96, 1), pltpu.roll(blk, 32, 1))
            out_ref[:, j * 128:(j + 1) * 128] = _bf((blk * ca + sw * sa) * scale)

    def rope_b(p, out_ref):
        for j in range(EV_SEG // 128):
            blk = p[:, j * 128:(j + 1) * 128]
            out_ref[:, j * 128:(j + 1) * 128] = _bf(blk * cb + pltpu.roll(blk, 64, 1) * sb)

    rope_a(seg(0), aq_ref, DA_QK ** -0.5 * math.log2(math.e))
    rope_a(seg(1), ak_ref, 1.0)
    av_ref[...] = _bf(seg(2))
    rope_b(seg(3), bq_ref)
    rope_b(seg(4) * (RT_QK ** -0.5), bk_ref)
    bv_ref[...] = _bf(seg(5))
    bg_ref[...] = seg(6)


def _even_project(xs, mod, w_in, tabs):
    B, N, _ = xs.shape
    tm = TM * max(s for s in (3, 2, 1) if N % (s * TM) == 0)
    nt = N // tm
    tile = lambda w: pl.BlockSpec((None, tm, w), lambda b, i: (b, i, 0))
    tab = pl.BlockSpec((tm, 128), lambda b, i: (i, 0))
    outs = [jax.ShapeDtypeStruct((B, N, EV_SEG), BF16)] * 6 + [jax.ShapeDtypeStruct((B, N, EV_SEG), F32)]
    return pl.pallas_call(
        _evproj_kernel,
        grid=(B, nt),
        in_specs=[tile(D), pl.BlockSpec((None, 2, 6, D), lambda b, i: (b, 0, 0, 0)),
                  _resident(w_in.shape), tab, tab, tab, tab],
        out_specs=[tile(EV_SEG)] * 7,
        out_shape=outs,
        scratch_shapes=[pltpu.VMEM((tm, D), BF16)],
        compiler_params=_params(2),
        name="even_project",
    )(xs, mod, w_in, *tabs)


def _attn_block(q, k_ref, v_ref, nk, lam, gn):
    lane = lax.broadcasted_iota(jnp.int32, q.shape, 1)
    zero = jnp.zeros_like(q)
    qq = jnp.concatenate([jnp.where(lane < DA_QK, q, zero), jnp.where(lane >= DA_QK, q, zero)], axis=0)
    s = _dot_nt(qq, k_ref[0:nk, :])
    yield
    e = jnp.exp2(s - jnp.max(s, -1, keepdims=True))
    inv = 1.0 / jnp.sum(e, -1, keepdims=True)
    eb = _bf(e)
    yield
    ev = _dot(eb, v_ref[0:nk, :])
    o = ev[0:Q_BLOCK] * inv[0:Q_BLOCK] - ev[Q_BLOCK:2 * Q_BLOCK] * (lam * inv[Q_BLOCK:2 * Q_BLOCK])
    return o * lax.rsqrt(jnp.mean(o * o, -1, keepdims=True) + LN_EPS) * gn


def _attn_kernel(lam_ref, q_ref, k_ref, v_ref, gn_ref, o_ref, *, n_ctx, n_all):
    i = pl.program_id(2)
    lam = lam_ref[0]
    gn = gn_ref[...]
    subs = [slice(j * Q_BLOCK, (j + 1) * Q_BLOCK) for j in range(ATT_ROWS // Q_BLOCK)]
    n_ctx_subs = n_ctx // Q_BLOCK

    def run(first_tile):
        nks = [n_ctx if (first_tile and j < n_ctx_subs) else n_all for j in range(len(subs))]
        outs = _lockstep([_attn_block(q_ref[rows, :], k_ref, v_ref, nk, lam, gn) for rows, nk in zip(subs, nks)])
        for rows, o in zip(subs, outs):
            o_ref[rows, :] = o.astype(o_ref.dtype)

    @pl.when(i == 0)
    def _():
        run(True)

    @pl.when(i > 0)
    def _():
        run(False)


def _diff_attention(aq, ak, av, lam, gn, n_ctx):
    B, N, _ = aq.shape
    assert N % ATT_ROWS == 0 and n_ctx <= ATT_ROWS and n_ctx % Q_BLOCK == 0
    kv = pl.BlockSpec((None, N, 128), lambda b, h, i: (b, 0, h))
    qo = pl.BlockSpec((None, ATT_ROWS, 128), lambda b, h, i: (b, i, h))
    return pl.pallas_call(
        functools.partial(_attn_kernel, n_ctx=n_ctx, n_all=N),
        grid=(B, DA_HEADS, N // ATT_ROWS),
        in_specs=[pl.BlockSpec(memory_space=pltpu.SMEM), qo, kv, kv,
                  pl.BlockSpec((1, 128), lambda b, h, i: (0, h))],
        out_specs=qo,
        out_shape=jax.ShapeDtypeStruct((B, N, EV_SEG), BF16),
        compiler_params=_params(3),
        name="diff_attention",
    )(lam, aq, ak, av, gn)


def _ret_kernel(lg_ref, q_ref, k_ref, v_ref, g_ref, o_ref, sf_ref, sb_ref, accf_ref, accb_ref,
                *, n_chunks, n_ctx_chunks):
    C = RT_CHUNK
    ii = lax.broadcasted_iota(jnp.int32, (C, C), 0)
    jj = lax.broadcasted_iota(jnp.int32, (C, C), 1)
    idx = lax.broadcasted_iota(jnp.int32, (C, 1), 0).astype(F32)
    consts = {}
    for h in range(RT_HEADS):
        for d in range(2):
            lg = lg_ref[d, h]
            diff = ((ii - jj) if d == 0 else (jj - ii)).astype(F32)
            dec = jnp.where(diff >= 0, jnp.exp(lg * jnp.maximum(diff, 0.0)), 0.0)
            if d == 0:
                zeta = jnp.exp(lg * (C - 1.0 - idx))
                xi = jnp.exp(lg * (idx + 1.0))
            else:
                zeta = jnp.exp(lg * idx)
                xi = jnp.exp(lg * (C - idx))
            consts[h, d] = (dec, zeta, xi, jnp.exp(lg * jnp.full((1, 1), float(C), F32)))
    sf_ref[...] = jnp.zeros_like(sf_ref)
    sb_ref[...] = jnp.zeros_like(sb_ref)

    def chunk(s, h, d, s_ref, acc_ref):
        dec, zeta, xi, g_chunk = consts[h, d]
        cols = slice(h * RT_QK, (h + 1) * RT_QK)
        if d == 0:
            c = s
        else:
            c = jnp.where(s < n_ctx_chunks, n_ctx_chunks - 1 - s, n_chunks + n_ctx_chunks - 1 - s)
        rows = pl.ds(pl.multiple_of(c * C, C), C)
        qc = q_ref[rows, cols]
        kc = k_ref[rows, cols]
        vc = v_ref[rows, cols]
        state = s_ref[:, cols]
        inner = _dot_nt(qc, kc)
        cross = _dot(qc, _bf(state))
        kz = _bf(kc.astype(F32) * zeta)
        s_ref[:, cols] = g_chunk * state + _dot_tn(kz, vc)
        yield
        acc_ref[rows, cols] = _dot(_bf(inner * dec), vc) + cross * xi

    def step(s, carry):
        _lockstep([chunk(s, h, d, s_ref, acc_ref) for h in range(RT_HEADS)
                   for d, (s_ref, acc_ref) in enumerate(((sf_ref, accf_ref), (sb_ref, accb_ref)))])
        return carry

    lax.fori_loop(0, n_chunks, step, 0)
    for h in range(RT_HEADS):
        cols = slice(h * RT_QK, (h + 1) * RT_QK)
        b = accf_ref[:, cols] + accb_ref[:, cols]
        g = g_ref[:, cols]
        b = b * lax.rsqrt(jnp.mean(b * b, -1, keepdims=True) + LN_EPS)
        o_ref[:, cols] = (b * (g * _sigmoid(g))).astype(o_ref.dtype)


def _retention(bq, bk, bv, bg, log_gamma, n_ctx):
    B, N, W = bq.shape
    blk = pl.BlockSpec((None, N, W), lambda b: (b, 0, 0))
    return pl.pallas_call(
        functools.partial(_ret_kernel, n_chunks=N // RT_CHUNK, n_ctx_chunks=n_ctx // RT_CHUNK),
        grid=(B,),
        in_specs=[pl.BlockSpec(memory_space=pltpu.SMEM), blk, blk, blk, blk],
        out_specs=blk,
        out_shape=jax.ShapeDtypeStruct((B, N, W), BF16),
        scratch_shapes=[pltpu.VMEM((RT_QK, W), F32), pltpu.VMEM((RT_QK, W), F32),
                        pltpu.VMEM((N, W), F32), pltpu.VMEM((N, W), F32)],
        compiler_params=_params(1),
        name="retention",
    )(log_gamma, bq, bk, bv, bg)


def _post_kernel(*refs, mix_widths, n_sub, has_ctx):
    n_mix = len(mix_widths)
    x_ref, mod_ref = refs[0], refs[1]
    mix_refs = refs[2:2 + n_mix]
    wo_ref, w1_ref, w3_ref, w2_ref, ln_ref, o_ref, x1_ref, hm_ref = refs[2 + n_mix:]
    i = pl.program_id(1)
    o = None
    off = 0
    for m_ref, w in zip(mix_refs, mix_widths):
        part = _dot(m_ref[...], wo_ref[off:off + w, :])
        o = part if o is None else o + part
        off += w
    subs = [slice(j * TM, (j + 1) * TM) for j in range(n_sub)]
    mods = []
    for j, rows in enumerate(subs):
        seg = jnp.where(i == 0, 0, 1) if (has_ctx and j == 0) else 1
        m = mod_ref[seg]
        mods.append(m)
        x1 = _ln(ALPHA * x_ref[rows, :] + m[2:3, :] * o[rows, :], ln_ref[0:1, :], ln_ref[1:2, :])
        x1_ref[rows, :] = x1
        hm_ref[rows, :] = _bf(x1 * (1.0 + m[4:5, :]) + m[3:4, :])
    hm = hm_ref[...]
    f = None
    off = 0
    for w in FF_CHUNKS:
        cols = slice(off, off + w)
        u = _dot(hm, w1_ref[:, cols])
        t = _dot(hm, w3_ref[:, cols])
        part = _dot(_bf(u * _sigmoid(u) * t), w2_ref[cols, :])
        f = part if f is None else f + part
        off += w
    for m, rows in zip(mods, subs):
        o_ref[rows, :] = _ln(ALPHA * x1_ref[rows, :] + m[5:6, :] * f[rows, :], ln_ref[2:3, :], ln_ref[3:4, :])


def _post_mixer(xs, mod, mixes, wo, w1, w3, w2, ln, latent_only, n_ctx):
    B, N, _ = xs.shape
    rows_out = N - n_ctx if latent_only else N
    n_sub = 1 if latent_only else max(s for s in (3, 2, 1) if N % (s * TM) == 0)
    tm = n_sub * TM
    t0 = n_ctx // tm if latent_only else 0
    nt = rows_out // tm
    in_tile = lambda w, off=t0: pl.BlockSpec((None, tm, w), lambda b, i: (b, i + off, 0))
    widths = tuple(m.shape[-1] for m in mixes)
    mix_off = [t0 if m.shape[1] == N else 0 for m in mixes]
    return pl.pallas_call(
        functools.partial(_post_kernel, mix_widths=widths, n_sub=n_sub, has_ctx=not latent_only),
        grid=(B, nt),
        in_specs=[in_tile(D), pl.BlockSpec((None, 2, 6, D), lambda b, i: (b, 0, 0, 0))]
                 + [in_tile(w, off) for w, off in zip(widths, mix_off)]
                 + [_resident(wo.shape), _resident(w1.shape), _resident(w3.shape), _resident(w2.shape),
                    _resident(ln.shape)],
        out_specs=pl.BlockSpec((None, tm, D), lambda b, i: (b, i, 0)),
        out_shape=jax.ShapeDtypeStruct((B, rows_out, D), F32),
        scratch_shapes=[pltpu.VMEM((tm, D), F32), pltpu.VMEM((tm, D), BF16)],
        compiler_params=_params(2),
        name="post_mixer",
    )(xs, mod, *mixes, wo, w1, w3, w2, ln)


PV_MU, PV_W0, PV_A0, PV_KK, PV_KA0, PV_V0, PV_KA1 = 0, 6, 8, 10, 11, 12, 13


def _head_sum(x, e_ref):
    rows = x.shape[0]
    hi, lo = _split2(x)
    cols = []
    for j in range(x.shape[1] // HEADS4_W):
        sl = slice(j * HEADS4_W, (j + 1) * HEADS4_W)
        both = _dot(jnp.concatenate([hi[:, sl], lo[:, sl]], axis=0), e_ref[...])
        cols.append(both[0:rows] + both[rows:2 * rows])
    return jnp.concatenate(cols, axis=1)


def _rwproj_kernel(*refs, has_vres, n_tiles):
    if has_vres:
        (xp_ref, xc_ref, xn_ref, mod_ref, vf_ref, pv_ref, wr_ref, wk_ref, wv_ref, w1_ref, w2_ref,
         a1_ref, a2_ref, g1_ref, g2_ref, e_ref, v1_ref, v2_ref,
         r_ref, v_ref, g_ref, kk_ref, lw_ref, kd_ref, ag_ref, hbuf, hs) = refs
    else:
        (xp_ref, xc_ref, xn_ref, mod_ref, pv_ref, wr_ref, wk_ref, wv_ref, w1_ref, w2_ref,
         a1_ref, a2_ref, g1_ref, g2_ref, e_ref,
         r_ref, v_ref, g_ref, kk_ref, lw_ref, kd_ref, ag_ref, hbuf, hs) = refs
    i = pl.program_id(1)
    one_scale = 1.0 + mod_ref[1:2, :]
    shift = mod_ref[0:1, :]
    hbuf[0:HALO, :] = xp_ref[...] * one_scale + shift
    hbuf[HALO:HALO + TM, :] = xc_ref[...] * one_scale + shift
    hbuf[HALO + TM:, :] = xn_ref[...] * one_scale + shift
    row = lax.broadcasted_iota(jnp.int32, (TM, 1), 0)
    q = D // 4

    @pl.when(i == 0)
    def _():
        hs[:, 0:2 * q] = jnp.where(row == 0, 0.0, hbuf[HALO - 1:HALO - 1 + TM, 0:2 * q])
        hs[:, 2 * q:] = jnp.where(row == TM - 1, 0.0, hbuf[HALO + 1:HALO + 1 + TM, 2 * q:])

    @pl.when(i > 0)
    def _():
        col = row & (GRID_W - 1)
        hs[:, 0:q] = jnp.where(col == 0, 0.0, hbuf[HALO - 1:HALO - 1 + TM, 0:q])
        hs[:, q:2 * q] = jnp.where(col == GRID_W - 1, 0.0, hbuf[HALO + 1:HALO + 1 + TM, q:2 * q])
        top = jnp.logical_and(i == 1, row < GRID_W)
        hs[:, 2 * q:3 * q] = jnp.where(top, 0.0, hbuf[0:TM, 2 * q:3 * q])
        bottom = jnp.logical_and(i == n_tiles - 1, row >= TM - GRID_W)
        hs[:, 3 * q:] = jnp.where(bottom, 0.0, hbuf[2 * HALO:2 * HALO + TM, 3 * q:])

    h = hbuf[HALO:HALO + TM, :]
    xx = hs[...] - h
    mix = lambda j: _bf(h + xx * pv_ref[PV_MU + j:PV_MU + j + 1, :])
    xr, xw, xk, xv, xa, xg = (mix(j) for j in range(6))
    r = _dot(xr, wr_ref[...])
    k = _dot(xk, wk_ref[...])
    v = _dot(xv, wv_ref[...])
    if has_vres:
        gate = _sigmoid(pv_ref[PV_V0:PV_V0 + 1, :] + _dot(_bf(_dot(xv, v1_ref[...])), v2_ref[...]))
        v = v + (vf_ref[...].astype(F32) - v) * gate
    g = _dot(_bf(_sigmoid(_dot(xg, g1_ref[...]))), g2_ref[...])
    kx = k * pv_ref[PV_KK:PV_KK + 1, :]
    kkn = kx * lax.rsqrt(jnp.maximum(_head_sum(kx * kx, e_ref), 1e-24))
    r_ref[...] = _bf(r)
    v_ref[...] = _bf(v)
    g_ref[...] = _bf(g)
    kk_ref[...] = _bf(kkn)
    tw = jnp.tanh(_dot(xw, w1_ref[...]))
    ta = _dot(xa, a1_ref[...])
    for d in range(2):
        cols = slice(d * LORA_PAD, (d + 1) * LORA_PAD)
        half_w = pv_ref[PV_W0 + d:PV_W0 + d + 1, :] + _dot(_bf(tw[:, cols]), w2_ref[d])
        c = -0.5 * math.exp(-0.5)
        lw_ref[d] = c * jnp.tanh(half_w) + c
        th = jnp.tanh(pv_ref[PV_A0 + d:PV_A0 + d + 1, :] + _dot(_bf(ta[:, cols]), a2_ref[d]))
        ag_ref[d] = _bf(0.5 * th + 0.5)
        kd_ref[d] = _bf(k * (pv_ref[PV_KA0:PV_KA0 + 1, :] + pv_ref[PV_KA1:PV_KA1 + 1, :] * th))


def _rwkv_project(xs, mod, vf, p):
    B, N, _ = xs.shape
    nt = N // TM
    hp = TM // HALO
    n_halo = N // HALO
    tile = pl.BlockSpec((None, TM, D), lambda b, i: (b, i, 0))
    tile2 = pl.BlockSpec((2, None, TM, D), lambda b, i: (0, b, i, 0))
    in_specs = [pl.BlockSpec((None, HALO, D), lambda b, i: (b, jnp.maximum(i * hp - 1, 0), 0)),
                tile,
                pl.BlockSpec((None, HALO, D), lambda b, i: (b, jnp.minimum((i + 1) * hp, n_halo - 1), 0)),
                pl.BlockSpec((None, None, 6, D), lambda b, i: (b, jnp.minimum(i, 1), 0, 0))]
    args = [xs, xs, xs, mod]
    has_vres = vf is not None
    if has_vres:
        in_specs.append(tile)
        args.append(vf)
    names = ["pv", "wr", "wk", "wv", "w1", "w2", "a1", "a2", "g1", "g2", "e"] + (["v1", "v2"] if has_vres else [])
    for nm in names:
        in_specs.append(_resident(p[nm].shape))
        args.append(p[nm])
    one = jax.ShapeDtypeStruct((B, N, D), BF16)
    two = jax.ShapeDtypeStruct((2, B, N, D), BF16)
    return pl.pallas_call(
        functools.partial(_rwproj_kernel, has_vres=has_vres, n_tiles=nt),
        grid=(B, nt),
        in_specs=in_specs,
        out_specs=[tile, tile, tile, tile, tile2, tile2, tile2],
        out_shape=[one, one, one, one, jax.ShapeDtypeStruct((2, B, N, D), F32), two, two],
        scratch_shapes=[pltpu.VMEM((TM + 2 * HALO, D), F32), pltpu.VMEM((TM, D), F32)],
        compiler_params=_params(2),
        name="rwkv_project",
    )(*args)


def _bd(x, head):
    zero = jnp.zeros_like(x)
    return jnp.concatenate([jnp.where(head == hh, x, zero) for hh in range(PAIR_W // RW_HEAD)], axis=0)


def _diag_blocks(full, head):
    return jnp.where(head == 0, full[0:RW_HEAD], full[RW_HEAD:2 * RW_HEAD])


def _scan_chunk(r_ref, v_ref, kk_ref, lw_ref, kd_ref, ag_ref, rows, cols, reverse):
    L, W = SCAN_L, PAIR_W
    row = lax.broadcasted_iota(jnp.int32, (L, W), 0)
    lane = lax.broadcasted_iota(jnp.int32, (L, W), 1)
    pos = lane & (RW_HEAD - 1)
    head = lane >> 6
    strict = (pos > row) if reverse else (pos < row)
    incl = (pos >= row) if reverse else (pos <= row)
    ident = pos == row
    bd = lambda x: _bd(x, head)

    load = lambda ref: ref[rows, cols].astype(F32)
    lw = load(lw_ref)
    cum = lw
    step = 1
    while step < L:
        if reverse:
            cum = cum + jnp.where(row < L - step, pltpu.roll(cum, L - step, 0), 0.0)
        else:
            cum = cum + jnp.where(row >= step, pltpu.roll(cum, step, 0), 0.0)
        step *= 2
    gam = jnp.exp(cum)
    gam_inv = jnp.exp(-cum)
    kk = load(kk_ref)
    a_t = _bf(-kk * jnp.exp(cum - lw))
    r_t = load(r_ref) * gam
    b_t = kk * load(ag_ref) * gam_inv
    k_t = load(kd_ref) * gam_inv
    vb = v_ref[rows, cols]
    g_last = gam[0:1, :] if reverse else gam[L - 1:L, :]
    bd_v = bd(vb)
    g4 = _dot_nt(jnp.concatenate([a_t, _bf(r_t)], axis=0),
                 jnp.concatenate([bd(_bf(b_t)), bd(_bf(k_t))], axis=0))
    yield
    n_m = jnp.where(strict, g4[0:L, 0:W], 0.0)
    p_m = jnp.where(strict, g4[0:L, W:2 * W], 0.0)
    rb = _bf(jnp.where(incl, g4[L:2 * L, 0:W], 0.0))
    rk = _bf(jnp.where(incl, g4[L:2 * L, W:2 * W], 0.0))
    pv_rkv = _dot(jnp.concatenate([_bf(p_m), rk], axis=0), bd_v)
    pv = pv_rkv[0:L]
    t_m = jnp.where(ident, 1.0, 0.0)
    m = 1
    while m < L:
        later, earlier = (pos, row) if reverse else (row, pos)
        coupling = jnp.logical_and(jnp.logical_and((later & m) != 0, (earlier & m) == 0),
                                   (row // (2 * m)) == (pos // (2 * m)))
        n_off = jnp.where(coupling, n_m, 0.0)
        if m == 1:
            t_m = t_m + n_off
        else:
            t_b = _bf(t_m)
            half = _bf(_dot(t_b, bd(_bf(n_off))))
            yield
            t_m = t_m + _dot(half, bd(t_b))
            yield
        m *= 2
    t_m = _bf(t_m)
    tw = _dot(t_m, jnp.concatenate([bd(a_t), bd(_bf(pv))], axis=1))
    yield
    a_hat = _bf(tw[:, 0:W])
    w_b = _bf(tw[:, W:2 * W])
    rb_aw = _dot(rb, jnp.concatenate([bd(a_hat), bd(w_b)], axis=1))
    q_hat = _bf(r_t + rb_aw[:, 0:W])
    y_c = rb_aw[:, W:2 * W] + pv_rkv[L:2 * L]
    b_g = _bf(b_t * g_last)
    k_g = _bf(k_t * g_last)
    m_m = jnp.where(ident, g_last, 0.0) + _diag_blocks(_dot_tn(b_g, a_hat), head)
    c_m = _diag_blocks(_dot_tn(jnp.concatenate([b_g, k_g], axis=0), jnp.concatenate([w_b, vb], axis=0)), head)
    return q_hat, y_c, _bf(m_m), c_m


def _lockstep(gens):
    results = [None] * len(gens)
    active = list(enumerate(gens))
    while active:
        still = []
        for idx, g in active:
            try:
                next(g)
                still.append((idx, g))
            except StopIteration as stop:
                results[idx] = stop.value
        active = still
    return results


def _scan_kernel(rf_ref, vf_ref, kkf_ref, lwf_ref, kdf_ref, agf_ref,
                 rb_ref, vb_ref, kkb_ref, lwb_ref, kdb_ref, agb_ref,
                 yf_ref, yb_ref, hf_ref, hb_ref):
    @pl.when(pl.program_id(2) == 0)
    def _():
        hf_ref[...] = jnp.zeros_like(hf_ref)
        hb_ref[...] = jnp.zeros_like(hb_ref)

    L = SCAN_L
    n_chunks = TM // L
    fwd = (rf_ref, vf_ref, kkf_ref, lwf_ref, kdf_ref, agf_ref)
    bwd = (rb_ref, vb_ref, kkb_ref, lwb_ref, kdb_ref, agb_ref)
    f_rows = [slice(c * L, (c + 1) * L) for c in range(n_chunks)]
    b_rows = f_rows[::-1]
    pairs = [slice(p * PAIR_W, (p + 1) * PAIR_W) for p in range(SCAN_W // PAIR_W)]
    chains = ([(fwd, hf_ref, yf_ref, f_rows, cols, False) for cols in pairs]
              + [(bwd, hb_ref, yb_ref, b_rows, cols, True) for cols in pairs])
    head = lax.broadcasted_iota(jnp.int32, (L, PAIR_W), 1) >> 6
    for first in range(0, n_chunks, SCAN_ROUND):
        steps = range(first, first + SCAN_ROUND)
        jobs = [(chain, i) for i in steps for chain in chains]
        pre = _lockstep([_scan_chunk(*refs, rows[i], cols, rev) for (refs, _, _, rows, cols, rev), i in jobs])
        for ((_, h_ref, y_ref, rows, cols, _), i), (q_hat, y_c, m_b, c_m) in zip(jobs, pre):
            both = _dot(jnp.concatenate([q_hat, m_b], axis=0), _bd(_bf(h_ref[:, cols]), head))
            y_ref[rows[i], cols] = (both[0:L] + y_c).astype(y_ref.dtype)
            h_ref[:, cols] = both[L:2 * L] + c_m


def _rwkv_scan(r, v, kk, lw, kd, ag, n_ctx):
    B, N, _ = r.shape
    nt = N // TM
    nc = n_ctx // TM

    def back(s):
        return jnp.where(s < nc, nc - 1 - s, nt + nc - 1 - s)

    f_one = pl.BlockSpec((None, TM, SCAN_W), lambda b, g, s: (b, s, g))
    b_one = pl.BlockSpec((None, TM, SCAN_W), lambda b, g, s: (b, back(s), g))
    f_two = pl.BlockSpec((None, None, TM, SCAN_W), lambda b, g, s: (0, b, s, g))
    b_two = pl.BlockSpec((None, None, TM, SCAN_W), lambda b, g, s: (1, b, back(s), g))
    out = jax.ShapeDtypeStruct((B, N, D), BF16)
    return pl.pallas_call(
        _scan_kernel,
        grid=(B, D // SCAN_W, nt),
        in_specs=[f_one, f_one, f_one, f_two, f_two, f_two, b_one, b_one, b_one, b_two, b_two, b_two],
        out_specs=[f_one, b_one],
        out_shape=[out, out],
        scratch_shapes=[pltpu.VMEM((RW_HEAD, SCAN_W), F32), pltpu.VMEM((RW_HEAD, SCAN_W), F32)],
        compiler_params=_params(3),
        name="rwkv_scan",
    )(r, v, kk, lw, kd, ag, r, v, kk, lw, kd, ag)


def _rwread_kernel(yf_ref, yb_ref, r_ref, v_ref, g_ref, kd0_ref, kd1_ref, vec_ref, e_ref, o_ref):
    inv = 1.0 / RW_HEAD
    y = yf_ref[...].astype(F32) + yb_ref[...].astype(F32)
    mu = _head_sum(y, e_ref) * inv
    yc = y - mu
    var = _head_sum(yc * yc, e_ref) * inv
    yn = yc * lax.rsqrt(var + RW_GN_EPS) * vec_ref[1:2, :] + vec_ref[2:3, :]
    f32 = lambda ref: ref[...].astype(F32)
    bonus = _head_sum(f32(r_ref) * (f32(kd0_ref) + f32(kd1_ref)) * vec_ref[0:1, :], e_ref) * f32(v_ref)
    o_ref[...] = _bf((yn + bonus) * f32(g_ref))


def _rwkv_readout(yf, yb, r, v, g, kd, vec, e, latent_only, n_ctx):
    B, N, _ = r.shape
    tm = TM if latent_only else TM * max(s for s in (3, 2, 1) if N % (s * TM) == 0)
    t0 = n_ctx // tm if latent_only else 0
    nt = N // tm - t0
    one = pl.BlockSpec((None, tm, D), lambda b, i: (b, i + t0, 0))
    pick = lambda d: pl.BlockSpec((None, None, tm, D), lambda b, i: (d, b, i + t0, 0))
    return pl.pallas_call(
        _rwread_kernel,
        grid=(B, nt),
        in_specs=[one, one, one, one, one, pick(0), pick(1), _resident(vec.shape), _resident(e.shape)],
        out_specs=pl.BlockSpec((None, tm, D), lambda b, i: (b, i, 0)),
        out_shape=jax.ShapeDtypeStruct((B, nt * tm, D), BF16),
        compiler_params=_params(2),
        name="rwkv_readout",
    )(yf, yb, r, v, g, kd, kd, vec, e)


def _rope_tables(n_lat, n_ctx):
    t = jnp.arange(n_lat)
    rowp = (t // GRID_W).astype(F32)
    colp = (t % GRID_W).astype(F32)

    def table(dim):
        n_freq = dim // 4
        inv = ROPE_BASE ** (-jnp.arange(n_freq, dtype=F32) / n_freq)
        ang = jnp.concatenate([rowp[:, None] * inv, colp[:, None] * inv], -1)
        cos, sin = jnp.cos(ang), jnp.sin(ang)
        reps = 128 // dim
        cos_t = jnp.tile(jnp.concatenate([cos, cos], -1), (1, reps))
        sin_t = jnp.tile(jnp.concatenate([-sin, sin], -1), (1, reps))
        cos_t = jnp.concatenate([jnp.ones((n_ctx, 128), F32), cos_t], 0)
        sin_t = jnp.concatenate([jnp.zeros((n_ctx, 128), F32), sin_t], 0)
        return cos_t, sin_t

    ca, sa = table(DA_QK)
    cb, sb = table(RT_QK)
    return ca, sa, cb, sb


def _pad_cols(w, n):
    return jnp.pad(w, ((0, 0), (0, n - w.shape[1])))


def _pad_rows(w, n):
    return jnp.pad(w, ((0, n - w.shape[0]), (0, 0)))


def kernel(x, c, ctx, c_ctx, mod_w, mod_b, ln1_g, ln1_b, ln2_g, ln2_b, ffn_w1, ffn_w3, ffn_w2, ev_w_in, ev_w_out, da_lam_q1, da_lam_k1, da_lam_q2, da_lam_k2, da_gn_g, rt_decay_logit, rw_mu, rw_wr, rw_wk, rw_wv, rw_wo, rw_w0, rw_w1, rw_w2, rw_a0, rw_a1, rw_a2, rw_v0, rw_v1, rw_v2, rw_g1, rw_g2, rw_kk, rw_ka, rw_rk, rw_lnx_g, rw_lnx_b):
    B, T, _ = x.shape
    n_ctx = ctx.shape[1]
    assert n_ctx == TM and T % TM == 0 and x.shape[2] == D and B <= 15
    xs = jnp.concatenate([ctx, x], axis=1)

    cc = jnp.zeros((16, D), F32).at[:B].set(c).at[B].set(c_ctx)
    mod_all = _adaln(cc, mod_w, mod_b)
    m_lat = mod_all[:, :B].reshape(DEPTH, B, 1, 6, D)
    m_ctx = jnp.broadcast_to(mod_all[:, B].reshape(DEPTH, 1, 1, 6, D), (DEPTH, B, 1, 6, D))
    mod_tab = jnp.concatenate([m_ctx, m_lat], axis=2)

    tabs = _rope_tables(T, n_ctx)
    lane_head = np.arange(HEADS4_W) // RW_HEAD
    head_ones = jnp.asarray(lane_head[:, None] == lane_head[None, :], BF16)

    vf = None
    for l in range(DEPTH):
        last = l == DEPTH - 1
        mod = mod_tab[l]
        ln = jnp.stack([ln1_g[l], ln1_b[l], ln2_g[l], ln2_b[l]])
        if l % 2 == 0:
            e = l // 2
            lam_init = 0.8 - 0.6 * math.exp(-0.3 * l)
            lam = (jnp.exp(jnp.sum(da_lam_q1[e] * da_lam_k1[e])) - jnp.exp(jnp.sum(da_lam_q2[e] * da_lam_k2[e]))
                   + lam_init).reshape(1).astype(F32)
            gn = (da_gn_g[e] * (1.0 - lam_init)).reshape(1, -1)
            log_gamma = jnp.log(jax.nn.sigmoid(rt_decay_logit[e].astype(F32)))
            aq, ak, av, bq, bk, bv, bg = _even_project(xs, mod, _bf(ev_w_in[e]), tabs)
            a_mix = _diff_attention(aq, ak, av, lam, gn, n_ctx)
            b_mix = _retention(bq, bk, bv, bg, log_gamma, n_ctx)
            mixes = (a_mix, b_mix)
            wo = _bf(ev_w_out[e])
        else:
            j = l // 2
            has_vres = j > 0
            pvec = jnp.zeros((16, D), F32)
            pvec = pvec.at[PV_MU:PV_MU + 6].set(rw_mu[j]).at[PV_W0:PV_W0 + 2].set(0.5 * rw_w0[j])
            pvec = pvec.at[PV_A0:PV_A0 + 2].set(0.5 * rw_a0[j]).at[PV_KK].set(rw_kk[j])
            pvec = pvec.at[PV_KA0].set(1.0 - 0.5 * rw_ka[j]).at[PV_KA1].set(0.5 * rw_ka[j])
            p = {
                "wr": _bf(rw_wr[j]), "wk": _bf(rw_wk[j]), "wv": _bf(rw_wv[j]),
                "w1": _bf(jnp.concatenate([_pad_cols(rw_w1[j, d], LORA_PAD) for d in range(2)], 1)),
                "w2": _bf(jnp.stack([_pad_rows(0.5 * rw_w2[j, d], LORA_PAD) for d in range(2)])),
                "a1": _bf(jnp.concatenate([_pad_cols(rw_a1[j, d], LORA_PAD) for d in range(2)], 1)),
                "a2": _bf(jnp.stack([_pad_rows(0.5 * rw_a2[j, d], LORA_PAD) for d in range(2)])),
                "g1": _bf(_pad_cols(rw_g1[j], GATE_PAD)), "g2": _bf(_pad_rows(rw_g2[j], GATE_PAD)),
                "e": head_ones,
            }
            if has_vres:
                pvec = pvec.at[PV_V0].set(rw_v0[j - 1])
                p["v1"] = _bf(_pad_cols(rw_v1[j - 1], LORA_PAD))
                p["v2"] = _bf(_pad_rows(rw_v2[j - 1], LORA_PAD))
            p["pv"] = pvec
            r, v, g, kk, lw, kd, ag = _rwkv_project(xs, mod, vf if has_vres else None, p)
            if not has_vres:
                vf = v
            yf, yb = _rwkv_scan(r, v, kk, lw, kd, ag, n_ctx)
            vec = jnp.zeros((8, D), F32).at[0].set(rw_rk[j].reshape(-1)).at[1].set(rw_lnx_g[j]).at[2].set(rw_lnx_b[j])
            mixes = (_rwkv_readout(yf, yb, r, v, g, kd, vec, head_ones, last, n_ctx),)
            wo = _bf(rw_wo[j])
        xs = _post_mixer(xs, mod, mixes, wo, _bf(ffn_w1[l]), _bf(ffn_w3[l]), _bf(ffn_w2[l]), ln, last, n_ctx)
    return xs
```

```python
import functools
import math

import jax
import jax.numpy as jnp
import numpy as np
from jax import lax
from jax.experimental import pallas as pl
from jax.experimental.pallas import tpu as pltpu

F32 = jnp.float32
BF16 = jnp.bfloat16

D = 1024
DEPTH = 4
GRID_W = 64
ALPHA = (2.0 * DEPTH) ** 0.25
LN_EPS = 1e-6
ROPE_BASE = 10000.0
D_FF = 2816
FF_CHUNKS = (768, 768, 768, 512)
DA_QK = 64
DA_HEADS = 4
RT_QK = 128
RT_HEADS = 4
RT_CHUNK = 128
Q_BLOCK = 128
ATT_ROWS = 768
EV_SEG = 512
RW_HEAD = 64
RW_GN_EPS = 64e-5
LORA_PAD = 128
GATE_PAD = 256
TM = 256
HALO = GRID_W
SCAN_L = 64
HEADS4_W = 4 * RW_HEAD
SCAN_W = 16 * RW_HEAD
PAIR_W = 2 * RW_HEAD
SCAN_ROUND = 4
VMEM_LIMIT = 56 * 1024 * 1024


def _dot(a, b):
    return jnp.dot(a, b, preferred_element_type=F32)


def _dot_nt(a, b):
    return lax.dot_general(a, b, (((1,), (1,)), ((), ())), preferred_element_type=F32)


def _dot_tn(a, b):
    return lax.dot_general(a, b, (((0,), (0,)), ((), ())), preferred_element_type=F32)


def _bf(x):
    return x.astype(BF16)


def _split2(x):
    hi = _bf(x)
    return hi, _bf(x - hi.astype(F32))


def _sigmoid(x):
    return 0.5 * jnp.tanh(0.5 * x) + 0.5


def _ln(x, g, b):
    mu = jnp.mean(x, -1, keepdims=True)
    xc = x - mu
    var = jnp.mean(xc * xc, -1, keepdims=True)
    return xc * lax.rsqrt(var + LN_EPS) * g + b


def _params(n_grid):
    return pltpu.CompilerParams(dimension_semantics=("arbitrary",) * n_grid,
                                vmem_limit_bytes=VMEM_LIMIT)


def _resident(shape):
    nd = len(shape)
    return pl.BlockSpec(shape, lambda *_: (0,) * nd, pipeline_mode=pl.Buffered(1))


def _mod_kernel(c_ref, w_ref, b_ref, o_ref):
    c = c_ref[...]
    act = _bf(c * _sigmoid(c))
    o_ref[...] = _dot(act, _bf(w_ref[...])) + b_ref[...]


def _adaln(cc, mod_w, mod_b):
    tn = 1536
    nt = mod_w.shape[2] // tn
    return pl.pallas_call(
        _mod_kernel,
        grid=(DEPTH, nt),
        in_specs=[pl.BlockSpec((16, D), lambda l, j: (0, 0)),
                  pl.BlockSpec((None, D, tn), lambda l, j: (l, 0, j)),
                  pl.BlockSpec((None, 1, tn), lambda l, j: (l, 0, j))],
        out_specs=pl.BlockSpec((None, 16, tn), lambda l, j: (l, 0, j)),
        out_shape=jax.ShapeDtypeStruct((DEPTH, 16, mod_w.shape[2]), F32),
        compiler_params=_params(2),
        name="adaln",
    )(cc, mod_w, mod_b.reshape(DEPTH, 1, -1))


def _evproj_kernel(x_ref, mod_ref, w_ref, ca_ref, sa_ref, cb_ref, sb_ref,
                   aq_ref, ak_ref, av_ref, bq_ref, bk_ref, bv_ref, bg_ref, h_ref):
    rows_all = x_ref.shape[0]
    for j in range(rows_all // TM):
        m = mod_ref[jnp.where(pl.program_id(1) == 0, 0, 1) if j == 0 else 1]
        rows = slice(j * TM, (j + 1) * TM)
        h_ref[rows, :] = _bf(x_ref[rows, :] * (1.0 + m[1:2, :]) + m[0:1, :])
    h = h_ref[...]
    ca, sa, cb, sb = ca_ref[...], sa_ref[...], cb_ref[...], sb_ref[...]
    lane = lax.broadcasted_iota(jnp.int32, (rows_all, 128), 1)
    first_half = (lane & (DA_QK // 2)) == 0

    def seg(j):
        return _dot(h, w_ref[:, j * EV_SEG:(j + 1) * EV_SEG])

    def rope_a(p, out_ref, scale):
        for j in range(EV_SEG // 128):
            blk = p[:, j * 128:(j + 1) * 128]
            sw = jnp.where(first_half, pltpu.roll(blk, 96, 1), pltpu.roll(blk, 32, 1))
            out_ref[:, j * 128:(j + 1) * 128] = _bf((blk * ca + sw * sa) * scale)

    def rope_b(p, out_ref):
        for j in range(EV_SEG // 128):
            blk = p[:, j * 128:(j + 1) * 128]
            out_ref[:, j * 128:(j + 1) * 128] = _bf(blk * cb + pltpu.roll(blk, 64, 1) * sb)

    rope_a(seg(0), aq_ref, DA_QK ** -0.5 * math.log2(math.e))
    rope_a(seg(1), ak_ref, 1.0)
    av_ref[...] = _bf(seg(2))
    rope_b(seg(3), bq_ref)
    rope_b(seg(4) * (RT_QK ** -0.5), bk_ref)
    bv_ref[...] = _bf(seg(5))
    bg_ref[...] = seg(6)


def _even_project(xs, mod, w_in, tabs):
    B, N, _ = xs.shape
    tm = TM * max(s for s in (3, 2, 1) if N % (s * TM) == 0)
    nt = N // tm
    tile = lambda w: pl.BlockSpec((None, tm, w), lambda b, i: (b, i, 0))
    tab = pl.BlockSpec((tm, 128), lambda b, i: (i, 0))
    outs = [jax.ShapeDtypeStruct((B, N, EV_SEG), BF16)] * 6 + [jax.ShapeDtypeStruct((B, N, EV_SEG), F32)]
    return pl.pallas_call(
        _evproj_kernel,
        grid=(B, nt),
        in_specs=[tile(D), pl.BlockSpec((None, 2, 6, D), lambda b, i: (b, 0, 0, 0)),
                  _resident(w_in.shape), tab, tab, tab, tab],
        out_specs=[tile(EV_SEG)] * 7,
        out_shape=outs,
        scratch_shapes=[pltpu.VMEM((tm, D), BF16)],
        compiler_params=_params(2),
        name="even_project",
    )(xs, mod, w_in, *tabs)


def _attn_block(q, k_ref, v_ref, nk, lam, gn):
    lane = lax.broadcasted_iota(jnp.int32, q.shape, 1)
    zero = jnp.zeros_like(q)
    qq = jnp.concatenate([jnp.where(lane < DA_QK, q, zero), jnp.where(lane >= DA_QK, q, zero)], axis=0)
    s = _dot_nt(qq, k_ref[0:nk, :])
    yield
    e = jnp.exp2(s - jnp.max(s, -1, keepdims=True))
    inv = 1.0 / jnp.sum(e, -1, keepdims=True)
    eb = _bf(e)
    yield
    ev = _dot(eb, v_ref[0:nk, :])
    o = ev[0:Q_BLOCK] * inv[0:Q_BLOCK] - ev[Q_BLOCK:2 * Q_BLOCK] * (lam * inv[Q_BLOCK:2 * Q_BLOCK])
    return o * lax.rsqrt(jnp.mean(o * o, -1, keepdims=True) + LN_EPS) * gn


def _attn_kernel(lam_ref, q_ref, k_ref, v_ref, gn_ref, o_ref, *, n_ctx, n_all):
    i = pl.program_id(2)
    lam = lam_ref[0]
    gn = gn_ref[...]
    subs = [slice(j * Q_BLOCK, (j + 1) * Q_BLOCK) for j in range(ATT_ROWS // Q_BLOCK)]
    n_ctx_subs = n_ctx // Q_BLOCK

    def run(first_tile):
        nks = [n_ctx if (first_tile and j < n_ctx_subs) else n_all for j in range(len(subs))]
        outs = _lockstep([_attn_block(q_ref[rows, :], k_ref, v_ref, nk, lam, gn) for rows, nk in zip(subs, nks)],
                         skew=1)
        for rows, o in zip(subs, outs):
            o_ref[rows, :] = o.astype(o_ref.dtype)

    @pl.when(i == 0)
    def _():
        run(True)

    @pl.when(i > 0)
    def _():
        run(False)


def _diff_attention(aq, ak, av, lam, gn, n_ctx):
    B, N, _ = aq.shape
    assert N % ATT_ROWS == 0 and n_ctx <= ATT_ROWS and n_ctx % Q_BLOCK == 0
    kv = pl.BlockSpec((None, N, 128), lambda b, h, i: (b, 0, h))
    qo = pl.BlockSpec((None, ATT_ROWS, 128), lambda b, h, i: (b, i, h))
    return pl.pallas_call(
        functools.partial(_attn_kernel, n_ctx=n_ctx, n_all=N),
        grid=(B, DA_HEADS, N // ATT_ROWS),
        in_specs=[pl.BlockSpec(memory_space=pltpu.SMEM), qo, kv, kv,
                  pl.BlockSpec((1, 128), lambda b, h, i: (0, h))],
        out_specs=qo,
        out_shape=jax.ShapeDtypeStruct((B, N, EV_SEG), BF16),
        compiler_params=_params(3),
        name="diff_attention",
    )(lam, aq, ak, av, gn)


def _ret_kernel(lg_ref, q_ref, k_ref, v_ref, g_ref, o_ref, sf_ref, sb_ref, accf_ref, accb_ref,
                *, n_chunks, n_ctx_chunks):
    C = RT_CHUNK
    ii = lax.broadcasted_iota(jnp.int32, (C, C), 0)
    jj = lax.broadcasted_iota(jnp.int32, (C, C), 1)
    idx = lax.broadcasted_iota(jnp.int32, (C, 1), 0).astype(F32)
    consts = {}
    for h in range(RT_HEADS):
        for d in range(2):
            lg = lg_ref[d, h]
            diff = ((ii - jj) if d == 0 else (jj - ii)).astype(F32)
            dec = jnp.where(diff >= 0, jnp.exp(lg * jnp.maximum(diff, 0.0)), 0.0)
            if d == 0:
                zeta = jnp.exp(lg * (C - 1.0 - idx))
                xi = jnp.exp(lg * (idx + 1.0))
            else:
                zeta = jnp.exp(lg * idx)
                xi = jnp.exp(lg * (C - idx))
            consts[h, d] = (dec, zeta, xi, jnp.exp(lg * jnp.full((1, 1), float(C), F32)))
    sf_ref[...] = jnp.zeros_like(sf_ref)
    sb_ref[...] = jnp.zeros_like(sb_ref)

    def chunk(s, h, d, s_ref, acc_ref):
        dec, zeta, xi, g_chunk = consts[h, d]
        cols = slice(h * RT_QK, (h + 1) * RT_QK)
        if d == 0:
            c = s
        else:
            c = jnp.where(s < n_ctx_chunks, n_ctx_chunks - 1 - s, n_chunks + n_ctx_chunks - 1 - s)
        rows = pl.ds(pl.multiple_of(c * C, C), C)
        qc = q_ref[rows, cols]
        kc = k_ref[rows, cols]
        vc = v_ref[rows, cols]
        state = s_ref[:, cols]
        inner = _dot_nt(qc, kc)
        cross = _dot(qc, _bf(state))
        kz = _bf(kc.astype(F32) * zeta)
        s_ref[:, cols] = g_chunk * state + _dot_tn(kz, vc)
        yield
        acc_ref[rows, cols] = _dot(_bf(inner * dec), vc) + cross * xi

    def step(s, carry):
        _lockstep([chunk(s, h, d, s_ref, acc_ref) for h in range(RT_HEADS)
                   for d, (s_ref, acc_ref) in enumerate(((sf_ref, accf_ref), (sb_ref, accb_ref)))])
        return carry

    lax.fori_loop(0, n_chunks, step, 0)
    for h in range(RT_HEADS):
        cols = slice(h * RT_QK, (h + 1) * RT_QK)
        b = accf_ref[:, cols] + accb_ref[:, cols]
        g = g_ref[:, cols]
        b = b * lax.rsqrt(jnp.mean(b * b, -1, keepdims=True) + LN_EPS)
        o_ref[:, cols] = (b * (g * _sigmoid(g))).astype(o_ref.dtype)


def _retention(bq, bk, bv, bg, log_gamma, n_ctx):
    B, N, W = bq.shape
    blk = pl.BlockSpec((None, N, W), lambda b: (b, 0, 0))
    return pl.pallas_call(
        functools.partial(_ret_kernel, n_chunks=N // RT_CHUNK, n_ctx_chunks=n_ctx // RT_CHUNK),
        grid=(B,),
        in_specs=[pl.BlockSpec(memory_space=pltpu.SMEM), blk, blk, blk, blk],
        out_specs=blk,
        out_shape=jax.ShapeDtypeStruct((B, N, W), BF16),
        scratch_shapes=[pltpu.VMEM((RT_QK, W), F32), pltpu.VMEM((RT_QK, W), F32),
                        pltpu.VMEM((N, W), F32), pltpu.VMEM((N, W), F32)],
        compiler_params=_params(1),
        name="retention",
    )(log_gamma, bq, bk, bv, bg)


def _post_kernel(*refs, mix_widths, n_sub, has_ctx):
    n_mix = len(mix_widths)
    x_ref, mod_ref = refs[0], refs[1]
    mix_refs = refs[2:2 + n_mix]
    wo_ref, w1_ref, w3_ref, w2_ref, ln_ref, o_ref, x1_ref, hm_ref = refs[2 + n_mix:]
    i = pl.program_id(1)
    o = None
    off = 0
    for m_ref, w in zip(mix_refs, mix_widths):
        part = _dot(m_ref[...], wo_ref[off:off + w, :])
        o = part if o is None else o + part
        off += w
    subs = [slice(j * TM, (j + 1) * TM) for j in range(n_sub)]
    mods = []
    for j, rows in enumerate(subs):
        seg = jnp.where(i == 0, 0, 1) if (has_ctx and j == 0) else 1
        m = mod_ref[seg]
        mods.append(m)
        x1 = _ln(ALPHA * x_ref[rows, :] + m[2:3, :] * o[rows, :], ln_ref[0:1, :], ln_ref[1:2, :])
        x1_ref[rows, :] = x1
        hm_ref[rows, :] = _bf(x1 * (1.0 + m[4:5, :]) + m[3:4, :])
    hm = hm_ref[...]
    f = None
    off = 0
    for w in FF_CHUNKS:
        cols = slice(off, off + w)
        u = _dot(hm, w1_ref[:, cols])
        t = _dot(hm, w3_ref[:, cols])
        part = _dot(_bf(u * _sigmoid(u) * t), w2_ref[cols, :])
        f = part if f is None else f + part
        off += w
    for m, rows in zip(mods, subs):
        o_ref[rows, :] = _ln(ALPHA * x1_ref[rows, :] + m[5:6, :] * f[rows, :], ln_ref[2:3, :], ln_ref[3:4, :])


def _post_mixer(xs, mod, mixes, wo, w1, w3, w2, ln, latent_only, n_ctx):
    B, N, _ = xs.shape
    rows_out = N - n_ctx if latent_only else N
    n_sub = 1 if latent_only else max(s for s in (3, 2, 1) if N % (s * TM) == 0)
    tm = n_sub * TM
    t0 = n_ctx // tm if latent_only else 0
    nt = rows_out // tm
    in_tile = lambda w, off=t0: pl.BlockSpec((None, tm, w), lambda b, i: (b, i + off, 0))
    widths = tuple(m.shape[-1] for m in mixes)
    mix_off = [t0 if m.shape[1] == N else 0 for m in mixes]
    return pl.pallas_call(
        functools.partial(_post_kernel, mix_widths=widths, n_sub=n_sub, has_ctx=not latent_only),
        grid=(B, nt),
        in_specs=[in_tile(D), pl.BlockSpec((None, 2, 6, D), lambda b, i: (b, 0, 0, 0))]
                 + [in_tile(w, off) for w, off in zip(widths, mix_off)]
                 + [_resident(wo.shape), _resident(w1.shape), _resident(w3.shape), _resident(w2.shape),
                    _resident(ln.shape)],
        out_specs=pl.BlockSpec((None, tm, D), lambda b, i: (b, i, 0)),
        out_shape=jax.ShapeDtypeStruct((B, rows_out, D), F32),
        scratch_shapes=[pltpu.VMEM((tm, D), F32), pltpu.VMEM((tm, D), BF16)],
        compiler_params=_params(2),
        name="post_mixer",
    )(xs, mod, *mixes, wo, w1, w3, w2, ln)


PV_MU, PV_W0, PV_A0, PV_KK, PV_KA0, PV_V0, PV_KA1 = 0, 6, 8, 10, 11, 12, 13


def _head_sum(x, e_ref):
    rows = x.shape[0]
    hi, lo = _split2(x)
    cols = []
    for j in range(x.shape[1] // HEADS4_W):
        sl = slice(j * HEADS4_W, (j + 1) * HEADS4_W)
        both = _dot(jnp.concatenate([hi[:, sl], lo[:, sl]], axis=0), e_ref[...])
        cols.append(both[0:rows] + both[rows:2 * rows])
    return jnp.concatenate(cols, axis=1)


def _rwproj_kernel(*refs, has_vres, n_tiles):
    if has_vres:
        (xp_ref, xc_ref, xn_ref, mod_ref, vf_ref, pv_ref, wr_ref, wk_ref, wv_ref, w1_ref, w2_ref,
         a1_ref, a2_ref, g1_ref, g2_ref, e_ref, v1_ref, v2_ref,
         r_ref, v_ref, g_ref, kk_ref, lw_ref, kd_ref, ag_ref, hbuf, hs) = refs
    else:
        (xp_ref, xc_ref, xn_ref, mod_ref, pv_ref, wr_ref, wk_ref, wv_ref, w1_ref, w2_ref,
         a1_ref, a2_ref, g1_ref, g2_ref, e_ref,
         r_ref, v_ref, g_ref, kk_ref, lw_ref, kd_ref, ag_ref, hbuf, hs) = refs
    i = pl.program_id(1)
    one_scale = 1.0 + mod_ref[1:2, :]
    shift = mod_ref[0:1, :]
    hbuf[0:HALO, :] = xp_ref[...] * one_scale + shift
    hbuf[HALO:HALO + TM, :] = xc_ref[...] * one_scale + shift
    hbuf[HALO + TM:, :] = xn_ref[...] * one_scale + shift
    row = lax.broadcasted_iota(jnp.int32, (TM, 1), 0)
    q = D // 4

    @pl.when(i == 0)
    def _():
        hs[:, 0:2 * q] = jnp.where(row == 0, 0.0, hbuf[HALO - 1:HALO - 1 + TM, 0:2 * q])
        hs[:, 2 * q:] = jnp.where(row == TM - 1, 0.0, hbuf[HALO + 1:HALO + 1 + TM, 2 * q:])

    @pl.when(i > 0)
    def _():
        col = row & (GRID_W - 1)
        hs[:, 0:q] = jnp.where(col == 0, 0.0, hbuf[HALO - 1:HALO - 1 + TM, 0:q])
        hs[:, q:2 * q] = jnp.where(col == GRID_W - 1, 0.0, hbuf[HALO + 1:HALO + 1 + TM, q:2 * q])
        top = jnp.logical_and(i == 1, row < GRID_W)
        hs[:, 2 * q:3 * q] = jnp.where(top, 0.0, hbuf[0:TM, 2 * q:3 * q])
        bottom = jnp.logical_and(i == n_tiles - 1, row >= TM - GRID_W)
        hs[:, 3 * q:] = jnp.where(bottom, 0.0, hbuf[2 * HALO:2 * HALO + TM, 3 * q:])

    h = hbuf[HALO:HALO + TM, :]
    xx = hs[...] - h
    mix = lambda j: _bf(h + xx * pv_ref[PV_MU + j:PV_MU + j + 1, :])
    xr, xw, xk, xv, xa, xg = (mix(j) for j in range(6))
    r = _dot(xr, wr_ref[...])
    k = _dot(xk, wk_ref[...])
    v = _dot(xv, wv_ref[...])
    if has_vres:
        gate = _sigmoid(pv_ref[PV_V0:PV_V0 + 1, :] + _dot(_bf(_dot(xv, v1_ref[...])), v2_ref[...]))
        v = v + (vf_ref[...].astype(F32) - v) * gate
    g = _dot(_bf(_sigmoid(_dot(xg, g1_ref[...]))), g2_ref[...])
    kx = k * pv_ref[PV_KK:PV_KK + 1, :]
    kkn = kx * lax.rsqrt(jnp.maximum(_head_sum(kx * kx, e_ref), 1e-24))
    r_ref[...] = _bf(r)
    v_ref[...] = _bf(v)
    g_ref[...] = _bf(g)
    kk_ref[...] = _bf(kkn)
    tw = jnp.tanh(_dot(xw, w1_ref[...]))
    ta = _dot(xa, a1_ref[...])
    for d in range(2):
        cols = slice(d * LORA_PAD, (d + 1) * LORA_PAD)
        half_w = pv_ref[PV_W0 + d:PV_W0 + d + 1, :] + _dot(_bf(tw[:, cols]), w2_ref[d])
        c = -0.5 * math.exp(-0.5)
        lw_ref[d] = c * jnp.tanh(half_w) + c
        th = jnp.tanh(pv_ref[PV_A0 + d:PV_A0 + d + 1, :] + _dot(_bf(ta[:, cols]), a2_ref[d]))
        ag_ref[d] = _bf(0.5 * th + 0.5)
        kd_ref[d] = _bf(k * (pv_ref[PV_KA0:PV_KA0 + 1, :] + pv_ref[PV_KA1:PV_KA1 + 1, :] * th))


def _rwkv_project(xs, mod, vf, p):
    B, N, _ = xs.shape
    nt = N // TM
    hp = TM // HALO
    n_halo = N // HALO
    tile = pl.BlockSpec((None, TM, D), lambda b, i: (b, i, 0))
    tile2 = pl.BlockSpec((2, None, TM, D), lambda b, i: (0, b, i, 0))
    in_specs = [pl.BlockSpec((None, HALO, D), lambda b, i: (b, jnp.maximum(i * hp - 1, 0), 0)),
                tile,
                pl.BlockSpec((None, HALO, D), lambda b, i: (b, jnp.minimum((i + 1) * hp, n_halo - 1), 0)),
                pl.BlockSpec((None, None, 6, D), lambda b, i: (b, jnp.minimum(i, 1), 0, 0))]
    args = [xs, xs, xs, mod]
    has_vres = vf is not None
    if has_vres:
        in_specs.append(tile)
        args.append(vf)
    names = ["pv", "wr", "wk", "wv", "w1", "w2", "a1", "a2", "g1", "g2", "e"] + (["v1", "v2"] if has_vres else [])
    for nm in names:
        in_specs.append(_resident(p[nm].shape))
        args.append(p[nm])
    one = jax.ShapeDtypeStruct((B, N, D), BF16)
    two = jax.ShapeDtypeStruct((2, B, N, D), BF16)
    return pl.pallas_call(
        functools.partial(_rwproj_kernel, has_vres=has_vres, n_tiles=nt),
        grid=(B, nt),
        in_specs=in_specs,
        out_specs=[tile, tile, tile, tile, tile2, tile2, tile2],
        out_shape=[one, one, one, one, jax.ShapeDtypeStruct((2, B, N, D), F32), two, two],
        scratch_shapes=[pltpu.VMEM((TM + 2 * HALO, D), F32), pltpu.VMEM((TM, D), F32)],
        compiler_params=_params(2),
        name="rwkv_project",
    )(*args)


def _bd(x, head):
    zero = jnp.zeros_like(x)
    return jnp.concatenate([jnp.where(head == hh, x, zero) for hh in range(PAIR_W // RW_HEAD)], axis=0)


def _diag_blocks(full, head):
    return jnp.where(head == 0, full[0:RW_HEAD], full[RW_HEAD:2 * RW_HEAD])


def _scan_chunk(r_ref, v_ref, kk_ref, lw_ref, kd_ref, ag_ref, rows, cols, reverse):
    L, W = SCAN_L, PAIR_W
    row = lax.broadcasted_iota(jnp.int32, (L, W), 0)
    lane = lax.broadcasted_iota(jnp.int32, (L, W), 1)
    pos = lane & (RW_HEAD - 1)
    head = lane >> 6
    strict = (pos > row) if reverse else (pos < row)
    incl = (pos >= row) if reverse else (pos <= row)
    ident = pos == row
    bd = lambda x: _bd(x, head)

    load = lambda ref: ref[rows, cols].astype(F32)
    lw = load(lw_ref)
    cum = lw
    step = 1
    while step < L:
        if reverse:
            cum = cum + jnp.where(row < L - step, pltpu.roll(cum, L - step, 0), 0.0)
        else:
            cum = cum + jnp.where(row >= step, pltpu.roll(cum, step, 0), 0.0)
        step *= 2
    gam = jnp.exp(cum)
    gam_inv = jnp.exp(-cum)
    kk = load(kk_ref)
    a_t = _bf(-kk * jnp.exp(cum - lw))
    r_t = load(r_ref) * gam
    b_t = kk * load(ag_ref) * gam_inv
    k_t = load(kd_ref) * gam_inv
    vb = v_ref[rows, cols]
    g_last = gam[0:1, :] if reverse else gam[L - 1:L, :]
    bd_v = bd(vb)
    g4 = _dot_nt(jnp.concatenate([a_t, _bf(r_t)], axis=0),
                 jnp.concatenate([bd(_bf(b_t)), bd(_bf(k_t))], axis=0))
    yield
    n_m = jnp.where(strict, g4[0:L, 0:W], 0.0)
    p_m = jnp.where(strict, g4[0:L, W:2 * W], 0.0)
    rb = _bf(jnp.where(incl, g4[L:2 * L, 0:W], 0.0))
    rk = _bf(jnp.where(incl, g4[L:2 * L, W:2 * W], 0.0))
    pv_rkv = _dot(jnp.concatenate([_bf(p_m), rk], axis=0), bd_v)
    pv = pv_rkv[0:L]
    t_m = jnp.where(ident, 1.0, 0.0)
    m = 1
    while m < L:
        later, earlier = (pos, row) if reverse else (row, pos)
        coupling = jnp.logical_and(jnp.logical_and((later & m) != 0, (earlier & m) == 0),
                                   (row // (2 * m)) == (pos // (2 * m)))
        n_off = jnp.where(coupling, n_m, 0.0)
        if m == 1:
            t_m = t_m + n_off
        else:
            t_b = _bf(t_m)
            half = _bf(_dot(t_b, bd(_bf(n_off))))
            yield
            t_m = t_m + _dot(half, bd(t_b))
            yield
        m *= 2
    t_m = _bf(t_m)
    tw = _dot(t_m, jnp.concatenate([bd(a_t), bd(_bf(pv))], axis=1))
    yield
    a_hat = _bf(tw[:, 0:W])
    w_b = _bf(tw[:, W:2 * W])
    rb_aw = _dot(rb, jnp.concatenate([bd(a_hat), bd(w_b)], axis=1))
    q_hat = _bf(r_t + rb_aw[:, 0:W])
    y_c = rb_aw[:, W:2 * W] + pv_rkv[L:2 * L]
    b_g = _bf(b_t * g_last)
    k_g = _bf(k_t * g_last)
    m_m = jnp.where(ident, g_last, 0.0) + _diag_blocks(_dot_tn(b_g, a_hat), head)
    c_m = _diag_blocks(_dot_tn(jnp.concatenate([b_g, k_g], axis=0), jnp.concatenate([w_b, vb], axis=0)), head)
    return q_hat, y_c, _bf(m_m), c_m


def _lockstep(gens, skew=0):
    results = [None] * len(gens)
    done = [False] * len(gens)
    t = 0
    while not all(done):
        for idx, g in enumerate(gens):
            if done[idx] or t < idx * skew:
                continue
            try:
                next(g)
            except StopIteration as stop:
                results[idx] = stop.value
                done[idx] = True
        t += 1
    return results


def _scan_kernel(rf_ref, vf_ref, kkf_ref, lwf_ref, kdf_ref, agf_ref,
                 rb_ref, vb_ref, kkb_ref, lwb_ref, kdb_ref, agb_ref,
                 yf_ref, yb_ref, hf_ref, hb_ref):
    @pl.when(pl.program_id(2) == 0)
    def _():
        hf_ref[...] = jnp.zeros_like(hf_ref)
        hb_ref[...] = jnp.zeros_like(hb_ref)

    L = SCAN_L
    n_chunks = TM // L
    fwd = (rf_ref, vf_ref, kkf_ref, lwf_ref, kdf_ref, agf_ref)
    bwd = (rb_ref, vb_ref, kkb_ref, lwb_ref, kdb_ref, agb_ref)
    f_rows = [slice(c * L, (c + 1) * L) for c in range(n_chunks)]
    b_rows = f_rows[::-1]
    pairs = [slice(p * PAIR_W, (p + 1) * PAIR_W) for p in range(SCAN_W // PAIR_W)]
    chains = ([(fwd, hf_ref, yf_ref, f_rows, cols, False) for cols in pairs]
              + [(bwd, hb_ref, yb_ref, b_rows, cols, True) for cols in pairs])
    head = lax.broadcasted_iota(jnp.int32, (L, PAIR_W), 1) >> 6
    for first in range(0, n_chunks, SCAN_ROUND):
        steps = range(first, first + SCAN_ROUND)
        jobs = [(chain, i) for i in steps for chain in chains]
        pre = _lockstep([_scan_chunk(*refs, rows[i], cols, rev) for (refs, _, _, rows, cols, rev), i in jobs])
        for ((_, h_ref, y_ref, rows, cols, _), i), (q_hat, y_c, m_b, c_m) in zip(jobs, pre):
            both = _dot(jnp.concatenate([q_hat, m_b], axis=0), _bd(_bf(h_ref[:, cols]), head))
            y_ref[rows[i], cols] = (both[0:L] + y_c).astype(y_ref.dtype)
            h_ref[:, cols] = both[L:2 * L] + c_m


def _rwkv_scan(r, v, kk, lw, kd, ag, n_ctx):
    B, N, _ = r.shape
    nt = N // TM
    nc = n_ctx // TM

    def back(s):
        return jnp.where(s < nc, nc - 1 - s, nt + nc - 1 - s)

    f_one = pl.BlockSpec((None, TM, SCAN_W), lambda b, g, s: (b, s, g))
    b_one = pl.BlockSpec((None, TM, SCAN_W), lambda b, g, s: (b, back(s), g))
    f_two = pl.BlockSpec((None, None, TM, SCAN_W), lambda b, g, s: (0, b, s, g))
    b_two = pl.BlockSpec((None, None, TM, SCAN_W), lambda b, g, s: (1, b, back(s), g))
    out = jax.ShapeDtypeStruct((B, N, D), BF16)
    return pl.pallas_call(
        _scan_kernel,
        grid=(B, D // SCAN_W, nt),
        in_specs=[f_one, f_one, f_one, f_two, f_two, f_two, b_one, b_one, b_one, b_two, b_two, b_two],
        out_specs=[f_one, b_one],
        out_shape=[out, out],
        scratch_shapes=[pltpu.VMEM((RW_HEAD, SCAN_W), F32), pltpu.VMEM((RW_HEAD, SCAN_W), F32)],
        compiler_params=_params(3),
        name="rwkv_scan",
    )(r, v, kk, lw, kd, ag, r, v, kk, lw, kd, ag)


def _rwread_kernel(yf_ref, yb_ref, r_ref, v_ref, g_ref, kd0_ref, kd1_ref, vec_ref, e_ref, o_ref):
    inv = 1.0 / RW_HEAD
    y = yf_ref[...].astype(F32) + yb_ref[...].astype(F32)
    mu = _head_sum(y, e_ref) * inv
    yc = y - mu
    var = _head_sum(yc * yc, e_ref) * inv
    yn = yc * lax.rsqrt(var + RW_GN_EPS) * vec_ref[1:2, :] + vec_ref[2:3, :]
    f32 = lambda ref: ref[...].astype(F32)
    bonus = _head_sum(f32(r_ref) * (f32(kd0_ref) + f32(kd1_ref)) * vec_ref[0:1, :], e_ref) * f32(v_ref)
    o_ref[...] = _bf((yn + bonus) * f32(g_ref))


def _rwkv_readout(yf, yb, r, v, g, kd, vec, e, latent_only, n_ctx):
    B, N, _ = r.shape
    tm = TM if latent_only else TM * max(s for s in (3, 2, 1) if N % (s * TM) == 0)
    t0 = n_ctx // tm if latent_only else 0
    nt = N // tm - t0
    one = pl.BlockSpec((None, tm, D), lambda b, i: (b, i + t0, 0))
    pick = lambda d: pl.BlockSpec((None, None, tm, D), lambda b, i: (d, b, i + t0, 0))
    return pl.pallas_call(
        _rwread_kernel,
        grid=(B, nt),
        in_specs=[one, one, one, one, one, pick(0), pick(1), _resident(vec.shape), _resident(e.shape)],
        out_specs=pl.BlockSpec((None, tm, D), lambda b, i: (b, i, 0)),
        out_shape=jax.ShapeDtypeStruct((B, nt * tm, D), BF16),
        compiler_params=_params(2),
        name="rwkv_readout",
    )(yf, yb, r, v, g, kd, kd, vec, e)


def _rope_tables(n_lat, n_ctx):
    t = jnp.arange(n_lat)
    rowp = (t // GRID_W).astype(F32)
    colp = (t % GRID_W).astype(F32)

    def table(dim):
        n_freq = dim // 4
        inv = ROPE_BASE ** (-jnp.arange(n_freq, dtype=F32) / n_freq)
        ang = jnp.concatenate([rowp[:, None] * inv, colp[:, None] * inv], -1)
        cos, sin = jnp.cos(ang), jnp.sin(ang)
        reps = 128 // dim
        cos_t = jnp.tile(jnp.concatenate([cos, cos], -1), (1, reps))
        sin_t = jnp.tile(jnp.concatenate([-sin, sin], -1), (1, reps))
        cos_t = jnp.concatenate([jnp.ones((n_ctx, 128), F32), cos_t], 0)
        sin_t = jnp.concatenate([jnp.zeros((n_ctx, 128), F32), sin_t], 0)
        return cos_t, sin_t

    ca, sa = table(DA_QK)
    cb, sb = table(RT_QK)
    return ca, sa, cb, sb


def _pad_cols(w, n):
    return jnp.pad(w, ((0, 0), (0, n - w.shape[1])))


def _pad_rows(w, n):
    return jnp.pad(w, ((0, n - w.shape[0]), (0, 0)))


def kernel(x, c, ctx, c_ctx, mod_w, mod_b, ln1_g, ln1_b, ln2_g, ln2_b, ffn_w1, ffn_w3, ffn_w2, ev_w_in, ev_w_out, da_lam_q1, da_lam_k1, da_lam_q2, da_lam_k2, da_gn_g, rt_decay_logit, rw_mu, rw_wr, rw_wk, rw_wv, rw_wo, rw_w0, rw_w1, rw_w2, rw_a0, rw_a1, rw_a2, rw_v0, rw_v1, rw_v2, rw_g1, rw_g2, rw_kk, rw_ka, rw_rk, rw_lnx_g, rw_lnx_b):
    B, T, _ = x.shape
    n_ctx = ctx.shape[1]
    assert n_ctx == TM and T % TM == 0 and x.shape[2] == D and B <= 15
    xs = jnp.concatenate([ctx, x], axis=1)

    cc = jnp.zeros((16, D), F32).at[:B].set(c).at[B].set(c_ctx)
    mod_all = _adaln(cc, mod_w, mod_b)
    m_lat = mod_all[:, :B].reshape(DEPTH, B, 1, 6, D)
    m_ctx = jnp.broadcast_to(mod_all[:, B].reshape(DEPTH, 1, 1, 6, D), (DEPTH, B, 1, 6, D))
    mod_tab = jnp.concatenate([m_ctx, m_lat], axis=2)

    tabs = _rope_tables(T, n_ctx)
    lane_head = np.arange(HEADS4_W) // RW_HEAD
    head_ones = jnp.asarray(lane_head[:, None] == lane_head[None, :], BF16)

    vf = None
    for l in range(DEPTH):
        last = l == DEPTH - 1
        mod = mod_tab[l]
        ln = jnp.stack([ln1_g[l], ln1_b[l], ln2_g[l], ln2_b[l]])
        if l % 2 == 0:
            e = l // 2
            lam_init = 0.8 - 0.6 * math.exp(-0.3 * l)
            lam = (jnp.exp(jnp.sum(da_lam_q1[e] * da_lam_k1[e])) - jnp.exp(jnp.sum(da_lam_q2[e] * da_lam_k2[e]))
                   + lam_init).reshape(1).astype(F32)
            gn = (da_gn_g[e] * (1.0 - lam_init)).reshape(1, -1)
            log_gamma = jnp.log(jax.nn.sigmoid(rt_decay_logit[e].astype(F32)))
            aq, ak, av, bq, bk, bv, bg = _even_project(xs, mod, _bf(ev_w_in[e]), tabs)
            a_mix = _diff_attention(aq, ak, av, lam, gn, n_ctx)
            b_mix = _retention(bq, bk, bv, bg, log_gamma, n_ctx)
            mixes = (a_mix, b_mix)
            wo = _bf(ev_w_out[e])
        else:
            j = l // 2
            has_vres = j > 0
            pvec = jnp.zeros((16, D), F32)
            pvec = pvec.at[PV_MU:PV_MU + 6].set(rw_mu[j]).at[PV_W0:PV_W0 + 2].set(0.5 * rw_w0[j])
            pvec = pvec.at[PV_A0:PV_A0 + 2].set(0.5 * rw_a0[j]).at[PV_KK].set(rw_kk[j])
            pvec = pvec.at[PV_KA0].set(1.0 - 0.5 * rw_ka[j]).at[PV_KA1].set(0.5 * rw_ka[j])
            p = {
                "wr": _bf(rw_wr[j]), "wk": _bf(rw_wk[j]), "wv": _bf(rw_wv[j]),
                "w1": _bf(jnp.concatenate([_pad_cols(rw_w1[j, d], LORA_PAD) for d in range(2)], 1)),
                "w2": _bf(jnp.stack([_pad_rows(0.5 * rw_w2[j, d], LORA_PAD) for d in range(2)])),
                "a1": _bf(jnp.concatenate([_pad_cols(rw_a1[j, d], LORA_PAD) for d in range(2)], 1)),
                "a2": _bf(jnp.stack([_pad_rows(0.5 * rw_a2[j, d], LORA_PAD) for d in range(2)])),
                "g1": _bf(_pad_cols(rw_g1[j], GATE_PAD)), "g2": _bf(_pad_rows(rw_g2[j], GATE_PAD)),
                "e": head_ones,
            }
            if has_vres:
                pvec = pvec.at[PV_V0].set(rw_v0[j - 1])
                p["v1"] = _bf(_pad_cols(rw_v1[j - 1], LORA_PAD))
                p["v2"] = _bf(_pad_rows(rw_v2[j - 1], LORA_PAD))
            p["pv"] = pvec
            r, v, g, kk, lw, kd, ag = _rwkv_project(xs, mod, vf if has_vres else None, p)
            if not has_vres:
                vf = v
            yf, yb = _rwkv_scan(r, v, kk, lw, kd, ag, n_ctx)
            vec = jnp.zeros((8, D), F32).at[0].set(rw_rk[j].reshape(-1)).at[1].set(rw_lnx_g[j]).at[2].set(rw_lnx_b[j])
            mixes = (_rwkv_readout(yf, yb, r, v, g, kd, vec, head_ones, last, n_ctx),)
            wo = _bf(rw_wo[j])
        xs = _post_mixer(xs, mod, mixes, wo, _bf(ffn_w1[l]), _bf(ffn_w3[l]), _bf(ffn_w2[l]), ln, last, n_ctx)
    return xs
```

```python
import functools
import math

import jax
import jax.numpy as jnp
import numpy as np
from jax import lax
from jax.experimental import pallas as pl
from jax.experimental.pallas import tpu as pltpu

F32 = jnp.float32
BF16 = jnp.bfloat16

D = 1024
DEPTH = 4
GRID_W = 64
ALPHA = (2.0 * DEPTH) ** 0.25
LN_EPS = 1e-6
ROPE_BASE = 10000.0
D_FF = 2816
FF_CHUNKS = (768, 768, 768, 512)
assert sum(FF_CHUNKS) == D_FF
DA_QK = 64
DA_HEADS = 4
RT_QK = 128
RT_HEADS = 4
RT_CHUNK = 128
Q_BLOCK = 128
ATT_ROWS = 768
EV_SEG = 512
RW_HEAD = 64
RW_GN_EPS = 64e-5
LORA_PAD = 128
GATE_PAD = 256
TM = 256
HALO = GRID_W
SCAN_L = 64
HEADS4_W = 4 * RW_HEAD
SCAN_W = 16 * RW_HEAD
PAIR_W = 2 * RW_HEAD
SCAN_ROUND = 4
VMEM_LIMIT = 56 * 1024 * 1024


def _dot(a, b):
    return jnp.dot(a, b, preferred_element_type=F32)


def _dot_nt(a, b):
    return lax.dot_general(a, b, (((1,), (1,)), ((), ())), preferred_element_type=F32)


def _dot_tn(a, b):
    return lax.dot_general(a, b, (((0,), (0,)), ((), ())), preferred_element_type=F32)


def _bf(x):
    return x.astype(BF16)


def _split2(x):
    hi = _bf(x)
    return hi, _bf(x - hi.astype(F32))


def _sigmoid(x):
    return 0.5 * jnp.tanh(0.5 * x) + 0.5


def _ln(x, g, b):
    mu = jnp.mean(x, -1, keepdims=True)
    xc = x - mu
    var = jnp.mean(xc * xc, -1, keepdims=True)
    return xc * lax.rsqrt(var + LN_EPS) * g + b


def _params(n_grid):
    return pltpu.CompilerParams(dimension_semantics=("arbitrary",) * n_grid,
                                vmem_limit_bytes=VMEM_LIMIT)


def _resident(shape):
    nd = len(shape)
    return pl.BlockSpec(shape, lambda *_: (0,) * nd, pipeline_mode=pl.Buffered(1))


def _mod_kernel(c_ref, w_ref, b_ref, o_ref):
    c = c_ref[...]
    act = _bf(c * _sigmoid(c))
    o_ref[...] = _dot(act, _bf(w_ref[...])) + b_ref[...]


def _adaln(cc, mod_w, mod_b):
    tn = 1536
    nt = mod_w.shape[2] // tn
    return pl.pallas_call(
        _mod_kernel,
        grid=(DEPTH, nt),
        in_specs=[pl.BlockSpec((16, D), lambda l, j: (0, 0)),
                  pl.BlockSpec((None, D, tn), lambda l, j: (l, 0, j)),
                  pl.BlockSpec((None, 1, tn), lambda l, j: (l, 0, j))],
        out_specs=pl.BlockSpec((None, 16, tn), lambda l, j: (l, 0, j)),
        out_shape=jax.ShapeDtypeStruct((DEPTH, 16, mod_w.shape[2]), F32),
        compiler_params=_params(2),
        name="adaln",
    )(cc, mod_w, mod_b.reshape(DEPTH, 1, -1))


def _evproj_kernel(x_ref, mod_ref, w_ref, ca_ref, sa_ref, cb_ref, sb_ref,
                   aq_ref, ak_ref, av_ref, bq_ref, bk_ref, bv_ref, bg_ref, h_ref):
    rows_all = x_ref.shape[0]
    for j in range(rows_all // TM):
        m = mod_ref[jnp.where(pl.program_id(1) == 0, 0, 1) if j == 0 else 1]
        rows = slice(j * TM, (j + 1) * TM)
        h_ref[rows, :] = _bf(x_ref[rows, :] * (1.0 + m[1:2, :]) + m[0:1, :])
    h = h_ref[...]
    ca, sa, cb, sb = ca_ref[...], sa_ref[...], cb_ref[...], sb_ref[...]
    lane = lax.broadcasted_iota(jnp.int32, (rows_all, 128), 1)
    first_half = (lane & (DA_QK // 2)) == 0

    def seg(j):
        return _dot(h, w_ref[:, j * EV_SEG:(j + 1) * EV_SEG])

    def rope_a(p, out_ref, scale):
        for j in range(EV_SEG // 128):
            blk = p[:, j * 128:(j + 1) * 128]
            sw = jnp.where(first_half, pltpu.roll(blk, 96, 1), pltpu.roll(blk, 32, 1))
            out_ref[:, j * 128:(j + 1) * 128] = _bf((blk * ca + sw * sa) * scale)

    def rope_b(p, out_ref):
        for j in range(EV_SEG // 128):
            blk = p[:, j * 128:(j + 1) * 128]
            out_ref[:, j * 128:(j + 1) * 128] = _bf(blk * cb + pltpu.roll(blk, 64, 1) * sb)

    rope_a(seg(0), aq_ref, DA_QK ** -0.5 * math.log2(math.e))
    rope_a(seg(1), ak_ref, 1.0)
    av_ref[...] = _bf(seg(2))
    rope_b(seg(3), bq_ref)
    rope_b(seg(4) * (RT_QK ** -0.5), bk_ref)
    bv_ref[...] = _bf(seg(5))
    bg_ref[...] = seg(6)


def _even_project(xs, mod, w_in, tabs):
    B, N, _ = xs.shape
    tm = TM * max(s for s in (3, 2, 1) if N % (s * TM) == 0)
    nt = N // tm
    tile = lambda w: pl.BlockSpec((None, tm, w), lambda b, i: (b, i, 0))
    tab = pl.BlockSpec((tm, 128), lambda b, i: (i, 0))
    outs = [jax.ShapeDtypeStruct((B, N, EV_SEG), BF16)] * 6 + [jax.ShapeDtypeStruct((B, N, EV_SEG), F32)]
    return pl.pallas_call(
        _evproj_kernel,
        grid=(B, nt),
        in_specs=[tile(D), pl.BlockSpec((None, 2, 6, D), lambda b, i: (b, 0, 0, 0)),
                  _resident(w_in.shape), tab, tab, tab, tab],
        out_specs=[tile(EV_SEG)] * 7,
        out_shape=outs,
        scratch_shapes=[pltpu.VMEM((tm, D), BF16)],
        compiler_params=_params(2),
        name="even_project",
    )(xs, mod, w_in, *tabs)


def _attn_block(q, k_ref, v_ref, nk, lam, gn):
    lane = lax.broadcasted_iota(jnp.int32, q.shape, 1)
    zero = jnp.zeros_like(q)
    qq = jnp.concatenate([jnp.where(lane < DA_QK, q, zero), jnp.where(lane >= DA_QK, q, zero)], axis=0)
    s = _dot_nt(qq, k_ref[0:nk, :])
    yield
    e = jnp.exp2(s - jnp.max(s, -1, keepdims=True))
    inv = 1.0 / jnp.sum(e, -1, keepdims=True)
    eb = _bf(e)
    yield
    ev = _dot(eb, v_ref[0:nk, :])
    o = ev[0:Q_BLOCK] * inv[0:Q_BLOCK] - ev[Q_BLOCK:2 * Q_BLOCK] * (lam * inv[Q_BLOCK:2 * Q_BLOCK])
    return o * lax.rsqrt(jnp.mean(o * o, -1, keepdims=True) + LN_EPS) * gn


def _attn_kernel(lam_ref, q_ref, k_ref, v_ref, gn_ref, o_ref, *, n_ctx, n_all):
    i = pl.program_id(2)
    lam = lam_ref[0]
    gn = gn_ref[...]
    subs = [slice(j * Q_BLOCK, (j + 1) * Q_BLOCK) for j in range(ATT_ROWS // Q_BLOCK)]
    n_ctx_subs = n_ctx // Q_BLOCK

    def run(first_tile):
        nks = [n_ctx if (first_tile and j < n_ctx_subs) else n_all for j in range(len(subs))]
        outs = _lockstep([_attn_block(q_ref[rows, :], k_ref, v_ref, nk, lam, gn) for rows, nk in zip(subs, nks)],
                         skew=1)
        for rows, o in zip(subs, outs):
            o_ref[rows, :] = o.astype(o_ref.dtype)

    @pl.when(i == 0)
    def _():
        run(True)

    @pl.when(i > 0)
    def _():
        run(False)


def _diff_attention(aq, ak, av, lam, gn, n_ctx):
    B, N, _ = aq.shape
    assert N % ATT_ROWS == 0 and n_ctx <= ATT_ROWS and n_ctx % Q_BLOCK == 0
    kv = pl.BlockSpec((None, N, 128), lambda b, h, i: (b, 0, h))
    qo = pl.BlockSpec((None, ATT_ROWS, 128), lambda b, h, i: (b, i, h))
    return pl.pallas_call(
        functools.partial(_attn_kernel, n_ctx=n_ctx, n_all=N),
        grid=(B, DA_HEADS, N // ATT_ROWS),
        in_specs=[pl.BlockSpec(memory_space=pltpu.SMEM), qo, kv, kv,
                  pl.BlockSpec((1, 128), lambda b, h, i: (0, h))],
        out_specs=qo,
        out_shape=jax.ShapeDtypeStruct((B, N, EV_SEG), BF16),
        compiler_params=_params(3),
        name="diff_attention",
    )(lam, aq, ak, av, gn)


def _ret_kernel(lg_ref, q_ref, k_ref, v_ref, g_ref, o_ref, sf_ref, sb_ref, accf_ref, accb_ref,
                *, n_chunks, n_ctx_chunks):
    C = RT_CHUNK
    ii = lax.broadcasted_iota(jnp.int32, (C, C), 0)
    jj = lax.broadcasted_iota(jnp.int32, (C, C), 1)
    idx = lax.broadcasted_iota(jnp.int32, (C, 1), 0).astype(F32)
    consts = {}
    for h in range(RT_HEADS):
        for d in range(2):
            lg = lg_ref[d, h]
            diff = ((ii - jj) if d == 0 else (jj - ii)).astype(F32)
            dec = jnp.where(diff >= 0, jnp.exp(lg * jnp.maximum(diff, 0.0)), 0.0)
            if d == 0:
                zeta = jnp.exp(lg * (C - 1.0 - idx))
                xi = jnp.exp(lg * (idx + 1.0))
            else:
                zeta = jnp.exp(lg * idx)
                xi = jnp.exp(lg * (C - idx))
            consts[h, d] = (dec, zeta, xi, jnp.exp(lg * jnp.full((1, 1), float(C), F32)))
    sf_ref[...] = jnp.zeros_like(sf_ref)
    sb_ref[...] = jnp.zeros_like(sb_ref)

    def chunk(s, h, d, s_ref, acc_ref):
        dec, zeta, xi, g_chunk = consts[h, d]
        cols = slice(h * RT_QK, (h + 1) * RT_QK)
        if d == 0:
            c = s
        else:
            c = jnp.where(s < n_ctx_chunks, n_ctx_chunks - 1 - s, n_chunks + n_ctx_chunks - 1 - s)
        rows = pl.ds(pl.multiple_of(c * C, C), C)
        qc = q_ref[rows, cols]
        kc = k_ref[rows, cols]
        vc = v_ref[rows, cols]
        state = s_ref[:, cols]
        inner = _dot_nt(qc, kc)
        cross = _dot(qc, _bf(state))
        kz = _bf(kc.astype(F32) * zeta)
        s_ref[:, cols] = g_chunk * state + _dot_tn(kz, vc)
        yield
        acc_ref[rows, cols] = _dot(_bf(inner * dec), vc) + cross * xi

    def step(s, carry):
        _lockstep([chunk(s, h, d, s_ref, acc_ref) for h in range(RT_HEADS)
                   for d, (s_ref, acc_ref) in enumerate(((sf_ref, accf_ref), (sb_ref, accb_ref)))])
        return carry

    lax.fori_loop(0, n_chunks, step, 0)
    for h in range(RT_HEADS):
        cols = slice(h * RT_QK, (h + 1) * RT_QK)
        b = accf_ref[:, cols] + accb_ref[:, cols]
        g = g_ref[:, cols]
        b = b * lax.rsqrt(jnp.mean(b * b, -1, keepdims=True) + LN_EPS)
        o_ref[:, cols] = (b * (g * _sigmoid(g))).astype(o_ref.dtype)


def _retention(bq, bk, bv, bg, log_gamma, n_ctx):
    B, N, W = bq.shape
    blk = pl.BlockSpec((None, N, W), lambda b: (b, 0, 0))
    return pl.pallas_call(
        functools.partial(_ret_kernel, n_chunks=N // RT_CHUNK, n_ctx_chunks=n_ctx // RT_CHUNK),
        grid=(B,),
        in_specs=[pl.BlockSpec(memory_space=pltpu.SMEM), blk, blk, blk, blk],
        out_specs=blk,
        out_shape=jax.ShapeDtypeStruct((B, N, W), BF16),
        scratch_shapes=[pltpu.VMEM((RT_QK, W), F32), pltpu.VMEM((RT_QK, W), F32),
                        pltpu.VMEM((N, W), F32), pltpu.VMEM((N, W), F32)],
        compiler_params=_params(1),
        name="retention",
    )(log_gamma, bq, bk, bv, bg)


def _post_kernel(*refs, mix_widths, n_sub, has_ctx):
    n_mix = len(mix_widths)
    x_ref, mod_ref = refs[0], refs[1]
    mix_refs = refs[2:2 + n_mix]
    wo_ref, w1_ref, w3_ref, w2_ref, ln_ref, o_ref, x1_ref, hm_ref = refs[2 + n_mix:]
    i = pl.program_id(1)
    o = None
    off = 0
    for m_ref, w in zip(mix_refs, mix_widths):
        part = _dot(m_ref[...], wo_ref[off:off + w, :])
        o = part if o is None else o + part
        off += w
    subs = [slice(j * TM, (j + 1) * TM) for j in range(n_sub)]
    mods = []
    for j, rows in enumerate(subs):
        seg = jnp.where(i == 0, 0, 1) if (has_ctx and j == 0) else 1
        m = mod_ref[seg]
        mods.append(m)
        x1 = _ln(ALPHA * x_ref[rows, :] + m[2:3, :] * o[rows, :], ln_ref[0:1, :], ln_ref[1:2, :])
        x1_ref[rows, :] = x1
        hm_ref[rows, :] = _bf(x1 * (1.0 + m[4:5, :]) + m[3:4, :])
    hm = hm_ref[...]
    f = None
    off = 0
    for w in FF_CHUNKS:
        cols = slice(off, off + w)
        u = _dot(hm, w1_ref[:, cols])
        t = _dot(hm, w3_ref[:, cols])
        part = _dot(_bf(u * _sigmoid(u) * t), w2_ref[cols, :])
        f = part if f is None else f + part
        off += w
    for m, rows in zip(mods, subs):
        o_ref[rows, :] = _ln(ALPHA * x1_ref[rows, :] + m[5:6, :] * f[rows, :], ln_ref[2:3, :], ln_ref[3:4, :])


def _post_mixer(xs, mod, mixes, wo, w1, w3, w2, ln, latent_only, n_ctx):
    B, N, _ = xs.shape
    rows_out = N - n_ctx if latent_only else N
    n_sub = 1 if latent_only else max(s for s in (3, 2, 1) if N % (s * TM) == 0)
    tm = n_sub * TM
    t0 = n_ctx // tm if latent_only else 0
    nt = rows_out // tm
    in_tile = lambda w, off=t0: pl.BlockSpec((None, tm, w), lambda b, i: (b, i + off, 0))
    widths = tuple(m.shape[-1] for m in mixes)
    mix_off = [t0 if m.shape[1] == N else 0 for m in mixes]
    return pl.pallas_call(
        functools.partial(_post_kernel, mix_widths=widths, n_sub=n_sub, has_ctx=not latent_only),
        grid=(B, nt),
        in_specs=[in_tile(D), pl.BlockSpec((None, 2, 6, D), lambda b, i: (b, 0, 0, 0))]
                 + [in_tile(w, off) for w, off in zip(widths, mix_off)]
                 + [_resident(wo.shape), _resident(w1.shape), _resident(w3.shape), _resident(w2.shape),
                    _resident(ln.shape)],
        out_specs=pl.BlockSpec((None, tm, D), lambda b, i: (b, i, 0)),
        out_shape=jax.ShapeDtypeStruct((B, rows_out, D), F32),
        scratch_shapes=[pltpu.VMEM((tm, D), F32), pltpu.VMEM((tm, D), BF16)],
        compiler_params=_params(2),
        name="post_mixer",
    )(xs, mod, *mixes, wo, w1, w3, w2, ln)


PV_MU, PV_W0, PV_A0, PV_KK, PV_KA0, PV_V0, PV_KA1 = 0, 6, 8, 10, 11, 12, 13


def _head_sum(x, e_ref):
    rows = x.shape[0]
    hi, lo = _split2(x)
    cols = []
    for j in range(x.shape[1] // HEADS4_W):
        sl = slice(j * HEADS4_W, (j + 1) * HEADS4_W)
        both = _dot(jnp.concatenate([hi[:, sl], lo[:, sl]], axis=0), e_ref[...])
        cols.append(both[0:rows] + both[rows:2 * rows])
    return jnp.concatenate(cols, axis=1)


def _rwproj_kernel(*refs, has_vres, n_tiles):
    if has_vres:
        (xp_ref, xc_ref, xn_ref, mod_ref, vf_ref, pv_ref, wr_ref, wk_ref, wv_ref, w1_ref, w2_ref,
         a1_ref, a2_ref, g1_ref, g2_ref, e_ref, v1_ref, v2_ref,
         r_ref, v_ref, g_ref, kk_ref, lw_ref, kd_ref, ag_ref, hbuf, hs) = refs
    else:
        (xp_ref, xc_ref, xn_ref, mod_ref, pv_ref, wr_ref, wk_ref, wv_ref, w1_ref, w2_ref,
         a1_ref, a2_ref, g1_ref, g2_ref, e_ref,
         r_ref, v_ref, g_ref, kk_ref, lw_ref, kd_ref, ag_ref, hbuf, hs) = refs
    i = pl.program_id(1)
    one_scale = 1.0 + mod_ref[1:2, :]
    shift = mod_ref[0:1, :]
    hbuf[0:HALO, :] = xp_ref[...] * one_scale + shift
    hbuf[HALO:HALO + TM, :] = xc_ref[...] * one_scale + shift
    hbuf[HALO + TM:, :] = xn_ref[...] * one_scale + shift
    row = lax.broadcasted_iota(jnp.int32, (TM, 1), 0)
    q = D // 4

    @pl.when(i == 0)
    def _():
        hs[:, 0:2 * q] = jnp.where(row == 0, 0.0, hbuf[HALO - 1:HALO - 1 + TM, 0:2 * q])
        hs[:, 2 * q:] = jnp.where(row == TM - 1, 0.0, hbuf[HALO + 1:HALO + 1 + TM, 2 * q:])

    @pl.when(i > 0)
    def _():
        col = row & (GRID_W - 1)
        hs[:, 0:q] = jnp.where(col == 0, 0.0, hbuf[HALO - 1:HALO - 1 + TM, 0:q])
        hs[:, q:2 * q] = jnp.where(col == GRID_W - 1, 0.0, hbuf[HALO + 1:HALO + 1 + TM, q:2 * q])
        top = jnp.logical_and(i == 1, row < GRID_W)
        hs[:, 2 * q:3 * q] = jnp.where(top, 0.0, hbuf[0:TM, 2 * q:3 * q])
        bottom = jnp.logical_and(i == n_tiles - 1, row >= TM - GRID_W)
        hs[:, 3 * q:] = jnp.where(bottom, 0.0, hbuf[2 * HALO:2 * HALO + TM, 3 * q:])

    h = hbuf[HALO:HALO + TM, :]
    xx = hs[...] - h
    mix = lambda j: _bf(h + xx * pv_ref[PV_MU + j:PV_MU + j + 1, :])
    xr, xw, xk, xv, xa, xg = (mix(j) for j in range(6))
    r = _dot(xr, wr_ref[...])
    k = _dot(xk, wk_ref[...])
    v = _dot(xv, wv_ref[...])
    if has_vres:
        gate = _sigmoid(pv_ref[PV_V0:PV_V0 + 1, :] + _dot(_bf(_dot(xv, v1_ref[...])), v2_ref[...]))
        v = v + (vf_ref[...].astype(F32) - v) * gate
    g = _dot(_bf(_sigmoid(_dot(xg, g1_ref[...]))), g2_ref[...])
    kx = k * pv_ref[PV_KK:PV_KK + 1, :]
    kkn = kx * lax.rsqrt(jnp.maximum(_head_sum(kx * kx, e_ref), 1e-24))
    r_ref[...] = _bf(r)
    v_ref[...] = _bf(v)
    g_ref[...] = _bf(g)
    kk_ref[...] = _bf(kkn)
    tw = jnp.tanh(_dot(xw, w1_ref[...]))
    ta = _dot(xa, a1_ref[...])
    for d in range(2):
        cols = slice(d * LORA_PAD, (d + 1) * LORA_PAD)
        half_w = pv_ref[PV_W0 + d:PV_W0 + d + 1, :] + _dot(_bf(tw[:, cols]), w2_ref[d])
        c = -0.5 * math.exp(-0.5)
        lw_ref[d] = c * jnp.tanh(half_w) + c
        th = jnp.tanh(pv_ref[PV_A0 + d:PV_A0 + d + 1, :] + _dot(_bf(ta[:, cols]), a2_ref[d]))
        ag_ref[d] = _bf(0.5 * th + 0.5)
        kd_ref[d] = _bf(k * (pv_ref[PV_KA0:PV_KA0 + 1, :] + pv_ref[PV_KA1:PV_KA1 + 1, :] * th))


def _rwkv_project(xs, mod, vf, p):
    B, N, _ = xs.shape
    nt = N // TM
    hp = TM // HALO
    n_halo = N // HALO
    tile = pl.BlockSpec((None, TM, D), lambda b, i: (b, i, 0))
    tile2 = pl.BlockSpec((2, None, TM, D), lambda b, i: (0, b, i, 0))
    in_specs = [pl.BlockSpec((None, HALO, D), lambda b, i: (b, jnp.maximum(i * hp - 1, 0), 0)),
                tile,
                pl.BlockSpec((None, HALO, D), lambda b, i: (b, jnp.minimum((i + 1) * hp, n_halo - 1), 0)),
                pl.BlockSpec((None, None, 6, D), lambda b, i: (b, jnp.minimum(i, 1), 0, 0))]
    args = [xs, xs, xs, mod]
    has_vres = vf is not None
    if has_vres:
        in_specs.append(tile)
        args.append(vf)
    names = ["pv", "wr", "wk", "wv", "w1", "w2", "a1", "a2", "g1", "g2", "e"] + (["v1", "v2"] if has_vres else [])
    for nm in names:
        in_specs.append(_resident(p[nm].shape))
        args.append(p[nm])
    one = jax.ShapeDtypeStruct((B, N, D), BF16)
    two = jax.ShapeDtypeStruct((2, B, N, D), BF16)
    return pl.pallas_call(
        functools.partial(_rwproj_kernel, has_vres=has_vres, n_tiles=nt),
        grid=(B, nt),
        in_specs=in_specs,
        out_specs=[tile, tile, tile, tile, tile2, tile2, tile2],
        out_shape=[one, one, one, one, jax.ShapeDtypeStruct((2, B, N, D), F32), two, two],
        scratch_shapes=[pltpu.VMEM((TM + 2 * HALO, D), F32), pltpu.VMEM((TM, D), F32)],
        compiler_params=_params(2),
        name="rwkv_project",
    )(*args)


def _bd(x, head):
    zero = jnp.zeros_like(x)
    return jnp.concatenate([jnp.where(head == hh, x, zero) for hh in range(PAIR_W // RW_HEAD)], axis=0)


def _diag_blocks(full, head):
    return jnp.where(head == 0, full[0:RW_HEAD], full[RW_HEAD:2 * RW_HEAD])


def _scan_chunk(r_ref, v_ref, kk_ref, lw_ref, kd_ref, ag_ref, rows, cols, reverse):
    L, W = SCAN_L, PAIR_W
    row = lax.broadcasted_iota(jnp.int32, (L, W), 0)
    lane = lax.broadcasted_iota(jnp.int32, (L, W), 1)
    pos = lane & (RW_HEAD - 1)
    head = lane >> 6
    strict = (pos > row) if reverse else (pos < row)
    incl = (pos >= row) if reverse else (pos <= row)
    ident = pos == row
    bd = lambda x: _bd(x, head)

    load = lambda ref: ref[rows, cols].astype(F32)
    lw = load(lw_ref)
    cum = lw
    step = 1
    while step < L:
        if reverse:
            cum = cum + jnp.where(row < L - step, pltpu.roll(cum, L - step, 0), 0.0)
        else:
            cum = cum + jnp.where(row >= step, pltpu.roll(cum, step, 0), 0.0)
        step *= 2
    gam = jnp.exp(cum)
    gam_inv = jnp.exp(-cum)
    kk = load(kk_ref)
    a_t = _bf(-kk * jnp.exp(cum - lw))
    r_t = load(r_ref) * gam
    b_t = kk * load(ag_ref) * gam_inv
    k_t = load(kd_ref) * gam_inv
    vb = v_ref[rows, cols]
    g_last = gam[0:1, :] if reverse else gam[L - 1:L, :]
    bd_v = bd(vb)
    g4 = _dot_nt(jnp.concatenate([a_t, _bf(r_t)], axis=0),
                 jnp.concatenate([bd(_bf(b_t)), bd(_bf(k_t))], axis=0))
    yield
    n_m = jnp.where(strict, g4[0:L, 0:W], 0.0)
    p_m = jnp.where(strict, g4[0:L, W:2 * W], 0.0)
    rb = _bf(jnp.where(incl, g4[L:2 * L, 0:W], 0.0))
    rk = _bf(jnp.where(incl, g4[L:2 * L, W:2 * W], 0.0))
    pv_rkv = _dot(jnp.concatenate([_bf(p_m), rk], axis=0), bd_v)
    pv = pv_rkv[0:L]
    t_m = jnp.where(ident, 1.0, 0.0)
    m = 1
    while m < L:
        later, earlier = (pos, row) if reverse else (row, pos)
        coupling = jnp.logical_and(jnp.logical_and((later & m) != 0, (earlier & m) == 0),
                                   (row // (2 * m)) == (pos // (2 * m)))
        n_off = jnp.where(coupling, n_m, 0.0)
        if m == 1:
            t_m = t_m + n_off
        else:
            t_b = _bf(t_m)
            half = _bf(_dot(t_b, bd(_bf(n_off))))
            yield
            t_m = t_m + _dot(half, bd(t_b))
            yield
        m *= 2
    t_m = _bf(t_m)
    tw = _dot(t_m, jnp.concatenate([bd(a_t), bd(_bf(pv))], axis=1))
    yield
    a_hat = _bf(tw[:, 0:W])
    w_b = _bf(tw[:, W:2 * W])
    rb_aw = _dot(rb, jnp.concatenate([bd(a_hat), bd(w_b)], axis=1))
    q_hat = _bf(r_t + rb_aw[:, 0:W])
    y_c = rb_aw[:, W:2 * W] + pv_rkv[L:2 * L]
    b_g = _bf(b_t * g_last)
    k_g = _bf(k_t * g_last)
    m_m = jnp.where(ident, g_last, 0.0) + _diag_blocks(_dot_tn(b_g, a_hat), head)
    c_m = _diag_blocks(_dot_tn(jnp.concatenate([b_g, k_g], axis=0), jnp.concatenate([w_b, vb], axis=0)), head)
    return q_hat, y_c, _bf(m_m), c_m


def _lockstep(gens, skew=0):
    results = [None] * len(gens)
    done = [False] * len(gens)
    t = 0
    while not all(done):
        for idx, g in enumerate(gens):
            if done[idx] or t < idx * skew:
                continue
            try:
                next(g)
            except StopIteration as stop:
                results[idx] = stop.value
                done[idx] = True
        t += 1
    return results


def _scan_kernel(rf_ref, vf_ref, kkf_ref, lwf_ref, kdf_ref, agf_ref,
                 rb_ref, vb_ref, kkb_ref, lwb_ref, kdb_ref, agb_ref,
                 yf_ref, yb_ref, hf_ref, hb_ref):
    @pl.when(pl.program_id(2) == 0)
    def _():
        hf_ref[...] = jnp.zeros_like(hf_ref)
        hb_ref[...] = jnp.zeros_like(hb_ref)

    L = SCAN_L
    n_chunks = TM // L
    fwd = (rf_ref, vf_ref, kkf_ref, lwf_ref, kdf_ref, agf_ref)
    bwd = (rb_ref, vb_ref, kkb_ref, lwb_ref, kdb_ref, agb_ref)
    f_rows = [slice(c * L, (c + 1) * L) for c in range(n_chunks)]
    b_rows = f_rows[::-1]
    pairs = [slice(p * PAIR_W, (p + 1) * PAIR_W) for p in range(SCAN_W // PAIR_W)]
    chains = ([(fwd, hf_ref, yf_ref, f_rows, cols, False) for cols in pairs]
              + [(bwd, hb_ref, yb_ref, b_rows, cols, True) for cols in pairs])
    head = lax.broadcasted_iota(jnp.int32, (L, PAIR_W), 1) >> 6
    for first in range(0, n_chunks, SCAN_ROUND):
        steps = range(first, first + SCAN_ROUND)
        jobs = [(chain, i) for i in steps for chain in chains]
        pre = _lockstep([_scan_chunk(*refs, rows[i], cols, rev) for (refs, _, _, rows, cols, rev), i in jobs])
        for ((_, h_ref, y_ref, rows, cols, _), i), (q_hat, y_c, m_b, c_m) in zip(jobs, pre):
            both = _dot(jnp.concatenate([q_hat, m_b], axis=0), _bd(_bf(h_ref[:, cols]), head))
            y_ref[rows[i], cols] = (both[0:L] + y_c).astype(y_ref.dtype)
            h_ref[:, cols] = both[L:2 * L] + c_m


def _rwkv_scan(r, v, kk, lw, kd, ag, n_ctx):
    B, N, _ = r.shape
    nt = N // TM
    nc = n_ctx // TM

    def back(s):
        return jnp.where(s < nc, nc - 1 - s, nt + nc - 1 - s)

    f_one = pl.BlockSpec((None, TM, SCAN_W), lambda b, g, s: (b, s, g))
    b_one = pl.BlockSpec((None, TM, SCAN_W), lambda b, g, s: (b, back(s), g))
    f_two = pl.BlockSpec((None, None, TM, SCAN_W), lambda b, g, s: (0, b, s, g))
    b_two = pl.BlockSpec((None, None, TM, SCAN_W), lambda b, g, s: (1, b, back(s), g))
    out = jax.ShapeDtypeStruct((B, N, D), BF16)
    return pl.pallas_call(
        _scan_kernel,
        grid=(B, D // SCAN_W, nt),
        in_specs=[f_one, f_one, f_one, f_two, f_two, f_two, b_one, b_one, b_one, b_two, b_two, b_two],
        out_specs=[f_one, b_one],
        out_shape=[out, out],
        scratch_shapes=[pltpu.VMEM((RW_HEAD, SCAN_W), F32), pltpu.VMEM((RW_HEAD, SCAN_W), F32)],
        compiler_params=_params(3),
        name="rwkv_scan",
    )(r, v, kk, lw, kd, ag, r, v, kk, lw, kd, ag)


def _rwread_kernel(yf_ref, yb_ref, r_ref, v_ref, g_ref, kd0_ref, kd1_ref, vec_ref, e_ref, o_ref):
    inv = 1.0 / RW_HEAD
    y = yf_ref[...].astype(F32) + yb_ref[...].astype(F32)
    mu = _head_sum(y, e_ref) * inv
    yc = y - mu
    var = _head_sum(yc * yc, e_ref) * inv
    yn = yc * lax.rsqrt(var + RW_GN_EPS) * vec_ref[1:2, :] + vec_ref[2:3, :]
    f32 = lambda ref: ref[...].astype(F32)
    bonus = _head_sum(f32(r_ref) * (f32(kd0_ref) + f32(kd1_ref)) * vec_ref[0:1, :], e_ref) * f32(v_ref)
    o_ref[...] = _bf((yn + bonus) * f32(g_ref))


def _rwkv_readout(yf, yb, r, v, g, kd, vec, e, latent_only, n_ctx):
    B, N, _ = r.shape
    tm = TM if latent_only else TM * max(s for s in (3, 2, 1) if N % (s * TM) == 0)
    t0 = n_ctx // tm if latent_only else 0
    nt = N // tm - t0
    one = pl.BlockSpec((None, tm, D), lambda b, i: (b, i + t0, 0))
    pick = lambda d: pl.BlockSpec((None, None, tm, D), lambda b, i: (d, b, i + t0, 0))
    return pl.pallas_call(
        _rwread_kernel,
        grid=(B, nt),
        in_specs=[one, one, one, one, one, pick(0), pick(1), _resident(vec.shape), _resident(e.shape)],
        out_specs=pl.BlockSpec((None, tm, D), lambda b, i: (b, i, 0)),
        out_shape=jax.ShapeDtypeStruct((B, nt * tm, D), BF16),
        compiler_params=_params(2),
        name="rwkv_readout",
    )(yf, yb, r, v, g, kd, kd, vec, e)


def _rope_tables(n_lat, n_ctx):
    t = jnp.arange(n_lat)
    rowp = (t // GRID_W).astype(F32)
    colp = (t % GRID_W).astype(F32)

    def table(dim):
        n_freq = dim // 4
        inv = ROPE_BASE ** (-jnp.arange(n_freq, dtype=F32) / n_freq)
        ang = jnp.concatenate([rowp[:, None] * inv, colp[:, None] * inv], -1)
        cos, sin = jnp.cos(ang), jnp.sin(ang)
        reps = 128 // dim
        cos_t = jnp.tile(jnp.concatenate([cos, cos], -1), (1, reps))
        sin_t = jnp.tile(jnp.concatenate([-sin, sin], -1), (1, reps))
        cos_t = jnp.concatenate([jnp.ones((n_ctx, 128), F32), cos_t], 0)
        sin_t = jnp.concatenate([jnp.zeros((n_ctx, 128), F32), sin_t], 0)
        return cos_t, sin_t

    ca, sa = table(DA_QK)
    cb, sb = table(RT_QK)
    return ca, sa, cb, sb


def _pad_cols(w, n):
    return jnp.pad(w, ((0, 0), (0, n - w.shape[1])))


def _pad_rows(w, n):
    return jnp.pad(w, ((0, n - w.shape[0]), (0, 0)))


def kernel(x, c, ctx, c_ctx, mod_w, mod_b, ln1_g, ln1_b, ln2_g, ln2_b, ffn_w1, ffn_w3, ffn_w2, ev_w_in, ev_w_out, da_lam_q1, da_lam_k1, da_lam_q2, da_lam_k2, da_gn_g, rt_decay_logit, rw_mu, rw_wr, rw_wk, rw_wv, rw_wo, rw_w0, rw_w1, rw_w2, rw_a0, rw_a1, rw_a2, rw_v0, rw_v1, rw_v2, rw_g1, rw_g2, rw_kk, rw_ka, rw_rk, rw_lnx_g, rw_lnx_b):
    B, T, _ = x.shape
    n_ctx = ctx.shape[1]
    assert n_ctx == TM and T % TM == 0 and x.shape[2] == D and B <= 15
    xs = jnp.concatenate([ctx, x], axis=1)

    cc = jnp.zeros((16, D), F32).at[:B].set(c).at[B].set(c_ctx)
    mod_all = _adaln(cc, mod_w, mod_b)
    m_lat = mod_all[:, :B].reshape(DEPTH, B, 1, 6, D)
    m_ctx = jnp.broadcast_to(mod_all[:, B].reshape(DEPTH, 1, 1, 6, D), (DEPTH, B, 1, 6, D))
    mod_tab = jnp.concatenate([m_ctx, m_lat], axis=2)

    tabs = _rope_tables(T, n_ctx)
    lane_head = np.arange(HEADS4_W) // RW_HEAD
    head_ones = jnp.asarray(lane_head[:, None] == lane_head[None, :], BF16)

    vf = None
    for l in range(DEPTH):
        last = l == DEPTH - 1
        mod = mod_tab[l]
        ln = jnp.stack([ln1_g[l], ln1_b[l], ln2_g[l], ln2_b[l]])
        if l % 2 == 0:
            e = l // 2
            lam_init = 0.8 - 0.6 * math.exp(-0.3 * l)
            lam = (jnp.exp(jnp.sum(da_lam_q1[e] * da_lam_k1[e])) - jnp.exp(jnp.sum(da_lam_q2[e] * da_lam_k2[e]))
                   + lam_init).reshape(1).astype(F32)
            gn = (da_gn_g[e] * (1.0 - lam_init)).reshape(1, -1)
            log_gamma = jnp.log(jax.nn.sigmoid(rt_decay_logit[e].astype(F32)))
            aq, ak, av, bq, bk, bv, bg = _even_project(xs, mod, _bf(ev_w_in[e]), tabs)
            a_mix = _diff_attention(aq, ak, av, lam, gn, n_ctx)
            b_mix = _retention(bq, bk, bv, bg, log_gamma, n_ctx)
            mixes = (a_mix, b_mix)
            wo = _bf(ev_w_out[e])
        else:
            j = l // 2
            has_vres = j > 0
            pvec = jnp.zeros((16, D), F32)
            pvec = pvec.at[PV_MU:PV_MU + 6].set(rw_mu[j]).at[PV_W0:PV_W0 + 2].set(0.5 * rw_w0[j])
            pvec = pvec.at[PV_A0:PV_A0 + 2].set(0.5 * rw_a0[j]).at[PV_KK].set(rw_kk[j])
            pvec = pvec.at[PV_KA0].set(1.0 - 0.5 * rw_ka[j]).at[PV_KA1].set(0.5 * rw_ka[j])
            p = {
                "wr": _bf(rw_wr[j]), "wk": _bf(rw_wk[j]), "wv": _bf(rw_wv[j]),
                "w1": _bf(jnp.concatenate([_pad_cols(rw_w1[j, d], LORA_PAD) for d in range(2)], 1)),
                "w2": _bf(jnp.stack([_pad_rows(0.5 * rw_w2[j, d], LORA_PAD) for d in range(2)])),
                "a1": _bf(jnp.concatenate([_pad_cols(rw_a1[j, d], LORA_PAD) for d in range(2)], 1)),
                "a2": _bf(jnp.stack([_pad_rows(0.5 * rw_a2[j, d], LORA_PAD) for d in range(2)])),
                "g1": _bf(_pad_cols(rw_g1[j], GATE_PAD)), "g2": _bf(_pad_rows(rw_g2[j], GATE_PAD)),
                "e": head_ones,
            }
            if has_vres:
                pvec = pvec.at[PV_V0].set(rw_v0[j - 1])
                p["v1"] = _bf(_pad_cols(rw_v1[j - 1], LORA_PAD))
                p["v2"] = _bf(_pad_rows(rw_v2[j - 1], LORA_PAD))
            p["pv"] = pvec
            r, v, g, kk, lw, kd, ag = _rwkv_project(xs, mod, vf if has_vres else None, p)
            if not has_vres:
                vf = v
            yf, yb = _rwkv_scan(r, v, kk, lw, kd, ag, n_ctx)
            vec = jnp.zeros((8, D), F32).at[0].set(rw_rk[j].reshape(-1)).at[1].set(rw_lnx_g[j]).at[2].set(rw_lnx_b[j])
            mixes = (_rwkv_readout(yf, yb, r, v, g, kd, vec, head_ones, last, n_ctx),)
            wo = _bf(rw_wo[j])
        xs = _post_mixer(xs, mod, mixes, wo, _bf(ffn_w1[l]), _bf(ffn_w3[l]), _bf(ffn_w2[l]), ln, last, n_ctx)
    return xs
```

```python
import functools
import math

import jax
import jax.numpy as jnp
import numpy as np
from jax import lax
from jax.experimental import pallas as pl
from jax.experimental.pallas import tpu as pltpu

F32 = jnp.float32
BF16 = jnp.bfloat16

D = 1024
DEPTH = 4
GRID_W = 64
ALPHA = (2.0 * DEPTH) ** 0.25
LN_EPS = 1e-6
ROPE_BASE = 10000.0
D_FF = 2816
FF_CHUNKS = (768, 768, 768, 512)
assert sum(FF_CHUNKS) == D_FF
DA_QK = 64
DA_HEADS = 4
RT_QK = 128
RT_HEADS = 4
RT_CHUNK = 128
Q_BLOCK = 128
ATT_ROWS = 768
EV_SEG = 512
RW_HEAD = 64
RW_GN_EPS = 64e-5
LORA_PAD = 128
GATE_PAD = 256
TM = 256
HALO = GRID_W
SCAN_L = 64
HEADS4_W = 4 * RW_HEAD
SCAN_W = 16 * RW_HEAD
PAIR_W = 2 * RW_HEAD
SCAN_ROUND = 4
VMEM_LIMIT = 56 * 1024 * 1024


def _dot(a, b):
    return jnp.dot(a, b, preferred_element_type=F32)


def _dot_nt(a, b):
    return lax.dot_general(a, b, (((1,), (1,)), ((), ())), preferred_element_type=F32)


def _dot_tn(a, b):
    return lax.dot_general(a, b, (((0,), (0,)), ((), ())), preferred_element_type=F32)


def _bf(x):
    return x.astype(BF16)


def _split2(x):
    hi = _bf(x)
    return hi, _bf(x - hi.astype(F32))


def _sigmoid(x):
    return 0.5 * jnp.tanh(0.5 * x) + 0.5


def _ln(x, g, b):
    mu = jnp.mean(x, -1, keepdims=True)
    xc = x - mu
    var = jnp.mean(xc * xc, -1, keepdims=True)
    return xc * lax.rsqrt(var + LN_EPS) * g + b


def _params(n_grid):
    return pltpu.CompilerParams(dimension_semantics=("parallel",) + ("arbitrary",) * (n_grid - 1),
                                vmem_limit_bytes=VMEM_LIMIT)


def _resident(shape):
    nd = len(shape)
    return pl.BlockSpec(shape, lambda *_: (0,) * nd, pipeline_mode=pl.Buffered(1))


def _mod_kernel(c_ref, w_ref, b_ref, o_ref):
    c = c_ref[...]
    act = _bf(c * _sigmoid(c))
    o_ref[...] = _dot(act, _bf(w_ref[...])) + b_ref[...]


def _adaln(cc, mod_w, mod_b):
    tn = 1536
    nt = mod_w.shape[2] // tn
    return pl.pallas_call(
        _mod_kernel,
        grid=(DEPTH, nt),
        in_specs=[pl.BlockSpec((16, D), lambda l, j: (0, 0)),
                  pl.BlockSpec((None, D, tn), lambda l, j: (l, 0, j)),
                  pl.BlockSpec((None, 1, tn), lambda l, j: (l, 0, j))],
        out_specs=pl.BlockSpec((None, 16, tn), lambda l, j: (l, 0, j)),
        out_shape=jax.ShapeDtypeStruct((DEPTH, 16, mod_w.shape[2]), F32),
        compiler_params=_params(2),
        name="adaln",
    )(cc, mod_w, mod_b.reshape(DEPTH, 1, -1))


def _evproj_kernel(x_ref, mod_ref, w_ref, ca_ref, sa_ref, cb_ref, sb_ref,
                   aq_ref, ak_ref, av_ref, bq_ref, bk_ref, bv_ref, bg_ref, h_ref):
    rows_all = x_ref.shape[0]
    for j in range(rows_all // TM):
        m = mod_ref[jnp.where(pl.program_id(1) == 0, 0, 1) if j == 0 else 1]
        rows = slice(j * TM, (j + 1) * TM)
        h_ref[rows, :] = _bf(x_ref[rows, :] * (1.0 + m[1:2, :]) + m[0:1, :])
    h = h_ref[...]
    ca, sa, cb, sb = ca_ref[...], sa_ref[...], cb_ref[...], sb_ref[...]
    lane = lax.broadcasted_iota(jnp.int32, (rows_all, 128), 1)
    first_half = (lane & (DA_QK // 2)) == 0

    def seg(j):
        return _dot(h, w_ref[:, j * EV_SEG:(j + 1) * EV_SEG])

    def rope_a(p, out_ref, scale):
        for j in range(EV_SEG // 128):
            blk = p[:, j * 128:(j + 1) * 128]
            sw = jnp.where(first_half, pltpu.roll(blk, 96, 1), pltpu.roll(blk, 32, 1))
            out_ref[:, j * 128:(j + 1) * 128] = _bf((blk * ca + sw * sa) * scale)

    def rope_b(p, out_ref):
        for j in range(EV_SEG // 128):
            blk = p[:, j * 128:(j + 1) * 128]
            out_ref[:, j * 128:(j + 1) * 128] = _bf(blk * cb + pltpu.roll(blk, 64, 1) * sb)

    rope_a(seg(0), aq_ref, DA_QK ** -0.5 * math.log2(math.e))
    rope_a(seg(1), ak_ref, 1.0)
    av_ref[...] = _bf(seg(2))
    rope_b(seg(3), bq_ref)
    rope_b(seg(4) * (RT_QK ** -0.5), bk_ref)
    bv_ref[...] = _bf(seg(5))
    bg_ref[...] = seg(6)


def _even_project(xs, mod, w_in, tabs):
    B, N, _ = xs.shape
    tm = TM * max(s for s in (3, 2, 1) if N % (s * TM) == 0)
    nt = N // tm
    tile = lambda w: pl.BlockSpec((None, tm, w), lambda b, i: (b, i, 0))
    tab = pl.BlockSpec((tm, 128), lambda b, i: (i, 0))
    outs = [jax.ShapeDtypeStruct((B, N, EV_SEG), BF16)] * 6 + [jax.ShapeDtypeStruct((B, N, EV_SEG), F32)]
    return pl.pallas_call(
        _evproj_kernel,
        grid=(B, nt),
        in_specs=[tile(D), pl.BlockSpec((None, 2, 6, D), lambda b, i: (b, 0, 0, 0)),
                  _resident(w_in.shape), tab, tab, tab, tab],
        out_specs=[tile(EV_SEG)] * 7,
        out_shape=outs,
        scratch_shapes=[pltpu.VMEM((tm, D), BF16)],
        compiler_params=_params(2),
        name="even_project",
    )(xs, mod, w_in, *tabs)


def _attn_block(q, k_ref, v_ref, nk, lam, gn):
    lane = lax.broadcasted_iota(jnp.int32, q.shape, 1)
    zero = jnp.zeros_like(q)
    qq = jnp.concatenate([jnp.where(lane < DA_QK, q, zero), jnp.where(lane >= DA_QK, q, zero)], axis=0)
    s = _dot_nt(qq, k_ref[0:nk, :])
    yield
    e = jnp.exp2(s - jnp.max(s, -1, keepdims=True))
    inv = 1.0 / jnp.sum(e, -1, keepdims=True)
    eb = _bf(e)
    yield
    ev = _dot(eb, v_ref[0:nk, :])
    o = ev[0:Q_BLOCK] * inv[0:Q_BLOCK] - ev[Q_BLOCK:2 * Q_BLOCK] * (lam * inv[Q_BLOCK:2 * Q_BLOCK])
    return o * lax.rsqrt(jnp.mean(o * o, -1, keepdims=True) + LN_EPS) * gn


def _attn_kernel(lam_ref, q_ref, k_ref, v_ref, gn_ref, o_ref, *, n_ctx, n_all):
    i = pl.program_id(2)
    lam = lam_ref[0]
    gn = gn_ref[...]
    subs = [slice(j * Q_BLOCK, (j + 1) * Q_BLOCK) for j in range(ATT_ROWS // Q_BLOCK)]
    n_ctx_subs = n_ctx // Q_BLOCK

    def run(first_tile):
        nks = [n_ctx if (first_tile and j < n_ctx_subs) else n_all for j in range(len(subs))]
        outs = _lockstep([_attn_block(q_ref[rows, :], k_ref, v_ref, nk, lam, gn) for rows, nk in zip(subs, nks)],
                         skew=1)
        for rows, o in zip(subs, outs):
            o_ref[rows, :] = o.astype(o_ref.dtype)

    @pl.when(i == 0)
    def _():
        run(True)

    @pl.when(i > 0)
    def _():
        run(False)


def _diff_attention(aq, ak, av, lam, gn, n_ctx):
    B, N, _ = aq.shape
    assert N % ATT_ROWS == 0 and n_ctx <= ATT_ROWS and n_ctx % Q_BLOCK == 0
    kv = pl.BlockSpec((None, N, 128), lambda b, h, i: (b, 0, h))
    qo = pl.BlockSpec((None, ATT_ROWS, 128), lambda b, h, i: (b, i, h))
    return pl.pallas_call(
        functools.partial(_attn_kernel, n_ctx=n_ctx, n_all=N),
        grid=(B, DA_HEADS, N // ATT_ROWS),
        in_specs=[pl.BlockSpec(memory_space=pltpu.SMEM), qo, kv, kv,
                  pl.BlockSpec((1, 128), lambda b, h, i: (0, h))],
        out_specs=qo,
        out_shape=jax.ShapeDtypeStruct((B, N, EV_SEG), BF16),
        compiler_params=_params(3),
        name="diff_attention",
    )(lam, aq, ak, av, gn)


def _ret_kernel(lg_ref, q_ref, k_ref, v_ref, g_ref, o_ref, sf_ref, sb_ref, accf_ref, accb_ref,
                *, n_chunks, n_ctx_chunks):
    C = RT_CHUNK
    ii = lax.broadcasted_iota(jnp.int32, (C, C), 0)
    jj = lax.broadcasted_iota(jnp.int32, (C, C), 1)
    idx = lax.broadcasted_iota(jnp.int32, (C, 1), 0).astype(F32)
    consts = {}
    for h in range(RT_HEADS):
        for d in range(2):
            lg = lg_ref[d, h]
            diff = ((ii - jj) if d == 0 else (jj - ii)).astype(F32)
            dec = jnp.where(diff >= 0, jnp.exp(lg * jnp.maximum(diff, 0.0)), 0.0)
            if d == 0:
                zeta = jnp.exp(lg * (C - 1.0 - idx))
                xi = jnp.exp(lg * (idx + 1.0))
            else:
                zeta = jnp.exp(lg * idx)
                xi = jnp.exp(lg * (C - idx))
            consts[h, d] = (dec, zeta, xi, jnp.exp(lg * jnp.full((1, 1), float(C), F32)))
    sf_ref[...] = jnp.zeros_like(sf_ref)
    sb_ref[...] = jnp.zeros_like(sb_ref)

    def chunk(s, h, d, s_ref, acc_ref):
        dec, zeta, xi, g_chunk = consts[h, d]
        cols = slice(h * RT_QK, (h + 1) * RT_QK)
        if d == 0:
            c = s
        else:
            c = jnp.where(s < n_ctx_chunks, n_ctx_chunks - 1 - s, n_chunks + n_ctx_chunks - 1 - s)
        rows = pl.ds(pl.multiple_of(c * C, C), C)
        qc = q_ref[rows, cols]
        kc = k_ref[rows, cols]
        vc = v_ref[rows, cols]
        state = s_ref[:, cols]
        inner = _dot_nt(qc, kc)
        cross = _dot(qc, _bf(state))
        kz = _bf(kc.astype(F32) * zeta)
        s_ref[:, cols] = g_chunk * state + _dot_tn(kz, vc)
        yield
        acc_ref[rows, cols] = _dot(_bf(inner * dec), vc) + cross * xi

    def step(s, carry):
        _lockstep([chunk(s, h, d, s_ref, acc_ref) for h in range(RT_HEADS)
                   for d, (s_ref, acc_ref) in enumerate(((sf_ref, accf_ref), (sb_ref, accb_ref)))])
        return carry

    lax.fori_loop(0, n_chunks, step, 0)
    for h in range(RT_HEADS):
        cols = slice(h * RT_QK, (h + 1) * RT_QK)
        b = accf_ref[:, cols] + accb_ref[:, cols]
        g = g_ref[:, cols]
        b = b * lax.rsqrt(jnp.mean(b * b, -1, keepdims=True) + LN_EPS)
        o_ref[:, cols] = (b * (g * _sigmoid(g))).astype(o_ref.dtype)


def _retention(bq, bk, bv, bg, log_gamma, n_ctx):
    B, N, W = bq.shape
    blk = pl.BlockSpec((None, N, W), lambda b: (b, 0, 0))
    return pl.pallas_call(
        functools.partial(_ret_kernel, n_chunks=N // RT_CHUNK, n_ctx_chunks=n_ctx // RT_CHUNK),
        grid=(B,),
        in_specs=[pl.BlockSpec(memory_space=pltpu.SMEM), blk, blk, blk, blk],
        out_specs=blk,
        out_shape=jax.ShapeDtypeStruct((B, N, W), BF16),
        scratch_shapes=[pltpu.VMEM((RT_QK, W), F32), pltpu.VMEM((RT_QK, W), F32),
                        pltpu.VMEM((N, W), F32), pltpu.VMEM((N, W), F32)],
        compiler_params=_params(1),
        name="retention",
    )(log_gamma, bq, bk, bv, bg)


def _post_kernel(*refs, mix_widths, n_sub, has_ctx):
    n_mix = len(mix_widths)
    x_ref, mod_ref = refs[0], refs[1]
    mix_refs = refs[2:2 + n_mix]
    wo_ref, w1_ref, w3_ref, w2_ref, ln_ref, o_ref, x1_ref, hm_ref = refs[2 + n_mix:]
    i = pl.program_id(1)
    o = None
    off = 0
    for m_ref, w in zip(mix_refs, mix_widths):
        part = _dot(m_ref[...], wo_ref[off:off + w, :])
        o = part if o is None else o + part
        off += w
    subs = [slice(j * TM, (j + 1) * TM) for j in range(n_sub)]
    mods = []
    for j, rows in enumerate(subs):
        seg = jnp.where(i == 0, 0, 1) if (has_ctx and j == 0) else 1
        m = mod_ref[seg]
        mods.append(m)
        x1 = _ln(ALPHA * x_ref[rows, :] + m[2:3, :] * o[rows, :], ln_ref[0:1, :], ln_ref[1:2, :])
        x1_ref[rows, :] = x1
        hm_ref[rows, :] = _bf(x1 * (1.0 + m[4:5, :]) + m[3:4, :])
    hm = hm_ref[...]
    f = None
    off = 0
    for w in FF_CHUNKS:
        cols = slice(off, off + w)
        u = _dot(hm, w1_ref[:, cols])
        t = _dot(hm, w3_ref[:, cols])
        part = _dot(_bf(u * _sigmoid(u) * t), w2_ref[cols, :])
        f = part if f is None else f + part
        off += w
    for m, rows in zip(mods, subs):
        o_ref[rows, :] = _ln(ALPHA * x1_ref[rows, :] + m[5:6, :] * f[rows, :], ln_ref[2:3, :], ln_ref[3:4, :])


def _post_mixer(xs, mod, mixes, wo, w1, w3, w2, ln, latent_only, n_ctx):
    B, N, _ = xs.shape
    rows_out = N - n_ctx if latent_only else N
    n_sub = 1 if latent_only else max(s for s in (3, 2, 1) if N % (s * TM) == 0)
    tm = n_sub * TM
    t0 = n_ctx // tm if latent_only else 0
    nt = rows_out // tm
    in_tile = lambda w, off=t0: pl.BlockSpec((None, tm, w), lambda b, i: (b, i + off, 0))
    widths = tuple(m.shape[-1] for m in mixes)
    mix_off = [t0 if m.shape[1] == N else 0 for m in mixes]
    return pl.pallas_call(
        functools.partial(_post_kernel, mix_widths=widths, n_sub=n_sub, has_ctx=not latent_only),
        grid=(B, nt),
        in_specs=[in_tile(D), pl.BlockSpec((None, 2, 6, D), lambda b, i: (b, 0, 0, 0))]
                 + [in_tile(w, off) for w, off in zip(widths, mix_off)]
                 + [_resident(wo.shape), _resident(w1.shape), _resident(w3.shape), _resident(w2.shape),
                    _resident(ln.shape)],
        out_specs=pl.BlockSpec((None, tm, D), lambda b, i: (b, i, 0)),
        out_shape=jax.ShapeDtypeStruct((B, rows_out, D), F32),
        scratch_shapes=[pltpu.VMEM((tm, D), F32), pltpu.VMEM((tm, D), BF16)],
        compiler_params=_params(2),
        name="post_mixer",
    )(xs, mod, *mixes, wo, w1, w3, w2, ln)


PV_MU, PV_W0, PV_A0, PV_KK, PV_KA0, PV_V0, PV_KA1 = 0, 6, 8, 10, 11, 12, 13


def _head_sum(x, e_ref):
    rows = x.shape[0]
    hi, lo = _split2(x)
    cols = []
    for j in range(x.shape[1] // HEADS4_W):
        sl = slice(j * HEADS4_W, (j + 1) * HEADS4_W)
        both = _dot(jnp.concatenate([hi[:, sl], lo[:, sl]], axis=0), e_ref[...])
        cols.append(both[0:rows] + both[rows:2 * rows])
    return jnp.concatenate(cols, axis=1)


def _rwproj_kernel(*refs, has_vres, n_tiles):
    if has_vres:
        (xp_ref, xc_ref, xn_ref, mod_ref, vf_ref, pv_ref, wr_ref, wk_ref, wv_ref, w1_ref, w2_ref,
         a1_ref, a2_ref, g1_ref, g2_ref, e_ref, v1_ref, v2_ref,
         r_ref, v_ref, g_ref, kk_ref, lw_ref, kd_ref, ag_ref, hbuf, hs) = refs
    else:
        (xp_ref, xc_ref, xn_ref, mod_ref, pv_ref, wr_ref, wk_ref, wv_ref, w1_ref, w2_ref,
         a1_ref, a2_ref, g1_ref, g2_ref, e_ref,
         r_ref, v_ref, g_ref, kk_ref, lw_ref, kd_ref, ag_ref, hbuf, hs) = refs
    i = pl.program_id(1)
    one_scale = 1.0 + mod_ref[1:2, :]
    shift = mod_ref[0:1, :]
    hbuf[0:HALO, :] = xp_ref[...] * one_scale + shift
    hbuf[HALO:HALO + TM, :] = xc_ref[...] * one_scale + shift
    hbuf[HALO + TM:, :] = xn_ref[...] * one_scale + shift
    row = lax.broadcasted_iota(jnp.int32, (TM, 1), 0)
    q = D // 4

    @pl.when(i == 0)
    def _():
        hs[:, 0:2 * q] = jnp.where(row == 0, 0.0, hbuf[HALO - 1:HALO - 1 + TM, 0:2 * q])
        hs[:, 2 * q:] = jnp.where(row == TM - 1, 0.0, hbuf[HALO + 1:HALO + 1 + TM, 2 * q:])

    @pl.when(i > 0)
    def _():
        col = row & (GRID_W - 1)
        hs[:, 0:q] = jnp.where(col == 0, 0.0, hbuf[HALO - 1:HALO - 1 + TM, 0:q])
        hs[:, q:2 * q] = jnp.where(col == GRID_W - 1, 0.0, hbuf[HALO + 1:HALO + 1 + TM, q:2 * q])
        top = jnp.logical_and(i == 1, row < GRID_W)
        hs[:, 2 * q:3 * q] = jnp.where(top, 0.0, hbuf[0:TM, 2 * q:3 * q])
        bottom = jnp.logical_and(i == n_tiles - 1, row >= TM - GRID_W)
        hs[:, 3 * q:] = jnp.where(bottom, 0.0, hbuf[2 * HALO:2 * HALO + TM, 3 * q:])

    h = hbuf[HALO:HALO + TM, :]
    xx = hs[...] - h
    mix = lambda j: _bf(h + xx * pv_ref[PV_MU + j:PV_MU + j + 1, :])
    xr, xw, xk, xv, xa, xg = (mix(j) for j in range(6))
    r = _dot(xr, wr_ref[...])
    k = _dot(xk, wk_ref[...])
    v = _dot(xv, wv_ref[...])
    if has_vres:
        gate = _sigmoid(pv_ref[PV_V0:PV_V0 + 1, :] + _dot(_bf(_dot(xv, v1_ref[...])), v2_ref[...]))
        v = v + (vf_ref[...].astype(F32) - v) * gate
    g = _dot(_bf(_sigmoid(_dot(xg, g1_ref[...]))), g2_ref[...])
    kx = k * pv_ref[PV_KK:PV_KK + 1, :]
    kkn = kx * lax.rsqrt(jnp.maximum(_head_sum(kx * kx, e_ref), 1e-24))
    r_ref[...] = _bf(r)
    v_ref[...] = _bf(v)
    g_ref[...] = _bf(g)
    kk_ref[...] = _bf(kkn)
    tw = jnp.tanh(_dot(xw, w1_ref[...]))
    ta = _dot(xa, a1_ref[...])
    for d in range(2):
        cols = slice(d * LORA_PAD, (d + 1) * LORA_PAD)
        half_w = pv_ref[PV_W0 + d:PV_W0 + d + 1, :] + _dot(_bf(tw[:, cols]), w2_ref[d])
        c = -0.5 * math.exp(-0.5)
        lw_ref[d] = c * jnp.tanh(half_w) + c
        th = jnp.tanh(pv_ref[PV_A0 + d:PV_A0 + d + 1, :] + _dot(_bf(ta[:, cols]), a2_ref[d]))
        ag_ref[d] = _bf(0.5 * th + 0.5)
        kd_ref[d] = _bf(k * (pv_ref[PV_KA0:PV_KA0 + 1, :] + pv_ref[PV_KA1:PV_KA1 + 1, :] * th))


def _rwkv_project(xs, mod, vf, p):
    B, N, _ = xs.shape
    nt = N // TM
    hp = TM // HALO
    n_halo = N // HALO
    tile = pl.BlockSpec((None, TM, D), lambda b, i: (b, i, 0))
    tile2 = pl.BlockSpec((2, None, TM, D), lambda b, i: (0, b, i, 0))
    in_specs = [pl.BlockSpec((None, HALO, D), lambda b, i: (b, jnp.maximum(i * hp - 1, 0), 0)),
                tile,
                pl.BlockSpec((None, HALO, D), lambda b, i: (b, jnp.minimum((i + 1) * hp, n_halo - 1), 0)),
                pl.BlockSpec((None, None, 6, D), lambda b, i: (b, jnp.minimum(i, 1), 0, 0))]
    args = [xs, xs, xs, mod]
    has_vres = vf is not None
    if has_vres:
        in_specs.append(tile)
        args.append(vf)
    names = ["pv", "wr", "wk", "wv", "w1", "w2", "a1", "a2", "g1", "g2", "e"] + (["v1", "v2"] if has_vres else [])
    for nm in names:
        in_specs.append(_resident(p[nm].shape))
        args.append(p[nm])
    one = jax.ShapeDtypeStruct((B, N, D), BF16)
    two = jax.ShapeDtypeStruct((2, B, N, D), BF16)
    return pl.pallas_call(
        functools.partial(_rwproj_kernel, has_vres=has_vres, n_tiles=nt),
        grid=(B, nt),
        in_specs=in_specs,
        out_specs=[tile, tile, tile, tile, tile2, tile2, tile2],
        out_shape=[one, one, one, one, jax.ShapeDtypeStruct((2, B, N, D), F32), two, two],
        scratch_shapes=[pltpu.VMEM((TM + 2 * HALO, D), F32), pltpu.VMEM((TM, D), F32)],
        compiler_params=_params(2),
        name="rwkv_project",
    )(*args)


def _bd(x, head):
    zero = jnp.zeros_like(x)
    return jnp.concatenate([jnp.where(head == hh, x, zero) for hh in range(PAIR_W // RW_HEAD)], axis=0)


def _diag_blocks(full, head):
    return jnp.where(head == 0, full[0:RW_HEAD], full[RW_HEAD:2 * RW_HEAD])


def _scan_chunk(r_ref, v_ref, kk_ref, lw_ref, kd_ref, ag_ref, rows, cols, reverse):
    L, W = SCAN_L, PAIR_W
    row = lax.broadcasted_iota(jnp.int32, (L, W), 0)
    lane = lax.broadcasted_iota(jnp.int32, (L, W), 1)
    pos = lane & (RW_HEAD - 1)
    head = lane >> 6
    strict = (pos > row) if reverse else (pos < row)
    incl = (pos >= row) if reverse else (pos <= row)
    ident = pos == row
    bd = lambda x: _bd(x, head)

    load = lambda ref: ref[rows, cols].astype(F32)
    lw = load(lw_ref)
    cum = lw
    step = 1
    while step < L:
        if reverse:
            cum = cum + jnp.where(row < L - step, pltpu.roll(cum, L - step, 0), 0.0)
        else:
            cum = cum + jnp.where(row >= step, pltpu.roll(cum, step, 0), 0.0)
        step *= 2
    gam = jnp.exp(cum)
    gam_inv = jnp.exp(-cum)
    kk = load(kk_ref)
    a_t = _bf(-kk * jnp.exp(cum - lw))
    r_t = load(r_ref) * gam
    b_t = kk * load(ag_ref) * gam_inv
    k_t = load(kd_ref) * gam_inv
    vb = v_ref[rows, cols]
    g_last = gam[0:1, :] if reverse else gam[L - 1:L, :]
    bd_v = bd(vb)
    g4 = _dot_nt(jnp.concatenate([a_t, _bf(r_t)], axis=0),
                 jnp.concatenate([bd(_bf(b_t)), bd(_bf(k_t))], axis=0))
    yield
    n_m = jnp.where(strict, g4[0:L, 0:W], 0.0)
    p_m = jnp.where(strict, g4[0:L, W:2 * W], 0.0)
    rb = _bf(jnp.where(incl, g4[L:2 * L, 0:W], 0.0))
    rk = _bf(jnp.where(incl, g4[L:2 * L, W:2 * W], 0.0))
    pv_rkv = _dot(jnp.concatenate([_bf(p_m), rk], axis=0), bd_v)
    pv = pv_rkv[0:L]
    t_m = jnp.where(ident, 1.0, 0.0)
    m = 1
    while m < L:
        later, earlier = (pos, row) if reverse else (row, pos)
        coupling = jnp.logical_and(jnp.logical_and((later & m) != 0, (earlier & m) == 0),
                                   (row // (2 * m)) == (pos // (2 * m)))
        n_off = jnp.where(coupling, n_m, 0.0)
        if m == 1:
            t_m = t_m + n_off
        else:
            t_b = _bf(t_m)
            half = _bf(_dot(t_b, bd(_bf(n_off))))
            yield
            t_m = t_m + _dot(half, bd(t_b))
            yield
        m *= 2
    t_m = _bf(t_m)
    tw = _dot(t_m, jnp.concatenate([bd(a_t), bd(_bf(pv))], axis=1))
    yield
    a_hat = _bf(tw[:, 0:W])
    w_b = _bf(tw[:, W:2 * W])
    rb_aw = _dot(rb, jnp.concatenate([bd(a_hat), bd(w_b)], axis=1))
    q_hat = _bf(r_t + rb_aw[:, 0:W])
    y_c = rb_aw[:, W:2 * W] + pv_rkv[L:2 * L]
    b_g = _bf(b_t * g_last)
    k_g = _bf(k_t * g_last)
    m_m = jnp.where(ident, g_last, 0.0) + _diag_blocks(_dot_tn(b_g, a_hat), head)
    c_m = _diag_blocks(_dot_tn(jnp.concatenate([b_g, k_g], axis=0), jnp.concatenate([w_b, vb], axis=0)), head)
    return q_hat, y_c, _bf(m_m), c_m


def _lockstep(gens, skew=0):
    results = [None] * len(gens)
    done = [False] * len(gens)
    t = 0
    while not all(done):
        for idx, g in enumerate(gens):
            if done[idx] or t < idx * skew:
                continue
            try:
                next(g)
            except StopIteration as stop:
                results[idx] = stop.value
                done[idx] = True
        t += 1
    return results


def _scan_kernel(rf_ref, vf_ref, kkf_ref, lwf_ref, kdf_ref, agf_ref,
                 rb_ref, vb_ref, kkb_ref, lwb_ref, kdb_ref, agb_ref,
                 yf_ref, yb_ref, hf_ref, hb_ref):
    @pl.when(pl.program_id(2) == 0)
    def _():
        hf_ref[...] = jnp.zeros_like(hf_ref)
        hb_ref[...] = jnp.zeros_like(hb_ref)

    L = SCAN_L
    n_chunks = TM // L
    fwd = (rf_ref, vf_ref, kkf_ref, lwf_ref, kdf_ref, agf_ref)
    bwd = (rb_ref, vb_ref, kkb_ref, lwb_ref, kdb_ref, agb_ref)
    f_rows = [slice(c * L, (c + 1) * L) for c in range(n_chunks)]
    b_rows = f_rows[::-1]
    pairs = [slice(p * PAIR_W, (p + 1) * PAIR_W) for p in range(SCAN_W // PAIR_W)]
    chains = ([(fwd, hf_ref, yf_ref, f_rows, cols, False) for cols in pairs]
              + [(bwd, hb_ref, yb_ref, b_rows, cols, True) for cols in pairs])
    head = lax.broadcasted_iota(jnp.int32, (L, PAIR_W), 1) >> 6
    for first in range(0, n_chunks, SCAN_ROUND):
        steps = range(first, first + SCAN_ROUND)
        jobs = [(chain, i) for i in steps for chain in chains]
        pre = _lockstep([_scan_chunk(*refs, rows[i], cols, rev) for (refs, _, _, rows, cols, rev), i in jobs])
        for ((_, h_ref, y_ref, rows, cols, _), i), (q_hat, y_c, m_b, c_m) in zip(jobs, pre):
            both = _dot(jnp.concatenate([q_hat, m_b], axis=0), _bd(_bf(h_ref[:, cols]), head))
            y_ref[rows[i], cols] = (both[0:L] + y_c).astype(y_ref.dtype)
            h_ref[:, cols] = both[L:2 * L] + c_m


def _rwkv_scan(r, v, kk, lw, kd, ag, n_ctx):
    B, N, _ = r.shape
    nt = N // TM
    nc = n_ctx // TM

    def back(s):
        return jnp.where(s < nc, nc - 1 - s, nt + nc - 1 - s)

    f_one = pl.BlockSpec((None, TM, SCAN_W), lambda b, g, s: (b, s, g))
    b_one = pl.BlockSpec((None, TM, SCAN_W), lambda b, g, s: (b, back(s), g))
    f_two = pl.BlockSpec((None, None, TM, SCAN_W), lambda b, g, s: (0, b, s, g))
    b_two = pl.BlockSpec((None, None, TM, SCAN_W), lambda b, g, s: (1, b, back(s), g))
    out = jax.ShapeDtypeStruct((B, N, D), BF16)
    return pl.pallas_call(
        _scan_kernel,
        grid=(B, D // SCAN_W, nt),
        in_specs=[f_one, f_one, f_one, f_two, f_two, f_two, b_one, b_one, b_one, b_two, b_two, b_two],
        out_specs=[f_one, b_one],
        out_shape=[out, out],
        scratch_shapes=[pltpu.VMEM((RW_HEAD, SCAN_W), F32), pltpu.VMEM((RW_HEAD, SCAN_W), F32)],
        compiler_params=_params(3),
        name="rwkv_scan",
    )(r, v, kk, lw, kd, ag, r, v, kk, lw, kd, ag)


def _rwread_kernel(yf_ref, yb_ref, r_ref, v_ref, g_ref, kd0_ref, kd1_ref, vec_ref, e_ref, o_ref):
    inv = 1.0 / RW_HEAD
    y = yf_ref[...].astype(F32) + yb_ref[...].astype(F32)
    mu = _head_sum(y, e_ref) * inv
    yc = y - mu
    var = _head_sum(yc * yc, e_ref) * inv
    yn = yc * lax.rsqrt(var + RW_GN_EPS) * vec_ref[1:2, :] + vec_ref[2:3, :]
    f32 = lambda ref: ref[...].astype(F32)
    bonus = _head_sum(f32(r_ref) * (f32(kd0_ref) + f32(kd1_ref)) * vec_ref[0:1, :], e_ref) * f32(v_ref)
    o_ref[...] = _bf((yn + bonus) * f32(g_ref))


def _rwkv_readout(yf, yb, r, v, g, kd, vec, e, latent_only, n_ctx):
    B, N, _ = r.shape
    tm = TM if latent_only else TM * max(s for s in (3, 2, 1) if N % (s * TM) == 0)
    t0 = n_ctx // tm if latent_only else 0
    nt = N // tm - t0
    one = pl.BlockSpec((None, tm, D), lambda b, i: (b, i + t0, 0))
    pick = lambda d: pl.BlockSpec((None, None, tm, D), lambda b, i: (d, b, i + t0, 0))
    return pl.pallas_call(
        _rwread_kernel,
        grid=(B, nt),
        in_specs=[one, one, one, one, one, pick(0), pick(1), _resident(vec.shape), _resident(e.shape)],
        out_specs=pl.BlockSpec((None, tm, D), lambda b, i: (b, i, 0)),
        out_shape=jax.ShapeDtypeStruct((B, nt * tm, D), BF16),
        compiler_params=_params(2),
        name="rwkv_readout",
    )(yf, yb, r, v, g, kd, kd, vec, e)


def _rope_tables(n_lat, n_ctx):
    t = jnp.arange(n_lat)
    rowp = (t // GRID_W).astype(F32)
    colp = (t % GRID_W).astype(F32)

    def table(dim):
        n_freq = dim // 4
        inv = ROPE_BASE ** (-jnp.arange(n_freq, dtype=F32) / n_freq)
        ang = jnp.concatenate([rowp[:, None] * inv, colp[:, None] * inv], -1)
        cos, sin = jnp.cos(ang), jnp.sin(ang)
        reps = 128 // dim
        cos_t = jnp.tile(jnp.concatenate([cos, cos], -1), (1, reps))
        sin_t = jnp.tile(jnp.concatenate([-sin, sin], -1), (1, reps))
        cos_t = jnp.concatenate([jnp.ones((n_ctx, 128), F32), cos_t], 0)
        sin_t = jnp.concatenate([jnp.zeros((n_ctx, 128), F32), sin_t], 0)
        return cos_t, sin_t

    ca, sa = table(DA_QK)
    cb, sb = table(RT_QK)
    return ca, sa, cb, sb


def _pad_cols(w, n):
    return jnp.pad(w, ((0, 0), (0, n - w.shape[1])))


def _pad_rows(w, n):
    return jnp.pad(w, ((0, n - w.shape[0]), (0, 0)))


def kernel(x, c, ctx, c_ctx, mod_w, mod_b, ln1_g, ln1_b, ln2_g, ln2_b, ffn_w1, ffn_w3, ffn_w2, ev_w_in, ev_w_out, da_lam_q1, da_lam_k1, da_lam_q2, da_lam_k2, da_gn_g, rt_decay_logit, rw_mu, rw_wr, rw_wk, rw_wv, rw_wo, rw_w0, rw_w1, rw_w2, rw_a0, rw_a1, rw_a2, rw_v0, rw_v1, rw_v2, rw_g1, rw_g2, rw_kk, rw_ka, rw_rk, rw_lnx_g, rw_lnx_b):
    B, T, _ = x.shape
    n_ctx = ctx.shape[1]
    assert n_ctx == TM and T % TM == 0 and x.shape[2] == D and B <= 15
    xs = jnp.concatenate([ctx, x], axis=1)

    cc = jnp.zeros((16, D), F32).at[:B].set(c).at[B].set(c_ctx)
    mod_all = _adaln(cc, mod_w, mod_b)
    m_lat = mod_all[:, :B].reshape(DEPTH, B, 1, 6, D)
    m_ctx = jnp.broadcast_to(mod_all[:, B].reshape(DEPTH, 1, 1, 6, D), (DEPTH, B, 1, 6, D))
    mod_tab = jnp.concatenate([m_ctx, m_lat], axis=2)

    tabs = _rope_tables(T, n_ctx)
    lane_head = np.arange(HEADS4_W) // RW_HEAD
    head_ones = jnp.asarray(lane_head[:, None] == lane_head[None, :], BF16)

    vf = None
    for l in range(DEPTH):
        last = l == DEPTH - 1
        mod = mod_tab[l]
        ln = jnp.stack([ln1_g[l], ln1_b[l], ln2_g[l], ln2_b[l]])
        if l % 2 == 0:
            e = l // 2
            lam_init = 0.8 - 0.6 * math.exp(-0.3 * l)
            lam = (jnp.exp(jnp.sum(da_lam_q1[e] * da_lam_k1[e])) - jnp.exp(jnp.sum(da_lam_q2[e] * da_lam_k2[e]))
                   + lam_init).reshape(1).astype(F32)
            gn = (da_gn_g[e] * (1.0 - lam_init)).reshape(1, -1)
            log_gamma = jnp.log(jax.nn.sigmoid(rt_decay_logit[e].astype(F32)))
            aq, ak, av, bq, bk, bv, bg = _even_project(xs, mod, _bf(ev_w_in[e]), tabs)
            a_mix = _diff_attention(aq, ak, av, lam, gn, n_ctx)
            b_mix = _retention(bq, bk, bv, bg, log_gamma, n_ctx)
            mixes = (a_mix, b_mix)
            wo = _bf(ev_w_out[e])
        else:
            j = l // 2
            has_vres = j > 0
            pvec = jnp.zeros((16, D), F32)
            pvec = pvec.at[PV_MU:PV_MU + 6].set(rw_mu[j]).at[PV_W0:PV_W0 + 2].set(0.5 * rw_w0[j])
            pvec = pvec.at[PV_A0:PV_A0 + 2].set(0.5 * rw_a0[j]).at[PV_KK].set(rw_kk[j])
            pvec = pvec.at[PV_KA0].set(1.0 - 0.5 * rw_ka[j]).at[PV_KA1].set(0.5 * rw_ka[j])
            p = {
                "wr": _bf(rw_wr[j]), "wk": _bf(rw_wk[j]), "wv": _bf(rw_wv[j]),
                "w1": _bf(jnp.concatenate([_pad_cols(rw_w1[j, d], LORA_PAD) for d in range(2)], 1)),
                "w2": _bf(jnp.stack([_pad_rows(0.5 * rw_w2[j, d], LORA_PAD) for d in range(2)])),
                "a1": _bf(jnp.concatenate([_pad_cols(rw_a1[j, d], LORA_PAD) for d in range(2)], 1)),
                "a2": _bf(jnp.stack([_pad_rows(0.5 * rw_a2[j, d], LORA_PAD) for d in range(2)])),
                "g1": _bf(_pad_cols(rw_g1[j], GATE_PAD)), "g2": _bf(_pad_rows(rw_g2[j], GATE_PAD)),
                "e": head_ones,
            }
            if has_vres:
                pvec = pvec.at[PV_V0].set(rw_v0[j - 1])
                p["v1"] = _bf(_pad_cols(rw_v1[j - 1], LORA_PAD))
                p["v2"] = _bf(_pad_rows(rw_v2[j - 1], LORA_PAD))
            p["pv"] = pvec
            r, v, g, kk, lw, kd, ag = _rwkv_project(xs, mod, vf if has_vres else None, p)
            if not has_vres:
                vf = v
            yf, yb = _rwkv_scan(r, v, kk, lw, kd, ag, n_ctx)
            vec = jnp.zeros((8, D), F32).at[0].set(rw_rk[j].reshape(-1)).at[1].set(rw_lnx_g[j]).at[2].set(rw_lnx_b[j])
            mixes = (_rwkv_readout(yf, yb, r, v, g, kd, vec, head_ones, last, n_ctx),)
            wo = _bf(rw_wo[j])
        xs = _post_mixer(xs, mod, mixes, wo, _bf(ffn_w1[l]), _bf(ffn_w3[l]), _bf(ffn_w2[l]), ln, last, n_ctx)
    return xs
```
